```python
import math
import jax
import jax.numpy as jnp
from jax import lax
import numpy as np

D_MODEL = 1024
BATCH = 2
SEQ = 16384
DEPTH = 4
DEC_BATCH = 8
DEC_SEQ = 64
PAST_LEN = 1024

CHUNK = 64
N_META = 16
Q_BLOCK = 128

FOX_HEADS = 8
FOX_HEAD_DIM = 64
FOX_WIDTH = FOX_HEADS * FOX_HEAD_DIM
FOX_SCALE = FOX_HEAD_DIM ** -0.5
FGT_BIAS_LO = 2.0
FGT_BIAS_HI = 7.0

MLA_HEADS = 8
MLA_NOPE_DIM = 64
MLA_ROPE_DIM = 32
MLA_QK_DIM = MLA_NOPE_DIM + MLA_ROPE_DIM
MLA_V_DIM = 64
MLA_Q_RANK = 256
MLA_KV_RANK = 128
MLA_WIDTH = MLA_HEADS * MLA_V_DIM
MLA_SCALE = MLA_QK_DIM ** -0.5
ROPE_THETA = 10000.0

IN_SIZES = (FOX_WIDTH, FOX_WIDTH, FOX_WIDTH, FOX_HEADS, MLA_Q_RANK, MLA_KV_RANK, MLA_ROPE_DIM, 2 * D_MODEL)
IN_WIDTH = sum(IN_SIZES)
IN_OFFSETS = tuple(sum(IN_SIZES[:i]) for i in range(1, len(IN_SIZES)))

N_EXPERTS = 16
N_GROUPS = 4
EXPERTS_PER_GROUP = N_EXPERTS // N_GROUPS
TOP_K = 2
D_EXPERT = 512

ALPHA = (2 * DEPTH) ** 0.25
BETA = (8 * DEPTH) ** -0.25
LN_EPS = 1e-5
RMS_EPS = 1e-6

kernel_name = 'fox_mla_gated_moe_stream_step'


def layer_norm(x, g, b):
    xf = x.astype(jnp.float32)
    mu = jnp.mean(xf, axis=-1, keepdims=True)
    var = jnp.mean(jnp.square(xf - mu), axis=-1, keepdims=True)
    y = (xf - mu) * lax.rsqrt(var + LN_EPS) * g.astype(jnp.float32) + b.astype(jnp.float32)
    return y.astype(x.dtype)


def rms_norm(x, g):
    xf = x.astype(jnp.float32)
    y = xf * lax.rsqrt(jnp.mean(jnp.square(xf), axis=-1, keepdims=True) + RMS_EPS) * g.astype(jnp.float32)
    return y.astype(x.dtype)


def rope(x, pos):
    half = MLA_ROPE_DIM // 2
    inv_freq = ROPE_THETA ** (-jnp.arange(half, dtype=jnp.float32) / half)
    ang = pos.astype(jnp.float32)[:, None] * inv_freq[None, :]
    if x.ndim == 4:
        ang = ang[:, None, :]
    cos, sin = jnp.cos(ang), jnp.sin(ang)
    xf = x.astype(jnp.float32)
    x1, x2 = xf[..., :half], xf[..., half:]
    return jnp.concatenate([x1 * cos - x2 * sin, x2 * cos + x1 * sin], axis=-1).astype(x.dtype)


def _attend(q, k, v, q_ids, k_ids, scale, q_bias, k_bias):
    s = jnp.einsum('bqhd,bkhd->bhqk', q, k, preferred_element_type=jnp.float32) * scale
    if q_bias is not None:
        s = s + (jnp.swapaxes(q_bias, 1, 2)[..., :, None] - jnp.swapaxes(k_bias, 1, 2)[..., None, :])
    allowed = k_ids[None, :] <= q_ids[:, None]
    s = jnp.where(allowed, s, -jnp.inf)
    p = jax.nn.softmax(s, axis=-1)
    return jnp.einsum('bhqk,bkhd->bqhd', p.astype(v.dtype), v)


def block_attention(q, k, v, q_ids, k_ids, scale, q_bias=None, k_bias=None):
    B, Tq, H, _ = q.shape
    if Tq <= Q_BLOCK:
        return _attend(q, k, v, q_ids, k_ids, scale, q_bias, k_bias)
    n_blocks = Tq // Q_BLOCK

    def one_block(i):
        start = i * Q_BLOCK
        qb = lax.dynamic_slice_in_dim(q, start, Q_BLOCK, axis=1)
        ib = lax.dynamic_slice_in_dim(q_ids, start, Q_BLOCK, axis=0)
        bb = None if q_bias is None else lax.dynamic_slice_in_dim(q_bias, start, Q_BLOCK, axis=1)
        return _attend(qb, k, v, ib, k_ids, scale, bb, k_bias)

    out = lax.map(one_block, jnp.arange(n_blocks))
    return jnp.moveaxis(out, 0, 1).reshape(B, Tq, H, v.shape[-1])


def token_mixer(h, rope_pos, fox_q_ids, fox_k_ids, q_chunk, k_chunk, past, lp):
    B, T, _ = h.shape
    proj = h @ lp['w_in']
    fq, fk, fv, f_logit, c_q, c_kv, k_r, g = jnp.split(proj, IN_OFFSETS, axis=-1)
    fox_shape = (B, T, FOX_HEADS, FOX_HEAD_DIM)
    q_f, k_f, v_f = fq.reshape(fox_shape), fk.reshape(fox_shape), fv.reshape(fox_shape)
    logf = jax.nn.log_sigmoid(f_logit.astype(jnp.float32) + lp['b_fgt'].astype(jnp.float32))
    q_m = (rms_norm(c_q, lp['g_q_norm']) @ lp['w_q_up']).reshape(B, T, MLA_HEADS, MLA_QK_DIM)
    q_m = jnp.concatenate([q_m[..., :MLA_NOPE_DIM], rope(q_m[..., MLA_NOPE_DIM:], rope_pos)], axis=-1)
    ckv = rms_norm(c_kv, lp['g_kv_norm'])
    k_rope = rope(k_r, rope_pos)
    new_rows = (k_f, v_f, logf, ckv, k_rope)
    if past is None:
        k_all, v_all, logf_all, ckv_all, kr_all = new_rows
    else:
        k_all, v_all, logf_all, ckv_all, kr_all = (jnp.concatenate([p_, n_], axis=1) for p_, n_ in zip(past, new_rows))
    cum = jnp.cumsum(logf_all.astype(jnp.float32), axis=1)
    fox = block_attention(q_f, k_all, v_all, fox_q_ids, fox_k_ids, FOX_SCALE, cum[:, -T:], cum)
    Tk = ckv_all.shape[1]
    kv = (ckv_all @ lp['w_kv_up']).reshape(B, Tk, MLA_HEADS, MLA_NOPE_DIM + MLA_V_DIM)
    k_m = jnp.concatenate([kv[..., :MLA_NOPE_DIM], jnp.broadcast_to(kr_all[:, :, None, :], (B, Tk, MLA_HEADS, MLA_ROPE_DIM)).astype(kv.dtype)], axis=-1)
    mla = block_attention(q_m, k_m, kv[..., MLA_NOPE_DIM:], q_chunk, k_chunk, MLA_SCALE)
    g_a, g_b = jnp.split(jax.nn.sigmoid(g + lp['b_gate']), 2, axis=-1)
    merged = g_a * (fox.reshape(B, T, FOX_WIDTH) @ lp['w_proj_fox']) + g_b * (mla.reshape(B, T, MLA_WIDTH) @ lp['w_proj_mla'])
    return merged @ lp['w_out'], new_rows


def routed_moe(h, w_router, b_router, w_gate, w_up, w_down):
    B, T, D = h.shape
    xt = h.reshape(B * T, D)
    scores = jax.nn.sigmoid((xt @ w_router).astype(jnp.float32))
    sel = scores + b_router.astype(jnp.float32)
    grouped = sel.reshape(-1, N_GROUPS, EXPERTS_PER_GROUP)
    group_score = jnp.sum(lax.top_k(grouped, TOP_K)[0], axis=-1)
    best_group = jnp.argmax(group_score, axis=-1)
    in_group = (jnp.arange(N_EXPERTS) // EXPERTS_PER_GROUP)[None, :] == best_group[:, None]
    _, top_idx = lax.top_k(jnp.where(in_group, sel, -jnp.inf), TOP_K)
    top_w = jnp.take_along_axis(scores, top_idx, axis=-1)
    top_w = top_w / jnp.sum(top_w, axis=-1, keepdims=True)
    gates = jnp.sum(jax.nn.one_hot(top_idx, N_EXPERTS, dtype=jnp.float32) * top_w[..., None], axis=1).astype(xt.dtype)
    y = jnp.zeros_like(xt)
    for e in range(N_EXPERTS):
        hid = jax.nn.silu(xt @ w_gate[e]) * (xt @ w_up[e])
        y = y + gates[:, e:e + 1] * (hid @ w_down[e])
    return y.reshape(B, T, D)


def trunk(x, rope_pos, fox_q_ids, fox_k_ids, q_chunk, k_chunk, caches, layers, w_router, b_router):
    rows = []
    for l in range(DEPTH):
        lp = layers[l]
        past = None if caches is None else tuple(c[l] for c in caches)
        mix, new_rows = token_mixer(x, rope_pos, fox_q_ids, fox_k_ids, q_chunk, k_chunk, past, lp)
        x = layer_norm(ALPHA * x + mix, lp['ln1_g'], lp['ln1_b'])
        ffn = routed_moe(x, w_router, b_router, lp['w_exp_gate'], lp['w_exp_up'], lp['w_exp_down'])
        x = layer_norm(ALPHA * x + ffn, lp['ln2_g'], lp['ln2_b'])
        rows.append(new_rows)
    stacked = tuple(jnp.stack([r[i] for r in rows], axis=0) for i in range(len(rows[0])))
    return x, stacked


def setup_inputs(seed: int = 0) -> dict:
    key = jax.random.key(seed)
    keys = iter(jax.random.split(key, 32))

    def nrm(shape, scale=1.0):
        return jax.random.normal(next(keys), shape, jnp.float32) * scale

    def gain(shape):
        return 1.0 + nrm(shape, 0.02)

    fgt_profile = jnp.linspace(FGT_BIAS_LO, FGT_BIAS_HI, FOX_HEADS, dtype=jnp.float32)
    D, E, F = D_MODEL, N_EXPERTS, D_EXPERT
    return {
        'x_prompt': nrm((BATCH, SEQ, D)),
        'x_sample': nrm((DEC_BATCH, DEC_SEQ, D)),
        'cache_fox_k': nrm((DEPTH, DEC_BATCH, PAST_LEN, FOX_HEADS, FOX_HEAD_DIM)),
        'cache_fox_v': nrm((DEPTH, DEC_BATCH, PAST_LEN, FOX_HEADS, FOX_HEAD_DIM)),
        'cache_fox_logf': jax.nn.log_sigmoid(fgt_profile + nrm((DEPTH, DEC_BATCH, PAST_LEN, FOX_HEADS))),
        'cache_mla_ckv': nrm((DEPTH, DEC_BATCH, PAST_LEN, MLA_KV_RANK)),
        'cache_mla_krope': nrm((DEPTH, DEC_BATCH, PAST_LEN, MLA_ROPE_DIM)),
        'meta_tokens': nrm((N_META, D)),
        'ln_in_g': gain((D,)),
        'ln_in_b': nrm((D,), 0.02),
        'w_in': nrm((DEPTH, D, IN_WIDTH), D ** -0.5),
        'b_fgt': fgt_profile[None, :] + nrm((DEPTH, FOX_HEADS), 0.1),
        'g_q_norm': gain((DEPTH, MLA_Q_RANK)),
        'w_q_up': nrm((DEPTH, MLA_Q_RANK, MLA_HEADS * MLA_QK_DIM), MLA_Q_RANK ** -0.5),
        'g_kv_norm': gain((DEPTH, MLA_KV_RANK)),
        'w_kv_up': nrm((DEPTH, MLA_KV_RANK, MLA_HEADS * (MLA_NOPE_DIM + MLA_V_DIM)), MLA_KV_RANK ** -0.5),
        'w_proj_fox': nrm((DEPTH, FOX_WIDTH, D), FOX_WIDTH ** -0.5),
        'w_proj_mla': nrm((DEPTH, MLA_WIDTH, D), MLA_WIDTH ** -0.5),
        'b_gate': nrm((DEPTH, 2 * D), 0.02),
        'w_out': nrm((DEPTH, D, D), BETA * D ** -0.5),
        'ln1_g': gain((DEPTH, D)),
        'ln1_b': nrm((DEPTH, D), 0.02),
        'ln2_g': gain((DEPTH, D)),
        'ln2_b': nrm((DEPTH, D), 0.02),
        'w_router': nrm((D, E), D ** -0.5),
        'b_router': nrm((E,), 0.01),
        'w_exp_gate': nrm((DEPTH, E, D, F), D ** -0.5),
        'w_exp_up': nrm((DEPTH, E, D, F), D ** -0.5),
        'w_exp_down': nrm((DEPTH, E, F, D), BETA * F ** -0.5),
    }


def reference(x_prompt, x_sample, cache_fox_k, cache_fox_v, cache_fox_logf, cache_mla_ckv, cache_mla_krope, meta_tokens, ln_in_g, ln_in_b, w_in, b_fgt, g_q_norm, w_q_up, g_kv_norm, w_kv_up, w_proj_fox, w_proj_mla, b_gate, w_out, ln1_g, ln1_b, ln2_g, ln2_b, w_router, b_router, w_exp_gate, w_exp_up, w_exp_down):
    layers = [dict(w_in=w_in[l], b_fgt=b_fgt[l], g_q_norm=g_q_norm[l], w_q_up=w_q_up[l], g_kv_norm=g_kv_norm[l], w_kv_up=w_kv_up[l], w_proj_fox=w_proj_fox[l], w_proj_mla=w_proj_mla[l], b_gate=b_gate[l], w_out=w_out[l], ln1_g=ln1_g[l], ln1_b=ln1_b[l], ln2_g=ln2_g[l], ln2_b=ln2_b[l], w_exp_gate=w_exp_gate[l], w_exp_up=w_exp_up[l], w_exp_down=w_exp_down[l]) for l in range(DEPTH)]

    B, S, D = x_prompt.shape
    T = N_META + S
    Tp = -(-T // Q_BLOCK) * Q_BLOCK
    meta = jnp.broadcast_to(meta_tokens[None].astype(x_prompt.dtype), (B, N_META, D))
    xp = jnp.concatenate([meta, x_prompt, jnp.zeros((B, Tp - T, D), x_prompt.dtype)], axis=1)
    xp = layer_norm(xp, ln_in_g, ln_in_b)
    pos = jnp.arange(Tp, dtype=jnp.int32)
    chunk_id = jnp.where(pos < N_META, 0, 1 + (pos - N_META) // CHUNK)
    yp, rows_p = trunk(xp, pos, pos, pos, chunk_id, chunk_id, None, layers, w_router, b_router)
    y_prompt = yp[:, N_META:T]
    new_fox_k_p, new_fox_v_p, new_fox_logf_p, new_ckv_p, new_krope_p = (r[:, :, :T] for r in rows_p)

    P = cache_fox_k.shape[2]
    Tn = x_sample.shape[1]
    xs = layer_norm(x_sample, ln_in_g, ln_in_b)
    k_pos = jnp.arange(P + Tn, dtype=jnp.int32)
    q_pos = k_pos[P:]
    caches = (cache_fox_k, cache_fox_v, cache_fox_logf, cache_mla_ckv, cache_mla_krope)
    y_sample, rows_s = trunk(xs, q_pos, q_pos, k_pos, q_pos // CHUNK, k_pos // CHUNK, caches, layers, w_router, b_router)
    new_fox_k_s, new_fox_v_s, new_fox_logf_s, new_ckv_s, new_krope_s = rows_s

    return (y_prompt, y_sample, new_fox_k_p, new_fox_v_p, new_fox_logf_p, new_ckv_p, new_krope_p, new_fox_k_s, new_fox_v_s, new_fox_logf_s, new_ckv_s, new_krope_s)
```

```python
import functools
import math

import jax
import jax.numpy as jnp
from jax import lax
from jax.experimental import pallas as pl
from jax.experimental.pallas import tpu as pltpu

F32 = jnp.float32
BF16 = jnp.bfloat16

CHUNK = 64
N_META = 16
FOX_HEADS = 8
FOX_HEAD_DIM = 64
FOX_WIDTH = FOX_HEADS * FOX_HEAD_DIM
MLA_HEADS = 8
MLA_NOPE_DIM = 64
MLA_ROPE_DIM = 32
MLA_QK_DIM = MLA_NOPE_DIM + MLA_ROPE_DIM
MLA_V_DIM = 64
N_EXPERTS = 16
N_GROUPS = 4
EXPERTS_PER_GROUP = N_EXPERTS // N_GROUPS
ROPE_THETA = 10000.0
LN_EPS = 1e-5
RMS_EPS = 1e-6

LANES = 128
ROW_TILE = 256
MOE_ROW_TILE = 512
ATTN_TILE = 256
SAMPLE_KEY_TILE = 128
MASK_VALUE = -1e30
VMEM_LIMIT = 56 * 1024 * 1024

C_QKV = 0
C_CQ = 3 * FOX_WIDTH
C_CKV = C_CQ + 256
C_LOGF = C_CKV + 128
C_KR = C_LOGF + LANES
C_KRS = C_KR + LANES
C_GATE = C_KRS + LANES


def _params(*sem):
    return pltpu.CompilerParams(dimension_semantics=sem, vmem_limit_bytes=VMEM_LIMIT)


def _layer_norm(x, g, b):
    mu = jnp.mean(x, axis=-1, keepdims=True)
    xc = x - mu
    var = jnp.mean(xc * xc, axis=-1, keepdims=True)
    return xc * lax.rsqrt(var + LN_EPS) * g + b


def _rms_norm(x, g):
    return x * lax.rsqrt(jnp.mean(x * x, axis=-1, keepdims=True) + RMS_EPS) * g


def _dot(a, b):
    return jnp.dot(a, b, preferred_element_type=F32)


def _split3(x):
    hi = x.astype(BF16)
    r1 = x - hi.astype(F32)
    mid = r1.astype(BF16)
    lo = (r1 - mid.astype(F32)).astype(BF16)
    return hi, mid, lo


def _ln_kernel(x_ref, g_ref, b_ref, o_ref):
    o_ref[...] = _layer_norm(x_ref[...], g_ref[...], b_ref[...])


def _input_ln(x, g, b):
    n, d = x.shape
    return pl.pallas_call(
        _ln_kernel,
        grid=(n // ROW_TILE,),
        in_specs=[
            pl.BlockSpec((ROW_TILE, d), lambda i: (i, 0)),
            pl.BlockSpec((1, d), lambda i: (0, 0)),
            pl.BlockSpec((1, d), lambda i: (0, 0)),
        ],
        out_specs=pl.BlockSpec((ROW_TILE, d), lambda i: (i, 0)),
        out_shape=jax.ShapeDtypeStruct((n, d), F32),
        compiler_params=_params("parallel"),
        name="input_ln",
    )(x, g, b)


def _pre_kernel(
    x_ref, w_ref, bf_ref, gq_ref, wq_ref, gkv_ref, wkv_ref, bg_ref, cos_ref, sin_ref,
    qf_ref, kf32_ref, vf32_ref, kf_ref, vf_ref, logf_ref, qn_ref, qr_ref, ckv_ref, kr32_ref, kr_ref,
    kn_ref, vm_ref, gate_ref, *, fox_scale, mla_scale,
):
    xb = x_ref[...].astype(BF16)
    cos = cos_ref[...]
    sin = sin_ref[...]

    qkv = _dot(xb, w_ref[:, C_QKV:C_CQ])
    qf_ref[...] = (qkv[:, 0:FOX_WIDTH] * fox_scale).astype(BF16)
    k = qkv[:, FOX_WIDTH:2 * FOX_WIDTH]
    v = qkv[:, 2 * FOX_WIDTH:3 * FOX_WIDTH]
    kf32_ref[...] = k
    vf32_ref[...] = v
    kf_ref[...] = k.astype(BF16)
    vf_ref[...] = v.astype(BF16)

    z = _dot(xb, w_ref[:, C_LOGF:C_KR]) + bf_ref[...]
    logf_ref[...] = jnp.minimum(z, 0.0) - jnp.log1p(jnp.exp(-jnp.abs(z)))

    cq = _rms_norm(_dot(xb, w_ref[:, C_CQ:C_CKV]), gq_ref[...]).astype(BF16)
    qm = _dot(cq, wq_ref[...])
    qn_ref[...] = (qm[:, 0:512] * mla_scale).astype(BF16)
    cos2 = jnp.concatenate([cos, cos], axis=1)
    sin2 = jnp.concatenate([sin, sin], axis=1)
    qr_ref[...] = ((qm[:, 512:768] * cos2 + qm[:, 768:1024] * sin2) * mla_scale).astype(BF16)

    ckv = _rms_norm(_dot(xb, w_ref[:, C_CKV:C_LOGF]), gkv_ref[...])
    ckv_ref[...] = ckv
    kv = _dot(ckv.astype(BF16), wkv_ref[...])
    kn_ref[...] = kv[:, 0:512].astype(BF16)
    vm_ref[...] = kv[:, 512:1024].astype(BF16)
    kr = _dot(xb, w_ref[:, C_KR:C_KRS]) * cos + _dot(xb, w_ref[:, C_KRS:C_GATE]) * sin
    kr32_ref[...] = kr
    kr_ref[...] = kr.astype(BF16)

    g = _dot(xb, w_ref[:, C_GATE:]) + bg_ref[...]
    gate_ref[...] = jax.nn.sigmoid(g).astype(BF16)


def _pre_attention(x, w, bf, gq, wq, gkv, wkv, bg, cos, sin):
    n, d = x.shape
    tm = ROW_TILE
    row = lambda width: pl.BlockSpec((tm, width), lambda i: (i, 0))
    full = lambda a: pl.BlockSpec(a.shape, lambda i: (0,) * a.ndim)
    outs = [
        ("qf", 512, BF16), ("kf32", 512, F32), ("vf32", 512, F32), ("kf", 512, BF16), ("vf", 512, BF16),
        ("logf", LANES, F32), ("qn", 512, BF16), ("qr", 256, BF16), ("ckv", 128, F32), ("kr32", LANES, F32),
        ("kr", LANES, BF16), ("kn", 512, BF16), ("vm", 512, BF16), ("gate", 2 * d, BF16),
    ]
    res = pl.pallas_call(
        functools.partial(_pre_kernel, fox_scale=FOX_HEAD_DIM ** -0.5, mla_scale=MLA_QK_DIM ** -0.5),
        grid=(n // tm,),
        in_specs=[row(d), full(w), full(bf), full(gq), full(wq), full(gkv), full(wkv), full(bg), row(LANES), row(LANES)],
        out_specs=[row(wd) for _, wd, _ in outs],
        out_shape=[jax.ShapeDtypeStruct((n, wd), dt) for _, wd, dt in outs],
        compiler_params=_params("parallel"),
        name="pre_attention",
    )(x, w, bf, gq, wq, gkv, wkv, bg, cos, sin)
    return {name: r for (name, _, _), r in zip(outs, res)}


def _kvup_kernel(c_ref, w_ref, kn_ref, vm_ref):
    kv = _dot(c_ref[0].astype(BF16), w_ref[0])
    kn_ref[0] = kv[:, 0:512].astype(BF16)
    vm_ref[0] = kv[:, 512:1024].astype(BF16)


def _kv_up(ckv, wkv):
    depth, n, r = ckv.shape
    tm = ROW_TILE
    return pl.pallas_call(
        _kvup_kernel,
        grid=(depth, n // tm),
        in_specs=[pl.BlockSpec((1, tm, r), lambda l, i: (l, i, 0)), pl.BlockSpec((1, r, 1024), lambda l, i: (l, 0, 0))],
        out_specs=[pl.BlockSpec((1, tm, 512), lambda l, i: (l, i, 0))] * 2,
        out_shape=[jax.ShapeDtypeStruct((depth, n, 512), BF16)] * 2,
        compiler_params=_params("parallel", "parallel"),
        name="kv_up_cached",
    )(ckv, wkv)


def _cumsum_kernel(x_ref, o_ref, carry_ref, *, tt):
    @pl.when(pl.program_id(1) == 0)
    def _():
        carry_ref[...] = jnp.zeros_like(carry_ref)

    r = lax.broadcasted_iota(jnp.int32, (tt, tt), 0)
    c = lax.broadcasted_iota(jnp.int32, (tt, tt), 1)
    tri = jnp.where(c <= r, 1.0, 0.0).astype(BF16)
    hi, mid, lo = _split3(x_ref[0])
    cum = _dot(tri, lo) + _dot(tri, mid) + _dot(tri, hi) + carry_ref[...]
    o_ref[0] = cum
    carry_ref[...] = cum[tt - 1:tt, :]


def _cumsum_time(x, tt):
    s, t, w = x.shape
    return pl.pallas_call(
        functools.partial(_cumsum_kernel, tt=tt),
        grid=(s, t // tt),
        in_specs=[pl.BlockSpec((1, tt, w), lambda b, i: (b, i, 0))],
        out_specs=pl.BlockSpec((1, tt, w), lambda b, i: (b, i, 0)),
        out_shape=jax.ShapeDtypeStruct((s, t, w), F32),
        scratch_shapes=[pltpu.VMEM((1, w), F32)],
        compiler_params=_params("parallel", "arbitrary"),
        name="cumsum_time",
    )(x)


def _chunk_end(pos, n_meta):
    body = n_meta + ((pos - n_meta) // CHUNK) * CHUNK + CHUNK - 1
    if n_meta == 0:
        return body
    return jnp.where(pos < n_meta, n_meta - 1, body)


def _chunk_id(pos, n_meta):
    return jnp.where(pos < n_meta, 0, 1 + (pos - n_meta) // CHUNK)


def _attn_kernel(*refs, fox, tq, tk, q_off, n_meta, n_ktiles):
    if fox:
        q_ref, k_ref, v_ref, cq_ref, ck_ref, o_ref = refs
    else:
        q_ref, qr_ref, k_ref, kr_ref, v_ref, o_ref = refs
    p = pl.program_id(1)
    i = pl.program_id(2)
    q_lo = q_off + i * tq
    q_hi = q_lo + tq - 1
    if fox:
        vis_lo, vis_hi = q_lo, q_hi
    else:
        vis_lo, vis_hi = _chunk_end(q_lo, n_meta), _chunk_end(q_hi, n_meta)
    n_full = jnp.minimum((vis_lo + 1) // tk, n_ktiles)
    n_tot = jnp.minimum(vis_hi // tk + 1, n_ktiles)

    lane = lax.broadcasted_iota(jnp.int32, (1, LANES), 1)
    q_pos = q_lo + lax.broadcasted_iota(jnp.int32, (tq, 1), 0)
    q2 = q_ref[0]
    outs = []
    for hh in range(2):
        qh = jnp.where(lane // FOX_HEAD_DIM == hh, q2, jnp.zeros_like(q2))
        if fox:
            h_lane = 2 * p + hh
            cq = jnp.sum(jnp.where(lane == h_lane, cq_ref[0], 0.0), axis=1, keepdims=True)
        else:
            r_group = 2 * (p % 2) + hh
            qr2 = qr_ref[0]
            qh = jnp.concatenate([qh, jnp.where(lane // MLA_ROPE_DIM == r_group, qr2, jnp.zeros_like(qr2))], axis=1)

        def step(j, carry, masked):
            m, l, acc = carry
            ks = pl.ds(pl.multiple_of(j * tk, tk), tk)
            kt = k_ref[0, ks, :]
            if not fox:
                kt = jnp.concatenate([kt, kr_ref[0, ks, :]], axis=1)
            s = lax.dot_general(qh, kt, (((1,), (1,)), ((), ())), preferred_element_type=F32)
            if fox:
                s = s + cq - ck_ref[0, 0, hh:hh + 1, ks]
            if masked:
                k_pos = j * tk + lax.broadcasted_iota(jnp.int32, (1, tk), 1)
                if fox:
                    ok = k_pos <= q_pos
                else:
                    ok = _chunk_id(k_pos, n_meta) <= _chunk_id(q_pos, n_meta)
                s = jnp.where(ok, s, MASK_VALUE)
            m_new = jnp.maximum(m, jnp.max(s, axis=1, keepdims=True))
            alpha = jnp.exp(m - m_new)
            pr = jnp.exp(s - m_new)
            l = alpha * l + jnp.sum(pr, axis=1, keepdims=True)
            acc = alpha * acc + _dot(pr.astype(BF16), v_ref[0, ks, :])
            return m_new, l, acc

        init = (jnp.full((tq, 1), MASK_VALUE, F32), jnp.zeros((tq, 1), F32), jnp.zeros((tq, LANES), F32))
        carry = lax.fori_loop(0, n_full, functools.partial(step, masked=False), init)
        m, l, acc = lax.fori_loop(n_full, n_tot, functools.partial(step, masked=True), carry)
        outs.append(acc / l)
    o_ref[0] = jnp.where(lane // FOX_HEAD_DIM == 0, outs[0], outs[1]).astype(o_ref.dtype)


def _attention(q, k, v, *, fox, tq, tk, q_off, n_meta, qr=None, kr=None, cq=None, ck=None):
    b, t_q, _ = q.shape
    t_k = k.shape[1]
    qspec = pl.BlockSpec((1, tq, LANES), lambda bb, p, i: (bb, i, p))
    kspec = pl.BlockSpec((1, t_k, LANES), lambda bb, p, i: (bb, 0, p))
    if fox:
        args = (q, k, v, cq, ck)
        in_specs = [
            qspec, kspec, kspec,
            pl.BlockSpec((1, tq, LANES), lambda bb, p, i: (bb, i, 0)),
            pl.BlockSpec((1, 1, 2, t_k), lambda bb, p, i: (bb, p, 0, 0)),
        ]
    else:
        args = (q, qr, k, kr, v)
        in_specs = [
            qspec,
            pl.BlockSpec((1, tq, LANES), lambda bb, p, i: (bb, i, p // 2)),
            kspec,
            pl.BlockSpec((1, t_k, LANES), lambda bb, p, i: (bb, 0, 0)),
            kspec,
        ]
    return pl.pallas_call(
        functools.partial(_attn_kernel, fox=fox, tq=tq, tk=tk, q_off=q_off, n_meta=n_meta, n_ktiles=t_k // tk),
        grid=(b, 4, t_q // tq),
        in_specs=in_specs,
        out_specs=qspec,
        out_shape=jax.ShapeDtypeStruct((b, t_q, 512), BF16),
        compiler_params=_params("parallel", "parallel", "arbitrary"),
        name="fox_attention" if fox else "mla_attention",
    )(*args)


def _group_partner(x, lane, d, width):
    fwd = pltpu.roll(x, LANES - d, 1)
    back = pltpu.roll(x, width - d, 1)
    return jnp.where(lane % width + d < width, fwd, back)


def _route(logits, b_router, lane):
    scores = jax.nn.sigmoid(logits)
    sel = scores + b_router
    e = EXPERTS_PER_GROUP
    rank = jnp.zeros(sel.shape, jnp.int32)
    for d in range(1, e):
        other = _group_partner(sel, lane, d, e)
        other_first = lane % e + d >= e
        beats = (other > sel) | ((other == sel) & other_first)
        rank = rank + beats.astype(jnp.int32)
    top2 = rank < 2
    kept = jnp.where(top2, sel, 0.0)
    gscore = kept
    for d in range(1, e):
        gscore = gscore + _group_partner(kept, lane, d, e)
    grank = jnp.zeros_like(rank)
    for gstep in range(1, N_GROUPS):
        d = gstep * e
        other = _group_partner(gscore, lane, d, N_EXPERTS)
        other_first = lane % N_EXPERTS + d >= N_EXPERTS
        beats = (other > gscore) | ((other == gscore) & other_first)
        grank = grank + beats.astype(jnp.int32)
    chosen = top2 & (grank == 0) & (lane < N_EXPERTS)
    w = jnp.where(chosen, scores, 0.0)
    return w / jnp.sum(w, axis=1, keepdims=True)


def _post_kernel(x_ref, fo_ref, mo_ref, gate_ref, pa_ref, pb_ref, wo_ref, g1_ref, b1_ref, wr_ref, br_ref,
                 x1_ref, gates_ref, *, alpha):
    d = x_ref.shape[1]
    merged = gate_ref[:, 0:d].astype(F32) * _dot(fo_ref[...], pa_ref[...])
    merged = merged + gate_ref[:, d:2 * d].astype(F32) * _dot(mo_ref[...], pb_ref[...])
    mix = _dot(merged.astype(BF16), wo_ref[...])
    x1 = _layer_norm(alpha * x_ref[...] + mix, g1_ref[...], b1_ref[...])
    x1_ref[...] = x1
    hi, mid, lo = _split3(x1)
    whi, wmid, wlo = wr_ref[0], wr_ref[1], wr_ref[2]
    logits = _dot(lo, whi) + _dot(mid, wmid) + _dot(hi, wlo) + _dot(mid, whi) + _dot(hi, wmid) + _dot(hi, whi)
    lane = lax.broadcasted_iota(jnp.int32, logits.shape, 1)
    gates_ref[...] = _route(logits, br_ref[...], lane)


def _post_attention(x, fo, mo, gate, pa, pb, wo, g1, b1, wr3, br, alpha):
    n, d = x.shape
    tm = ROW_TILE
    row = lambda width: pl.BlockSpec((tm, width), lambda i: (i, 0))
    full = lambda a: pl.BlockSpec(a.shape, lambda i: (0,) * a.ndim)
    return pl.pallas_call(
        functools.partial(_post_kernel, alpha=alpha),
        grid=(n // tm,),
        in_specs=[row(d), row(512), row(512), row(2 * d), full(pa), full(pb), full(wo), full(g1), full(b1), full(wr3), full(br)],
        out_specs=[row(d), row(LANES)],
        out_shape=[jax.ShapeDtypeStruct((n, d), F32), jax.ShapeDtypeStruct((n, LANES), F32)],
        compiler_params=_params("parallel"),
        name="post_attention",
    )(x, fo, mo, gate, pa, pb, wo, g1, b1, wr3, br)


def _moe_kernel(x_ref, gates_ref, wg_ref, wu_ref, wd_ref, g2_ref, b2_ref, o_ref, xb_ref, acc_ref, *, alpha):
    e = pl.program_id(1)

    @pl.when(e == 0)
    def _():
        xb_ref[...] = x_ref[...].astype(BF16)
        acc_ref[...] = jnp.zeros_like(acc_ref)

    lane = lax.broadcasted_iota(jnp.int32, (1, LANES), 1)
    gate = jnp.sum(jnp.where(lane == e, gates_ref[...], 0.0), axis=1, keepdims=True)
    xb = xb_ref[...]
    hid = jax.nn.silu(_dot(xb, wg_ref[0])) * _dot(xb, wu_ref[0])
    acc_ref[...] += _dot((hid * gate).astype(BF16), wd_ref[0])

    @pl.when(e == pl.num_programs(1) - 1)
    def _():
        o_ref[...] = _layer_norm(alpha * x_ref[...] + acc_ref[...], g2_ref[...], b2_ref[...])


def _moe(x, gates, wg, wu, wd, g2, b2, alpha):
    n, d = x.shape
    n_exp, _, f = wg.shape
    tm = MOE_ROW_TILE
    return pl.pallas_call(
        functools.partial(_moe_kernel, alpha=alpha),
        grid=(n // tm, n_exp),
        in_specs=[
            pl.BlockSpec((tm, d), lambda i, e: (i, 0)),
            pl.BlockSpec((tm, LANES), lambda i, e: (i, 0)),
            pl.BlockSpec((1, d, f), lambda i, e: (e, 0, 0)),
            pl.BlockSpec((1, d, f), lambda i, e: (e, 0, 0)),
            pl.BlockSpec((1, f, d), lambda i, e: (e, 0, 0)),
            pl.BlockSpec((1, d), lambda i, e: (0, 0)),
            pl.BlockSpec((1, d), lambda i, e: (0, 0)),
        ],
        out_specs=pl.BlockSpec((tm, d), lambda i, e: (i, 0)),
        out_shape=jax.ShapeDtypeStruct((n, d), F32),
        scratch_shapes=[pltpu.VMEM((tm, d), BF16), pltpu.VMEM((tm, d), F32)],
        compiler_params=_params("parallel", "arbitrary"),
        name="moe_ln2",
    )(x, gates, wg, wu, wd, g2, b2)


def _rope_tables(pos):
    half = MLA_ROPE_DIM // 2
    inv_freq = ROPE_THETA ** (-jnp.arange(half, dtype=F32) / half)
    ang = pos.astype(F32)[:, None] * inv_freq[None, :]
    cos, sin = jnp.cos(ang), jnp.sin(ang)
    cos32 = jnp.concatenate([cos, cos], axis=1)
    sin32 = jnp.concatenate([-sin, sin], axis=1)
    return jnp.tile(cos32, (1, LANES // MLA_ROPE_DIM)), jnp.tile(sin32, (1, LANES // MLA_ROPE_DIM))


def _swap_halves(w):
    half = MLA_ROPE_DIM // 2
    return jnp.concatenate([w[..., half:], w[..., :half]], axis=-1)


def _prepare_w_in(w_in):
    depth, d, _ = w_in.shape
    o = [0]
    for sz in (FOX_WIDTH, FOX_WIDTH, FOX_WIDTH, FOX_HEADS, 256, 128, MLA_ROPE_DIM, 2 * d):
        o.append(o[-1] + sz)
    qkv = w_in[..., o[0]:o[3]]
    wf = w_in[..., o[3]:o[4]]
    wcq = w_in[..., o[4]:o[5]]
    wckv = w_in[..., o[5]:o[6]]
    wkr = w_in[..., o[6]:o[7]]
    wg = w_in[..., o[7]:o[8]]
    reps = LANES // MLA_ROPE_DIM
    wf_pad = jnp.concatenate([wf, jnp.zeros((depth, d, LANES - FOX_HEADS), w_in.dtype)], axis=-1)
    return jnp.concatenate(
        [qkv, wcq, wckv, wf_pad, jnp.tile(wkr, (1, 1, reps)), jnp.tile(_swap_halves(wkr), (1, 1, reps)), wg], axis=-1
    ).astype(BF16)


def _prepare_w_q_up(w):
    depth, r, _ = w.shape
    w = w.reshape(depth, r, MLA_HEADS, MLA_QK_DIM)
    nope = w[..., :MLA_NOPE_DIM].reshape(depth, r, MLA_HEADS * MLA_NOPE_DIM)
    rope = w[..., MLA_NOPE_DIM:]
    rope_sw = _swap_halves(rope).reshape(depth, r, MLA_HEADS * MLA_ROPE_DIM)
    rope = rope.reshape(depth, r, MLA_HEADS * MLA_ROPE_DIM)
    return jnp.concatenate([nope, rope, rope_sw], axis=-1).astype(BF16)


def _prepare_w_kv_up(w):
    depth, r, _ = w.shape
    w = w.reshape(depth, r, MLA_HEADS, MLA_NOPE_DIM + MLA_V_DIM)
    kn = w[..., :MLA_NOPE_DIM].reshape(depth, r, MLA_HEADS * MLA_NOPE_DIM)
    vm = w[..., MLA_NOPE_DIM:].reshape(depth, r, MLA_HEADS * MLA_V_DIM)
    return jnp.concatenate([kn, vm], axis=-1).astype(BF16)


def _pad_lanes(x, width=LANES):
    return jnp.pad(x, [(0, 0)] * (x.ndim - 1) + [(0, width - x.shape[-1])])


def _key_rows(cum):
    b, t, _ = cum.shape
    return jnp.transpose(cum[:, :, :FOX_HEADS], (0, 2, 1)).reshape(b, FOX_HEADS // 2, 2, t)


def kernel(x_prompt, x_sample, cache_fox_k, cache_fox_v, cache_fox_logf, cache_mla_ckv, cache_mla_krope, meta_tokens, ln_in_g, ln_in_b, w_in, b_fgt, g_q_norm, w_q_up, g_kv_norm, w_kv_up, w_proj_fox, w_proj_mla, b_gate, w_out, ln1_g, ln1_b, ln2_g, ln2_b, w_router, b_router, w_exp_gate, w_exp_up, w_exp_down):
    bp, seq, d = x_prompt.shape
    bs, tn, _ = x_sample.shape
    depth = w_in.shape[0]
    past = cache_fox_k.shape[2]
    alpha = (2 * depth) ** 0.25
    t_real = N_META + seq
    tp = -(-t_real // ATTN_TILE) * ATTN_TILE
    n_p = bp * tp
    n_s = bs * tn
    n = n_p + n_s
    n_pad = -(-n // MOE_ROW_TILE) * MOE_ROW_TILE
    tk_s = -(-(past + tn) // SAMPLE_KEY_TILE) * SAMPLE_KEY_TILE

    meta = jnp.broadcast_to(meta_tokens[None].astype(F32), (bp, N_META, d))
    xp = jnp.concatenate([meta, x_prompt, jnp.zeros((bp, tp - t_real, d), F32)], axis=1)
    x = jnp.concatenate([xp.reshape(n_p, d), x_sample.reshape(n_s, d), jnp.zeros((n_pad - n, d), F32)], axis=0)
    pos = jnp.concatenate([
        jnp.tile(jnp.arange(tp, dtype=jnp.int32), bp),
        jnp.tile(past + jnp.arange(tn, dtype=jnp.int32), bs),
        jnp.zeros((n_pad - n,), jnp.int32),
    ])
    cos, sin = _rope_tables(pos)

    w_in_r = _prepare_w_in(w_in)
    w_q_r = _prepare_w_q_up(w_q_up)
    w_kv_r = _prepare_w_kv_up(w_kv_up)
    bf_r = _pad_lanes(b_fgt)[:, None, :]
    pa, pb, wo = w_proj_fox.astype(BF16), w_proj_mla.astype(BF16), w_out.astype(BF16)
    wg, wu, wd = w_exp_gate.astype(BF16), w_exp_up.astype(BF16), w_exp_down.astype(BF16)
    wr = _pad_lanes(w_router)
    wr_hi = wr.astype(BF16)
    wr_mid = (wr - wr_hi.astype(F32)).astype(BF16)
    wr_lo = (wr - wr_hi.astype(F32) - wr_mid.astype(F32)).astype(BF16)
    wr3 = jnp.stack([wr_hi, wr_mid, wr_lo])
    br = _pad_lanes(b_router[None, :])

    zpad = tk_s - past - tn
    kn_c, vm_c = _kv_up(cache_mla_ckv.reshape(depth, bs * past, -1), w_kv_r)
    kn_c = kn_c.reshape(depth, bs, past, 512)
    vm_c = vm_c.reshape(depth, bs, past, 512)
    kf_c = cache_fox_k.reshape(depth, bs, past, FOX_WIDTH).astype(BF16)
    vf_c = cache_fox_v.reshape(depth, bs, past, FOX_WIDTH).astype(BF16)
    kr_c = jnp.tile(cache_mla_krope, (1, 1, 1, LANES // MLA_ROPE_DIM)).astype(BF16)
    logf_c = _pad_lanes(cache_fox_logf)

    def with_past(cached, new):
        new = new.reshape(bs, tn, new.shape[-1])
        return jnp.concatenate([cached, new, jnp.zeros((bs, zpad, new.shape[-1]), new.dtype)], axis=1)

    x = _input_ln(x, ln_in_g[None, :], ln_in_b[None, :])
    rows = []
    for l in range(depth):
        pre = _pre_attention(
            x, w_in_r[l], bf_r[l], g_q_norm[l][None, :], w_q_r[l], g_kv_norm[l][None, :], w_kv_r[l],
            b_gate[l][None, :], cos, sin,
        )
        grp_p = lambda a: a[:n_p].reshape(bp, tp, a.shape[-1])
        grp_s = lambda a: a[n_p:n].reshape(bs, tn, a.shape[-1])

        cum_p = _cumsum_time(grp_p(pre["logf"]), ATTN_TILE)
        fox_p = _attention(
            grp_p(pre["qf"]), grp_p(pre["kf"]), grp_p(pre["vf"]), fox=True, tq=ATTN_TILE, tk=ATTN_TILE, q_off=0,
            n_meta=N_META, cq=cum_p, ck=_key_rows(cum_p),
        )
        mla_p = _attention(
            grp_p(pre["qn"]), grp_p(pre["kn"]), grp_p(pre["vm"]), fox=False, tq=ATTN_TILE, tk=ATTN_TILE, q_off=0,
            n_meta=N_META, qr=grp_p(pre["qr"]), kr=grp_p(pre["kr"]),
        )

        cum_s = _cumsum_time(with_past(logf_c[l], pre["logf"][n_p:n]), SAMPLE_KEY_TILE)
        fox_s = _attention(
            grp_s(pre["qf"]), with_past(kf_c[l], pre["kf"][n_p:n]), with_past(vf_c[l], pre["vf"][n_p:n]), fox=True,
            tq=tn, tk=SAMPLE_KEY_TILE, q_off=past, n_meta=0, cq=cum_s[:, past:past + tn], ck=_key_rows(cum_s),
        )
        mla_s = _attention(
            grp_s(pre["qn"]), with_past(kn_c[l], pre["kn"][n_p:n]), with_past(vm_c[l], pre["vm"][n_p:n]), fox=False,
            tq=tn, tk=SAMPLE_KEY_TILE, q_off=past, n_meta=0, qr=grp_s(pre["qr"]),
            kr=with_past(kr_c[l], pre["kr"][n_p:n]),
        )

        tail = jnp.zeros((n_pad - n, 512), BF16)
        fo = jnp.concatenate([fox_p.reshape(n_p, 512), fox_s.reshape(n_s, 512), tail], axis=0)
        mo = jnp.concatenate([mla_p.reshape(n_p, 512), mla_s.reshape(n_s, 512), tail], axis=0)
        x1, gates = _post_attention(
            x, fo, mo, pre["gate"], pa[l], pb[l], wo[l], ln1_g[l][None, :], ln1_b[l][None, :], wr3, br, alpha
        )
        x = _moe(x1, gates, wg[l], wu[l], wd[l], ln2_g[l][None, :], ln2_b[l][None, :], alpha)
        rows.append(pre)

    def stack_p(name, width):
        a = jnp.stack([r[name][:n_p].reshape(bp, tp, -1)[:, :t_real, :width] for r in rows])
        return a

    def stack_s(name, width):
        return jnp.stack([r[name][n_p:n].reshape(bs, tn, -1)[:, :, :width] for r in rows])

    heads = (FOX_HEADS, FOX_HEAD_DIM)
    y_prompt = x[:n_p].reshape(bp, tp, d)[:, N_META:t_real]
    y_sample = x[n_p:n].reshape(bs, tn, d)
    return (
        y_prompt,
        y_sample,
        stack_p("kf32", 512).reshape(depth, bp, t_real, *heads),
        stack_p("vf32", 512).reshape(depth, bp, t_real, *heads),
        stack_p("logf", FOX_HEADS),
        stack_p("ckv", 128),
        stack_p("kr32", MLA_ROPE_DIM),
        stack_s("kf32", 512).reshape(depth, bs, tn, *heads),
        stack_s("vf32", 512).reshape(depth, bs, tn, *heads),
        stack_s("logf", FOX_HEADS),
        stack_s("ckv", 128),
        stack_s("kr32", MLA_ROPE_DIM),
    )
```

```python
import functools

import jax
import jax.numpy as jnp
from jax import lax
from jax.experimental import pallas as pl
from jax.experimental.pallas import tpu as pltpu

F32 = jnp.float32
BF16 = jnp.bfloat16

CHUNK = 64
N_META = 16
FOX_HEADS = 8
FOX_HEAD_DIM = 64
FOX_WIDTH = FOX_HEADS * FOX_HEAD_DIM
MLA_HEADS = 8
MLA_NOPE_DIM = 64
MLA_ROPE_DIM = 32
MLA_QK_DIM = MLA_NOPE_DIM + MLA_ROPE_DIM
MLA_V_DIM = 64
N_EXPERTS = 16
N_GROUPS = 4
EXPERTS_PER_GROUP = N_EXPERTS // N_GROUPS
ROPE_THETA = 10000.0
LN_EPS = 1e-5
RMS_EPS = 1e-6
LOG2E = 1.4426950408889634

LANES = 128
ROW_TILE = 256
MOE_ROW_TILE = 512
ATTN_TILE = 256
SAMPLE_QUERY_TILE = 128
SAMPLE_KEY_TILE = 128
AUG_GROUP = 8
ONES_ROWS = 16
MASK_VALUE = -1e30
VMEM_LIMIT = 56 * 1024 * 1024

C_QKV = 0
C_CQ = 3 * FOX_WIDTH
C_CKV = C_CQ + 256
C_LOGF = C_CKV + 128
C_KR = C_LOGF + LANES
C_KRS = C_KR + LANES
C_GATE = C_KRS + LANES


def _params(*sem):
    return pltpu.CompilerParams(dimension_semantics=sem, vmem_limit_bytes=VMEM_LIMIT)


def _layer_norm(x, g, b):
    mu = jnp.mean(x, axis=-1, keepdims=True)
    xc = x - mu
    var = jnp.mean(xc * xc, axis=-1, keepdims=True)
    return xc * lax.rsqrt(var + LN_EPS) * g + b


def _rms_norm(x, g):
    return x * lax.rsqrt(jnp.mean(x * x, axis=-1, keepdims=True) + RMS_EPS) * g


def _dot(a, b):
    return jnp.dot(a, b, preferred_element_type=F32)


def _split3(x):
    hi = x.astype(BF16)
    r1 = x - hi.astype(F32)
    mid = r1.astype(BF16)
    lo = (r1 - mid.astype(F32)).astype(BF16)
    return hi, mid, lo


def _ln_kernel(x_ref, g_ref, b_ref, o_ref):
    o_ref[...] = _layer_norm(x_ref[...], g_ref[...], b_ref[...])


def _input_ln(x, g, b):
    n, d = x.shape
    return pl.pallas_call(
        _ln_kernel,
        grid=(n // ROW_TILE,),
        in_specs=[
            pl.BlockSpec((ROW_TILE, d), lambda i: (i, 0)),
            pl.BlockSpec((1, d), lambda i: (0, 0)),
            pl.BlockSpec((1, d), lambda i: (0, 0)),
        ],
        out_specs=pl.BlockSpec((ROW_TILE, d), lambda i: (i, 0)),
        out_shape=jax.ShapeDtypeStruct((n, d), F32),
        compiler_params=_params("parallel"),
        name="input_ln",
    )(x, g, b)


def _pre_kernel(
    x_ref, w_ref, bf_ref, gq_ref, wq_ref, gkv_ref, wkv_ref, bg_ref, cos_ref, sin_ref,
    qf_ref, kf32_ref, vf32_ref, kf_ref, vft_ref, logf_ref, qn_ref, qr_ref, ckv_ref, kr32_ref, kr_ref,
    kn_ref, vmt_ref, gate_ref, *, fox_scale, mla_scale,
):
    xb = x_ref[...].astype(BF16)
    cos = cos_ref[...]
    sin = sin_ref[...]

    qkv = _dot(xb, w_ref[:, C_QKV:C_CQ])
    qf_ref[...] = (qkv[:, 0:FOX_WIDTH] * fox_scale).astype(BF16)
    k = qkv[:, FOX_WIDTH:2 * FOX_WIDTH]
    v = qkv[:, 2 * FOX_WIDTH:3 * FOX_WIDTH]
    kf32_ref[...] = k
    vf32_ref[...] = v
    kf_ref[...] = k.astype(BF16)
    vft_ref[...] = v.T.astype(BF16)

    z = _dot(xb, w_ref[:, C_LOGF:C_KR]) + bf_ref[...]
    logf_ref[...] = jnp.minimum(z, 0.0) - jnp.log1p(jnp.exp(-jnp.abs(z)))

    cq = _rms_norm(_dot(xb, w_ref[:, C_CQ:C_CKV]), gq_ref[...]).astype(BF16)
    qm = _dot(cq, wq_ref[...])
    qn_ref[...] = (qm[:, 0:512] * mla_scale).astype(BF16)
    cos2 = jnp.concatenate([cos, cos], axis=1)
    sin2 = jnp.concatenate([sin, sin], axis=1)
    qr_ref[...] = ((qm[:, 512:768] * cos2 + qm[:, 768:1024] * sin2) * mla_scale).astype(BF16)

    ckv = _rms_norm(_dot(xb, w_ref[:, C_CKV:C_LOGF]), gkv_ref[...])
    ckv_ref[...] = ckv
    kv = _dot(ckv.astype(BF16), wkv_ref[...])
    kn_ref[...] = kv[:, 0:512].astype(BF16)
    vmt_ref[...] = kv[:, 512:1024].T.astype(BF16)
    kr = _dot(xb, w_ref[:, C_KR:C_KRS]) * cos + _dot(xb, w_ref[:, C_KRS:C_GATE]) * sin
    kr32_ref[...] = kr
    kr_ref[...] = kr.astype(BF16)

    g = _dot(xb, w_ref[:, C_GATE:]) + bg_ref[...]
    gate_ref[...] = jax.nn.sigmoid(g).astype(BF16)


def _pre_attention(x, w, bf, gq, wq, gkv, wkv, bg, cos, sin):
    n, d = x.shape
    tm = ROW_TILE
    row = lambda width: pl.BlockSpec((tm, width), lambda i: (i, 0))
    col = pl.BlockSpec((512, tm), lambda i: (0, i))
    full = lambda a: pl.BlockSpec(a.shape, lambda i: (0,) * a.ndim)
    outs = [
        ("qf", 512, BF16), ("kf32", 512, F32), ("vf32", 512, F32), ("kf", 512, BF16), ("vft", None, BF16),
        ("logf", LANES, F32), ("qn", 512, BF16), ("qr", 256, BF16), ("ckv", 128, F32), ("kr32", LANES, F32),
        ("kr", LANES, BF16), ("kn", 512, BF16), ("vmt", None, BF16), ("gate", 2 * d, BF16),
    ]
    res = pl.pallas_call(
        functools.partial(
            _pre_kernel, fox_scale=FOX_HEAD_DIM ** -0.5 * LOG2E, mla_scale=MLA_QK_DIM ** -0.5 * LOG2E
        ),
        grid=(n // tm,),
        in_specs=[row(d), full(w), full(bf), full(gq), full(wq), full(gkv), full(wkv), full(bg), row(LANES), row(LANES)],
        out_specs=[col if wd is None else row(wd) for _, wd, _ in outs],
        out_shape=[jax.ShapeDtypeStruct((512, n) if wd is None else (n, wd), dt) for _, wd, dt in outs],
        compiler_params=_params("parallel"),
        name="pre_attention",
    )(x, w, bf, gq, wq, gkv, wkv, bg, cos, sin)
    return {name: r for (name, _, _), r in zip(outs, res)}


def _kvup_kernel(c_ref, w_ref, kn_ref, vmt_ref):
    kv = _dot(c_ref[0].astype(BF16), w_ref[0])
    kn_ref[0] = kv[:, 0:512].astype(BF16)
    vmt_ref[0] = kv[:, 512:1024].T.astype(BF16)


def _kv_up(ckv, wkv):
    depth, n, r = ckv.shape
    tm = ROW_TILE
    return pl.pallas_call(
        _kvup_kernel,
        grid=(depth, n // tm),
        in_specs=[pl.BlockSpec((1, tm, r), lambda l, i: (l, i, 0)), pl.BlockSpec((1, r, 1024), lambda l, i: (l, 0, 0))],
        out_specs=[pl.BlockSpec((1, tm, 512), lambda l, i: (l, i, 0)), pl.BlockSpec((1, 512, tm), lambda l, i: (l, 0, i))],
        out_shape=[jax.ShapeDtypeStruct((depth, n, 512), BF16), jax.ShapeDtypeStruct((depth, 512, n), BF16)],
        compiler_params=_params("parallel", "parallel"),
        name="kv_up_cached",
    )(ckv, wkv)


def _cumsum_kernel(x_ref, o_ref, carry_ref, *, tt):
    @pl.when(pl.program_id(1) == 0)
    def _():
        carry_ref[...] = jnp.zeros_like(carry_ref)

    r = lax.broadcasted_iota(jnp.int32, (tt, tt), 0)
    c = lax.broadcasted_iota(jnp.int32, (tt, tt), 1)
    tri = jnp.where(c <= r, 1.0, 0.0).astype(BF16)
    hi, mid, lo = _split3(x_ref[0])
    cum = _dot(tri, lo) + _dot(tri, mid) + _dot(tri, hi) + carry_ref[...]
    carry_ref[...] = cum[tt - 1:tt, :]
    hi, mid, lo = _split3(cum * LOG2E)
    o_ref[0] = jnp.concatenate([hi, mid, lo], axis=1)


def _cumsum_time(x, tt):
    s, t, w = x.shape
    return pl.pallas_call(
        functools.partial(_cumsum_kernel, tt=tt),
        grid=(s, t // tt),
        in_specs=[pl.BlockSpec((1, tt, w), lambda b, i: (b, i, 0))],
        out_specs=pl.BlockSpec((1, tt, 3 * w), lambda b, i: (b, i, 0)),
        out_shape=jax.ShapeDtypeStruct((s, t, 3 * w), BF16),
        scratch_shapes=[pltpu.VMEM((1, w), F32)],
        compiler_params=_params("parallel", "arbitrary"),
        name="cumsum_time",
    )(x)


def _bias_operands(cum3):
    s, t, _ = cum3.shape
    c = cum3.reshape(s, t, 3, LANES)[..., :FOX_HEADS]
    c = jnp.transpose(c, (0, 3, 1, 2)).reshape(s, FOX_HEADS // 2, 2, t, 3)
    c = jnp.transpose(c, (0, 1, 3, 2, 4))
    one = jnp.ones_like(c)
    zero = jnp.zeros(c.shape[:-1] + (AUG_GROUP - 6,), c.dtype)
    pad = lambda a: _pad_lanes(a.reshape(s, FOX_HEADS // 2, t, 2 * AUG_GROUP))
    return pad(jnp.concatenate([c, one, zero], axis=-1)), pad(jnp.concatenate([one, -c, zero], axis=-1))


def _chunk_end(pos, n_meta):
    body = n_meta + ((pos - n_meta) // CHUNK) * CHUNK + CHUNK - 1
    if n_meta == 0:
        return body
    return jnp.where(pos < n_meta, n_meta - 1, body)


def _chunk_id(pos, n_meta):
    return jnp.where(pos < n_meta, 0, 1 + (pos - n_meta) // CHUNK)


def _attn_kernel(q_ref, qx_ref, k_ref, kx_ref, vt_ref, o_ref, *, fox, tq, tk, q_off, n_meta, n_ktiles):
    p = pl.program_id(1)
    i = pl.program_id(2)
    q_lo = q_off + i * tq
    q_hi = q_lo + tq - 1
    if fox:
        vis_lo, vis_hi = q_lo, q_hi
    else:
        vis_lo, vis_hi = _chunk_end(q_lo, n_meta), _chunk_end(q_hi, n_meta)
    n_tot = jnp.minimum(vis_hi // tk + 1, n_ktiles)
    n_full = jnp.minimum((vis_lo + 1) // tk, n_tot - 1)

    lane = lax.broadcasted_iota(jnp.int32, (1, LANES), 1)
    q_pos = q_lo + lax.broadcasted_iota(jnp.int32, (1, tq), 1)
    q2 = q_ref[...]
    qx2 = qx_ref[...]
    qc = []
    for hh in range(2):
        x_group = lane // AUG_GROUP == hh if fox else lane // MLA_ROPE_DIM == 2 * (p % 2) + hh
        qc.append(jnp.concatenate([
            jnp.where(lane // FOX_HEAD_DIM == hh, q2, jnp.zeros_like(q2)),
            jnp.where(x_group, qx2, jnp.zeros_like(qx2)),
        ], axis=1))
    ones = jnp.ones((ONES_ROWS, tk), BF16)

    def scores(j, hh):
        ks = pl.ds(pl.multiple_of(j * tk, tk), tk)
        kc = jnp.concatenate([k_ref[ks, :], kx_ref[ks, :]], axis=1)
        return lax.dot_general(kc, qc[hh], (((1,), (1,)), ((), ())), preferred_element_type=F32)

    def update(j, hh, st, state, masked):
        m, acc = state
        ks = pl.ds(pl.multiple_of(j * tk, tk), tk)
        if masked:
            k_pos = j * tk + lax.broadcasted_iota(jnp.int32, (tk, 1), 0)
            if fox:
                ok = k_pos <= q_pos
            else:
                ok = _chunk_id(k_pos, n_meta) <= _chunk_id(q_pos, n_meta)
            st = jnp.where(ok, st, MASK_VALUE)
        m_new = jnp.maximum(m, jnp.max(st, axis=0, keepdims=True))
        alpha = jnp.exp2(m - m_new)
        pt = jnp.exp2(st - m_new).astype(BF16)
        va = jnp.concatenate([vt_ref[FOX_HEAD_DIM * hh:FOX_HEAD_DIM * (hh + 1), ks], ones], axis=0)
        return m_new, alpha * acc + _dot(va, pt)

    def pipelined(j, carry):
        (s0, s1), (st0, st1) = carry
        n0 = scores(j + 1, 0)
        st0 = update(j, 0, s0, st0, False)
        n1 = scores(j + 1, 1)
        st1 = update(j, 1, s1, st1, False)
        return (n0, n1), (st0, st1)

    def tail(j, states):
        return tuple(update(j, hh, scores(j, hh), states[hh], True) for hh in range(2))

    init = (jnp.full((1, tq), MASK_VALUE, F32), jnp.zeros((FOX_HEAD_DIM + ONES_ROWS, tq), F32))
    first = (scores(0, 0), scores(0, 1))
    last, states = lax.fori_loop(0, n_full, pipelined, (first, (init, init)))
    states = tuple(update(n_full, hh, last[hh], states[hh], True) for hh in range(2))
    states = lax.fori_loop(n_full + 1, n_tot, tail, states)
    out_t = jnp.concatenate(
        [acc[0:FOX_HEAD_DIM] / acc[FOX_HEAD_DIM:FOX_HEAD_DIM + 1] for _, acc in states], axis=0
    )
    o_ref[...] = out_t.T.astype(o_ref.dtype)


def _attention(q, qx, k, kx, vt, *, fox, tq, tk, q_off, n_meta):
    b, t_q, _ = q.shape
    t_k = k.shape[1]
    qspec = pl.BlockSpec((None, tq, LANES), lambda bb, p, i: (bb, i, p))
    kspec = pl.BlockSpec((None, t_k, LANES), lambda bb, p, i: (bb, 0, p))
    if fox:
        qxspec = pl.BlockSpec((None, None, tq, LANES), lambda bb, p, i: (bb, p, i, 0))
        kxspec = pl.BlockSpec((None, None, t_k, LANES), lambda bb, p, i: (bb, p, 0, 0))
    else:
        qxspec = pl.BlockSpec((None, tq, LANES), lambda bb, p, i: (bb, i, p // 2))
        kxspec = pl.BlockSpec((None, t_k, LANES), lambda bb, p, i: (bb, 0, 0))
    if vt.ndim == 2:
        vspec = pl.BlockSpec((LANES, t_k), lambda bb, p, i: (p, bb))
    else:
        vspec = pl.BlockSpec((None, LANES, t_k), lambda bb, p, i: (bb, p, 0))
    return pl.pallas_call(
        functools.partial(_attn_kernel, fox=fox, tq=tq, tk=tk, q_off=q_off, n_meta=n_meta, n_ktiles=t_k // tk),
        grid=(b, 4, t_q // tq),
        in_specs=[qspec, qxspec, kspec, kxspec, vspec],
        out_specs=qspec,
        out_shape=jax.ShapeDtypeStruct((b, t_q, 512), BF16),
        compiler_params=_params("parallel", "parallel", "arbitrary"),
        name="fox_attention" if fox else "mla_attention",
    )(q, qx, k, kx, vt)


def _group_partner(x, lane, d, width):
    fwd = pltpu.roll(x, LANES - d, 1)
    back = pltpu.roll(x, width - d, 1)
    return jnp.where(lane % width + d < width, fwd, back)


def _route(logits, b_router, lane):
    scores = jax.nn.sigmoid(logits)
    sel = scores + b_router
    e = EXPERTS_PER_GROUP
    rank = jnp.zeros(sel.shape, jnp.int32)
    for d in range(1, e):
        other = _group_partner(sel, lane, d, e)
        other_first = lane % e + d >= e
        beats = (other > sel) | ((other == sel) & other_first)
        rank = rank + beats.astype(jnp.int32)
    top2 = rank < 2
    kept = jnp.where(top2, sel, 0.0)
    gscore = kept
    for d in range(1, e):
        gscore = gscore + _group_partner(kept, lane, d, e)
    grank = jnp.zeros_like(rank)
    for gstep in range(1, N_GROUPS):
        d = gstep * e
        other = _group_partner(gscore, lane, d, N_EXPERTS)
        other_first = lane % N_EXPERTS + d >= N_EXPERTS
        beats = (other > gscore) | ((other == gscore) & other_first)
        grank = grank + beats.astype(jnp.int32)
    chosen = top2 & (grank == 0) & (lane < N_EXPERTS)
    w = jnp.where(chosen, scores, 0.0)
    return w / jnp.sum(w, axis=1, keepdims=True)


def _post_kernel(x_ref, fo_ref, mo_ref, gate_ref, pa_ref, pb_ref, wo_ref, g1_ref, b1_ref, wr_ref, br_ref,
                 x1_ref, gates_ref, *, alpha):
    d = x_ref.shape[1]
    merged = gate_ref[:, 0:d].astype(F32) * _dot(fo_ref[...], pa_ref[...])
    merged = merged + gate_ref[:, d:2 * d].astype(F32) * _dot(mo_ref[...], pb_ref[...])
    mix = _dot(merged.astype(BF16), wo_ref[...])
    x1 = _layer_norm(alpha * x_ref[...] + mix, g1_ref[...], b1_ref[...])
    x1_ref[...] = x1
    hi, mid, lo = _split3(x1)
    whi, wmid, wlo = wr_ref[0], wr_ref[1], wr_ref[2]
    logits = _dot(lo, whi) + _dot(mid, wmid) + _dot(hi, wlo) + _dot(mid, whi) + _dot(hi, wmid) + _dot(hi, whi)
    lane = lax.broadcasted_iota(jnp.int32, logits.shape, 1)
    gates_ref[...] = _route(logits, br_ref[...], lane)


def _post_attention(x, fo, mo, gate, pa, pb, wo, g1, b1, wr3, br, alpha):
    n, d = x.shape
    tm = ROW_TILE
    row = lambda width: pl.BlockSpec((tm, width), lambda i: (i, 0))
    full = lambda a: pl.BlockSpec(a.shape, lambda i: (0,) * a.ndim)
    return pl.pallas_call(
        functools.partial(_post_kernel, alpha=alpha),
        grid=(n // tm,),
        in_specs=[row(d), row(512), row(512), row(2 * d), full(pa), full(pb), full(wo), full(g1), full(b1), full(wr3), full(br)],
        out_specs=[row(d), row(LANES)],
        out_shape=[jax.ShapeDtypeStruct((n, d), F32), jax.ShapeDtypeStruct((n, LANES), F32)],
        compiler_params=_params("parallel"),
        name="post_attention",
    )(x, fo, mo, gate, pa, pb, wo, g1, b1, wr3, br)


def _moe_kernel(x_ref, gates_ref, wg_ref, wu_ref, wd_ref, g2_ref, b2_ref, o_ref, xb_ref, acc_ref, *, alpha):
    e = pl.program_id(1)

    @pl.when(e == 0)
    def _():
        xb_ref[...] = x_ref[...].astype(BF16)
        acc_ref[...] = jnp.zeros_like(acc_ref)

    lane = lax.broadcasted_iota(jnp.int32, (1, LANES), 1)
    gate = jnp.sum(jnp.where(lane == e, gates_ref[...], 0.0), axis=1, keepdims=True)
    xb = xb_ref[...]
    hid = jax.nn.silu(_dot(xb, wg_ref[0])) * _dot(xb, wu_ref[0])
    acc_ref[...] += _dot((hid * gate).astype(BF16), wd_ref[0])

    @pl.when(e == pl.num_programs(1) - 1)
    def _():
        o_ref[...] = _layer_norm(alpha * x_ref[...] + acc_ref[...], g2_ref[...], b2_ref[...])


def _moe(x, gates, wg, wu, wd, g2, b2, alpha):
    n, d = x.shape
    n_exp, _, f = wg.shape
    tm = MOE_ROW_TILE
    return pl.pallas_call(
        functools.partial(_moe_kernel, alpha=alpha),
        grid=(n // tm, n_exp),
        in_specs=[
            pl.BlockSpec((tm, d), lambda i, e: (i, 0)),
            pl.BlockSpec((tm, LANES), lambda i, e: (i, 0)),
            pl.BlockSpec((1, d, f), lambda i, e: (e, 0, 0)),
            pl.BlockSpec((1, d, f), lambda i, e: (e, 0, 0)),
            pl.BlockSpec((1, f, d), lambda i, e: (e, 0, 0)),
            pl.BlockSpec((1, d), lambda i, e: (0, 0)),
            pl.BlockSpec((1, d), lambda i, e: (0, 0)),
        ],
        out_specs=pl.BlockSpec((tm, d), lambda i, e: (i, 0)),
        out_shape=jax.ShapeDtypeStruct((n, d), F32),
        scratch_shapes=[pltpu.VMEM((tm, d), BF16), pltpu.VMEM((tm, d), F32)],
        compiler_params=_params("parallel", "arbitrary"),
        name="moe_ln2",
    )(x, gates, wg, wu, wd, g2, b2)


def _rope_tables(pos):
    half = MLA_ROPE_DIM // 2
    inv_freq = ROPE_THETA ** (-jnp.arange(half, dtype=F32) / half)
    ang = pos.astype(F32)[:, None] * inv_freq[None, :]
    cos, sin = jnp.cos(ang), jnp.sin(ang)
    cos32 = jnp.concatenate([cos, cos], axis=1)
    sin32 = jnp.concatenate([-sin, sin], axis=1)
    return jnp.tile(cos32, (1, LANES // MLA_ROPE_DIM)), jnp.tile(sin32, (1, LANES // MLA_ROPE_DIM))


def _swap_halves(w):
    half = MLA_ROPE_DIM // 2
    return jnp.concatenate([w[..., half:], w[..., :half]], axis=-1)


def _prepare_w_in(w_in):
    depth, d, _ = w_in.shape
    o = [0]
    for sz in (FOX_WIDTH, FOX_WIDTH, FOX_WIDTH, FOX_HEADS, 256, 128, MLA_ROPE_DIM, 2 * d):
        o.append(o[-1] + sz)
    qkv = w_in[..., o[0]:o[3]]
    wf = w_in[..., o[3]:o[4]]
    wcq = w_in[..., o[4]:o[5]]
    wckv = w_in[..., o[5]:o[6]]
    wkr = w_in[..., o[6]:o[7]]
    wg = w_in[..., o[7]:o[8]]
    reps = LANES // MLA_ROPE_DIM
    wf_pad = jnp.concatenate([wf, jnp.zeros((depth, d, LANES - FOX_HEADS), w_in.dtype)], axis=-1)
    return jnp.concatenate(
        [qkv, wcq, wckv, wf_pad, jnp.tile(wkr, (1, 1, reps)), jnp.tile(_swap_halves(wkr), (1, 1, reps)), wg], axis=-1
    ).astype(BF16)


def _prepare_w_q_up(w):
    depth, r, _ = w.shape
    w = w.reshape(depth, r, MLA_HEADS, MLA_QK_DIM)
    nope = w[..., :MLA_NOPE_DIM].reshape(depth, r, MLA_HEADS * MLA_NOPE_DIM)
    rope = w[..., MLA_NOPE_DIM:]
    rope_sw = _swap_halves(rope).reshape(depth, r, MLA_HEADS * MLA_ROPE_DIM)
    rope = rope.reshape(depth, r, MLA_HEADS * MLA_ROPE_DIM)
    return jnp.concatenate([nope, rope, rope_sw], axis=-1).astype(BF16)


def _prepare_w_kv_up(w):
    depth, r, _ = w.shape
    w = w.reshape(depth, r, MLA_HEADS, MLA_NOPE_DIM + MLA_V_DIM)
    kn = w[..., :MLA_NOPE_DIM].reshape(depth, r, MLA_HEADS * MLA_NOPE_DIM)
    vm = w[..., MLA_NOPE_DIM:].reshape(depth, r, MLA_HEADS * MLA_V_DIM)
    return jnp.concatenate([kn, vm], axis=-1).astype(BF16)


def _pad_lanes(x, width=LANES):
    return jnp.pad(x, [(0, 0)] * (x.ndim - 1) + [(0, width - x.shape[-1])])


def kernel(x_prompt, x_sample, cache_fox_k, cache_fox_v, cache_fox_logf, cache_mla_ckv, cache_mla_krope, meta_tokens, ln_in_g, ln_in_b, w_in, b_fgt, g_q_norm, w_q_up, g_kv_norm, w_kv_up, w_proj_fox, w_proj_mla, b_gate, w_out, ln1_g, ln1_b, ln2_g, ln2_b, w_router, b_router, w_exp_gate, w_exp_up, w_exp_down):
    bp, seq, d = x_prompt.shape
    bs, tn, _ = x_sample.shape
    depth = w_in.shape[0]
    past = cache_fox_k.shape[2]
    alpha = (2 * depth) ** 0.25
    t_real = N_META + seq
    tp = -(-t_real // ATTN_TILE) * ATTN_TILE
    n_p = bp * tp
    n_s = bs * tn
    n = n_p + n_s
    n_pad = -(-n // MOE_ROW_TILE) * MOE_ROW_TILE
    tk_s = -(-(past + tn) // SAMPLE_KEY_TILE) * SAMPLE_KEY_TILE
    tq_s = SAMPLE_QUERY_TILE

    meta = jnp.broadcast_to(meta_tokens[None].astype(F32), (bp, N_META, d))
    xp = jnp.concatenate([meta, x_prompt, jnp.zeros((bp, tp - t_real, d), F32)], axis=1)
    x = jnp.concatenate([xp.reshape(n_p, d), x_sample.reshape(n_s, d), jnp.zeros((n_pad - n, d), F32)], axis=0)
    pos = jnp.concatenate([
        jnp.tile(jnp.arange(tp, dtype=jnp.int32), bp),
        jnp.tile(past + jnp.arange(tn, dtype=jnp.int32), bs),
        jnp.zeros((n_pad - n,), jnp.int32),
    ])
    cos, sin = _rope_tables(pos)

    w_in_r = _prepare_w_in(w_in)
    w_q_r = _prepare_w_q_up(w_q_up)
    w_kv_r = _prepare_w_kv_up(w_kv_up)
    bf_r = _pad_lanes(b_fgt)[:, None, :]
    pa, pb, wo = w_proj_fox.astype(BF16), w_proj_mla.astype(BF16), w_out.astype(BF16)
    wg, wu, wd = w_exp_gate.astype(BF16), w_exp_up.astype(BF16), w_exp_down.astype(BF16)
    wr = _pad_lanes(w_router)
    wr_hi = wr.astype(BF16)
    wr_mid = (wr - wr_hi.astype(F32)).astype(BF16)
    wr_lo = (wr - wr_hi.astype(F32) - wr_mid.astype(F32)).astype(BF16)
    wr3 = jnp.stack([wr_hi, wr_mid, wr_lo])
    br = _pad_lanes(b_router[None, :])

    zpad = tk_s - past - tn
    kn_c, vmt_c = _kv_up(cache_mla_ckv.reshape(depth, bs * past, -1), w_kv_r)
    kn_c = kn_c.reshape(depth, bs, past, 512)
    vmt_c = jnp.transpose(vmt_c.reshape(depth, 512, bs, past), (0, 2, 1, 3))
    kf_c = cache_fox_k.reshape(depth, bs, past, FOX_WIDTH).astype(BF16)
    vft_c = jnp.transpose(cache_fox_v.reshape(depth, bs, past, FOX_WIDTH), (0, 1, 3, 2)).astype(BF16)
    kr_c = jnp.tile(cache_mla_krope, (1, 1, 1, LANES // MLA_ROPE_DIM)).astype(BF16)
    logf_c = _pad_lanes(cache_fox_logf)

    def with_past(cached, new):
        new = new.reshape(bs, tn, new.shape[-1])
        return jnp.concatenate([cached, new, jnp.zeros((bs, zpad, new.shape[-1]), new.dtype)], axis=1)

    def with_past_t(cached_t, new_t):
        new_t = jnp.transpose(new_t.reshape(512, bs, tn), (1, 0, 2))
        return jnp.concatenate([cached_t, new_t, jnp.zeros((bs, 512, zpad), new_t.dtype)], axis=2)

    def sample_queries(a):
        a = a.reshape(bs, tn, a.shape[-1])
        return jnp.pad(a, ((0, 0), (0, tq_s - tn), (0, 0)))

    x = _input_ln(x, ln_in_g[None, :], ln_in_b[None, :])
    rows = []
    for l in range(depth):
        pre = _pre_attention(
            x, w_in_r[l], bf_r[l], g_q_norm[l][None, :], w_q_r[l], g_kv_norm[l][None, :], w_kv_r[l],
            b_gate[l][None, :], cos, sin,
        )
        grp_p = lambda a: a[:n_p].reshape(bp, tp, a.shape[-1])

        qx_p, kx_p = _bias_operands(_cumsum_time(grp_p(pre["logf"]), ATTN_TILE))
        fox_p = _attention(
            grp_p(pre["qf"]), qx_p, grp_p(pre["kf"]), kx_p, pre["vft"],
            fox=True, tq=ATTN_TILE, tk=ATTN_TILE, q_off=0, n_meta=N_META,
        )
        mla_p = _attention(
            grp_p(pre["qn"]), grp_p(pre["qr"]), grp_p(pre["kn"]), grp_p(pre["kr"]), pre["vmt"],
            fox=False, tq=ATTN_TILE, tk=ATTN_TILE, q_off=0, n_meta=N_META,
        )

        qx_s, kx_s = _bias_operands(_cumsum_time(with_past(logf_c[l], pre["logf"][n_p:n]), SAMPLE_KEY_TILE))
        qx_s = jnp.pad(qx_s[:, :, past:past + tn], ((0, 0), (0, 0), (0, tq_s - tn), (0, 0)))
        fox_s = _attention(
            sample_queries(pre["qf"][n_p:n]), qx_s, with_past(kf_c[l], pre["kf"][n_p:n]), kx_s,
            with_past_t(vft_c[l], pre["vft"][:, n_p:n]),
            fox=True, tq=tq_s, tk=SAMPLE_KEY_TILE, q_off=past, n_meta=0,
        )
        mla_s = _attention(
            sample_queries(pre["qn"][n_p:n]), sample_queries(pre["qr"][n_p:n]), with_past(kn_c[l], pre["kn"][n_p:n]),
            with_past(kr_c[l], pre["kr"][n_p:n]), with_past_t(vmt_c[l], pre["vmt"][:, n_p:n]),
            fox=False, tq=tq_s, tk=SAMPLE_KEY_TILE, q_off=past, n_meta=0,
        )

        tail = jnp.zeros((n_pad - n, 512), BF16)
        fo = jnp.concatenate([fox_p.reshape(n_p, 512), fox_s[:, :tn].reshape(n_s, 512), tail], axis=0)
        mo = jnp.concatenate([mla_p.reshape(n_p, 512), mla_s[:, :tn].reshape(n_s, 512), tail], axis=0)
        x1, gates = _post_attention(
            x, fo, mo, pre["gate"], pa[l], pb[l], wo[l], ln1_g[l][None, :], ln1_b[l][None, :], wr3, br, alpha
        )
        x = _moe(x1, gates, wg[l], wu[l], wd[l], ln2_g[l][None, :], ln2_b[l][None, :], alpha)
        rows.append(pre)

    def stack_p(name, width):
        return jnp.stack([r[name][:n_p].reshape(bp, tp, -1)[:, :t_real, :width] for r in rows])

    def stack_s(name, width):
        return jnp.stack([r[name][n_p:n].reshape(bs, tn, -1)[:, :, :width] for r in rows])

    heads = (FOX_HEADS, FOX_HEAD_DIM)
    y_prompt = x[:n_p].reshape(bp, tp, d)[:, N_META:t_real]
    y_sample = x[n_p:n].reshape(bs, tn, d)
    return (
        y_prompt,
        y_sample,
        stack_p("kf32", 512).reshape(depth, bp, t_real, *heads),
        stack_p("vf32", 512).reshape(depth, bp, t_real, *heads),
        stack_p("logf", FOX_HEADS),
        stack_p("ckv", 128),
        stack_p("kr32", MLA_ROPE_DIM),
        stack_s("kf32", 512).reshape(depth, bs, tn, *heads),
        stack_s("vf32", 512).reshape(depth, bs, tn, *heads),
        stack_s("logf", FOX_HEADS),
        stack_s("ckv", 128),
        stack_s("kr32", MLA_ROPE_DIM),
    )
```

```python
import functools

import jax
import jax.numpy as jnp
from jax import lax
from jax.experimental import pallas as pl
from jax.experimental.pallas import tpu as pltpu

F32 = jnp.float32
BF16 = jnp.bfloat16

CHUNK = 64
N_META = 16
FOX_HEADS = 8
FOX_HEAD_DIM = 64
FOX_WIDTH = FOX_HEADS * FOX_HEAD_DIM
MLA_HEADS = 8
MLA_NOPE_DIM = 64
MLA_ROPE_DIM = 32
MLA_QK_DIM = MLA_NOPE_DIM + MLA_ROPE_DIM
MLA_V_DIM = 64
N_EXPERTS = 16
N_GROUPS = 4
EXPERTS_PER_GROUP = N_EXPERTS // N_GROUPS
ROPE_THETA = 10000.0
LN_EPS = 1e-5
RMS_EPS = 1e-6
LOG2E = 1.4426950408889634

LANES = 128
ROW_TILE = 256
MOE_ROW_TILE = 512
ATTN_TILE = 256
ATTN_SLOTS = 4
SAMPLE_QUERY_TILE = 128
SAMPLE_KEY_TILE = 128
AUG_GROUP = 8
ONES_ROWS = 16
MASK_VALUE = -1e30
VMEM_LIMIT = 56 * 1024 * 1024

C_QKV = 0
C_CQ = 3 * FOX_WIDTH
C_CKV = C_CQ + 256
C_LOGF = C_CKV + 128
C_KR = C_LOGF + LANES
C_KRS = C_KR + LANES
C_GATE = C_KRS + LANES


def _params(*sem):
    return pltpu.CompilerParams(dimension_semantics=sem, vmem_limit_bytes=VMEM_LIMIT)


def _layer_norm(x, g, b):
    mu = jnp.mean(x, axis=-1, keepdims=True)
    xc = x - mu
    var = jnp.mean(xc * xc, axis=-1, keepdims=True)
    return xc * lax.rsqrt(var + LN_EPS) * g + b


def _rms_norm(x, g):
    return x * lax.rsqrt(jnp.mean(x * x, axis=-1, keepdims=True) + RMS_EPS) * g


def _dot(a, b):
    return jnp.dot(a, b, preferred_element_type=F32)


def _split3(x):
    hi = x.astype(BF16)
    r1 = x - hi.astype(F32)
    mid = r1.astype(BF16)
    lo = (r1 - mid.astype(F32)).astype(BF16)
    return hi, mid, lo


def _ln_kernel(x_ref, g_ref, b_ref, o_ref):
    o_ref[...] = _layer_norm(x_ref[...], g_ref[...], b_ref[...])


def _input_ln(x, g, b):
    n, d = x.shape
    return pl.pallas_call(
        _ln_kernel,
        grid=(n // ROW_TILE,),
        in_specs=[
            pl.BlockSpec((ROW_TILE, d), lambda i: (i, 0)),
            pl.BlockSpec((1, d), lambda i: (0, 0)),
            pl.BlockSpec((1, d), lambda i: (0, 0)),
        ],
        out_specs=pl.BlockSpec((ROW_TILE, d), lambda i: (i, 0)),
        out_shape=jax.ShapeDtypeStruct((n, d), F32),
        compiler_params=_params("parallel"),
        name="input_ln",
    )(x, g, b)


def _pre_kernel(
    x_ref, w_ref, bf_ref, gq_ref, wq_ref, gkv_ref, wkv_ref, bg_ref, cos_ref, sin_ref,
    qf_ref, kf32_ref, vf32_ref, kf_ref, vft_ref, logf_ref, qn_ref, qr_ref, ckv_ref, kr32_ref, kr_ref,
    kn_ref, vmt_ref, gate_ref, *, fox_scale, mla_scale,
):
    xb = x_ref[...].astype(BF16)
    cos = cos_ref[...]
    sin = sin_ref[...]

    qkv = _dot(xb, w_ref[:, C_QKV:C_CQ])
    qf_ref[...] = (qkv[:, 0:FOX_WIDTH] * fox_scale).astype(BF16)
    k = qkv[:, FOX_WIDTH:2 * FOX_WIDTH]
    v = qkv[:, 2 * FOX_WIDTH:3 * FOX_WIDTH]
    kf32_ref[...] = k
    vf32_ref[...] = v
    kf_ref[...] = k.astype(BF16)
    vft_ref[...] = v.T.astype(BF16)

    z = _dot(xb, w_ref[:, C_LOGF:C_KR]) + bf_ref[...]
    logf_ref[...] = jnp.minimum(z, 0.0) - jnp.log1p(jnp.exp(-jnp.abs(z)))

    cq = _rms_norm(_dot(xb, w_ref[:, C_CQ:C_CKV]), gq_ref[...]).astype(BF16)
    qm = _dot(cq, wq_ref[...])
    qn_ref[...] = (qm[:, 0:512] * mla_scale).astype(BF16)
    cos2 = jnp.concatenate([cos, cos], axis=1)
    sin2 = jnp.concatenate([sin, sin], axis=1)
    qr_ref[...] = ((qm[:, 512:768] * cos2 + qm[:, 768:1024] * sin2) * mla_scale).astype(BF16)

    ckv = _rms_norm(_dot(xb, w_ref[:, C_CKV:C_LOGF]), gkv_ref[...])
    ckv_ref[...] = ckv
    kv = _dot(ckv.astype(BF16), wkv_ref[...])
    kn_ref[...] = kv[:, 0:512].astype(BF16)
    vmt_ref[...] = kv[:, 512:1024].T.astype(BF16)
    kr = _dot(xb, w_ref[:, C_KR:C_KRS]) * cos + _dot(xb, w_ref[:, C_KRS:C_GATE]) * sin
    kr32_ref[...] = kr
    kr_ref[...] = kr.astype(BF16)

    g = _dot(xb, w_ref[:, C_GATE:]) + bg_ref[...]
    gate_ref[...] = jax.nn.sigmoid(g).astype(BF16)


def _pre_attention(x, w, bf, gq, wq, gkv, wkv, bg, cos, sin):
    n, d = x.shape
    tm = ROW_TILE
    row = lambda width: pl.BlockSpec((tm, width), lambda i: (i, 0))
    col = pl.BlockSpec((512, tm), lambda i: (0, i))
    full = lambda a: pl.BlockSpec(a.shape, lambda i: (0,) * a.ndim)
    outs = [
        ("qf", 512, BF16), ("kf32", 512, F32), ("vf32", 512, F32), ("kf", 512, BF16), ("vft", None, BF16),
        ("logf", LANES, F32), ("qn", 512, BF16), ("qr", 256, BF16), ("ckv", 128, F32), ("kr32", LANES, F32),
        ("kr", LANES, BF16), ("kn", 512, BF16), ("vmt", None, BF16), ("gate", 2 * d, BF16),
    ]
    res = pl.pallas_call(
        functools.partial(
            _pre_kernel, fox_scale=FOX_HEAD_DIM ** -0.5 * LOG2E, mla_scale=MLA_QK_DIM ** -0.5 * LOG2E
        ),
        grid=(n // tm,),
        in_specs=[row(d), full(w), full(bf), full(gq), full(wq), full(gkv), full(wkv), full(bg), row(LANES), row(LANES)],
        out_specs=[col if wd is None else row(wd) for _, wd, _ in outs],
        out_shape=[jax.ShapeDtypeStruct((512, n) if wd is None else (n, wd), dt) for _, wd, dt in outs],
        compiler_params=_params("parallel"),
        name="pre_attention",
    )(x, w, bf, gq, wq, gkv, wkv, bg, cos, sin)
    return {name: r for (name, _, _), r in zip(outs, res)}


def _kvup_kernel(c_ref, w_ref, kn_ref, vmt_ref):
    kv = _dot(c_ref[0].astype(BF16), w_ref[0])
    kn_ref[0] = kv[:, 0:512].astype(BF16)
    vmt_ref[0] = kv[:, 512:1024].T.astype(BF16)


def _kv_up(ckv, wkv):
    depth, n, r = ckv.shape
    tm = ROW_TILE
    return pl.pallas_call(
        _kvup_kernel,
        grid=(depth, n // tm),
        in_specs=[pl.BlockSpec((1, tm, r), lambda l, i: (l, i, 0)), pl.BlockSpec((1, r, 1024), lambda l, i: (l, 0, 0))],
        out_specs=[pl.BlockSpec((1, tm, 512), lambda l, i: (l, i, 0)), pl.BlockSpec((1, 512, tm), lambda l, i: (l, 0, i))],
        out_shape=[jax.ShapeDtypeStruct((depth, n, 512), BF16), jax.ShapeDtypeStruct((depth, 512, n), BF16)],
        compiler_params=_params("parallel", "parallel"),
        name="kv_up_cached",
    )(ckv, wkv)


def _cumsum_kernel(x_ref, o_ref, carry_ref, *, tt):
    @pl.when(pl.program_id(1) == 0)
    def _():
        carry_ref[...] = jnp.zeros_like(carry_ref)

    r = lax.broadcasted_iota(jnp.int32, (tt, tt), 0)
    c = lax.broadcasted_iota(jnp.int32, (tt, tt), 1)
    tri = jnp.where(c <= r, 1.0, 0.0).astype(BF16)
    hi, mid, lo = _split3(x_ref[0])
    cum = _dot(tri, lo) + _dot(tri, mid) + _dot(tri, hi) + carry_ref[...]
    carry_ref[...] = cum[tt - 1:tt, :]
    hi, mid, lo = _split3(cum * LOG2E)
    o_ref[0] = jnp.concatenate([hi, mid, lo], axis=1)


def _cumsum_time(x, tt):
    s, t, w = x.shape
    return pl.pallas_call(
        functools.partial(_cumsum_kernel, tt=tt),
        grid=(s, t // tt),
        in_specs=[pl.BlockSpec((1, tt, w), lambda b, i: (b, i, 0))],
        out_specs=pl.BlockSpec((1, tt, 3 * w), lambda b, i: (b, i, 0)),
        out_shape=jax.ShapeDtypeStruct((s, t, 3 * w), BF16),
        scratch_shapes=[pltpu.VMEM((1, w), F32)],
        compiler_params=_params("parallel", "arbitrary"),
        name="cumsum_time",
    )(x)


def _bias_operands(cum3):
    s, t, _ = cum3.shape
    c = cum3.reshape(s, t, 3, LANES)[..., :FOX_HEADS]
    c = jnp.transpose(c, (0, 3, 1, 2)).reshape(s, FOX_HEADS // 2, 2, t, 3)
    c = jnp.transpose(c, (0, 1, 3, 2, 4))
    one = jnp.ones_like(c)
    zero = jnp.zeros(c.shape[:-1] + (AUG_GROUP - 6,), c.dtype)
    pad = lambda a: _pad_lanes(a.reshape(s, FOX_HEADS // 2, t, 2 * AUG_GROUP))
    return pad(jnp.concatenate([c, one, zero], axis=-1)), pad(jnp.concatenate([one, -c, zero], axis=-1))


def _chunk_end(pos, n_meta):
    body = n_meta + ((pos - n_meta) // CHUNK) * CHUNK + CHUNK - 1
    if n_meta == 0:
        return body
    return jnp.where(pos < n_meta, n_meta - 1, body)


def _chunk_id(pos, n_meta):
    return jnp.where(pos < n_meta, 0, 1 + (pos - n_meta) // CHUNK)


def _attn_kernel(q_ref, qx_ref, k_ref, kx_ref, vt_ref, o_ref, *slots, fox, tq, tk, q_off, n_meta, n_ktiles):
    p = pl.program_id(1)
    i = pl.program_id(2)
    q_lo = q_off + i * tq
    q_hi = q_lo + tq - 1
    if fox:
        vis_lo, vis_hi = q_lo, q_hi
    else:
        vis_lo, vis_hi = _chunk_end(q_lo, n_meta), _chunk_end(q_hi, n_meta)
    n_tot = jnp.minimum(vis_hi // tk + 1, n_ktiles)
    n_full = jnp.minimum((vis_lo + 1) // tk, n_tot - 1)

    lane = lax.broadcasted_iota(jnp.int32, (1, LANES), 1)
    q_pos = q_lo + lax.broadcasted_iota(jnp.int32, (1, tq), 1)
    q2 = q_ref[...]
    qx2 = qx_ref[...]
    qc = []
    for hh in range(2):
        x_group = lane // AUG_GROUP == hh if fox else lane // MLA_ROPE_DIM == 2 * (p % 2) + hh
        qc.append(jnp.concatenate([
            jnp.where(lane // FOX_HEAD_DIM == hh, q2, jnp.zeros_like(q2)),
            jnp.where(x_group, qx2, jnp.zeros_like(qx2)),
        ], axis=1))
    qcb = jnp.concatenate(qc, axis=0)
    q_pos2 = jnp.concatenate([q_pos, q_pos], axis=1)
    ones = jnp.ones((ONES_ROWS, tk), BF16)
    n_slots = len(slots)

    def key_rows(j):
        return pl.ds(pl.multiple_of(jnp.minimum(j, n_ktiles - 1) * tk, tk), tk)

    def produce(slot_ref, j, masked):
        ks = key_rows(j)
        kc = jnp.concatenate([k_ref[ks, :], kx_ref[ks, :]], axis=1)
        st = lax.dot_general(kc, qcb, (((1,), (1,)), ((), ())), preferred_element_type=F32)
        if masked:
            k_pos = j * tk + lax.broadcasted_iota(jnp.int32, (tk, 1), 0)
            if fox:
                ok = k_pos <= q_pos2
            else:
                ok = _chunk_id(k_pos, n_meta) <= _chunk_id(q_pos2, n_meta)
            st = jnp.where(ok, st, MASK_VALUE)
        slot_ref[...] = st
        return jnp.max(st, axis=0, keepdims=True)

    def accumulate(j, pt, alpha, accs):
        ks = key_rows(j)
        new = []
        for hh in range(2):
            va = jnp.concatenate([vt_ref[FOX_HEAD_DIM * hh:FOX_HEAD_DIM * (hh + 1), ks], ones], axis=0)
            sl = slice(hh * tq, (hh + 1) * tq)
            new.append(alpha[:, sl] * accs[hh] + _dot(va, pt[:, sl]))
        return tuple(new)

    def step(u, carry, mode):
        m, cmax, accs = carry
        cmax = list(cmax)
        for r in range(n_slots):
            m_new = jnp.maximum(m, cmax[r])
            pt = jnp.exp2(slots[r][...] - m_new).astype(BF16)
            if mode != "drain":
                cmax[r] = produce(slots[r], n_slots * (u + 1) + r, mode == "masked")
            accs = accumulate(n_slots * u + r, pt, jnp.exp2(m - m_new), accs)
            m = m_new
        return m, tuple(cmax), accs

    m0 = jnp.full((1, 2 * tq), MASK_VALUE, F32)
    a0 = jnp.zeros((FOX_HEAD_DIM + ONES_ROWS, tq), F32)
    carry = (m0, tuple(produce(slots[r], r, True) for r in range(n_slots)), (a0, a0))
    n_steps = (n_tot + n_slots - 1) // n_slots
    n_plain = jnp.maximum(n_full - n_slots, 0) // n_slots
    carry = lax.fori_loop(0, n_plain, functools.partial(step, mode="plain"), carry)
    carry = lax.fori_loop(n_plain, n_steps - 1, functools.partial(step, mode="masked"), carry)
    _, _, accs = step(n_steps - 1, carry, "drain")
    out_t = jnp.concatenate(
        [acc[0:FOX_HEAD_DIM] / acc[FOX_HEAD_DIM:FOX_HEAD_DIM + 1] for acc in accs], axis=0
    )
    o_ref[...] = out_t.T.astype(o_ref.dtype)


def _attention(q, qx, k, kx, vt, *, fox, tq, tk, q_off, n_meta):
    b, t_q, _ = q.shape
    t_k = k.shape[1]
    qspec = pl.BlockSpec((None, tq, LANES), lambda bb, p, i: (bb, i, p))
    kspec = pl.BlockSpec((None, t_k, LANES), lambda bb, p, i: (bb, 0, p))
    if fox:
        qxspec = pl.BlockSpec((None, None, tq, LANES), lambda bb, p, i: (bb, p, i, 0))
        kxspec = pl.BlockSpec((None, None, t_k, LANES), lambda bb, p, i: (bb, p, 0, 0))
    else:
        qxspec = pl.BlockSpec((None, tq, LANES), lambda bb, p, i: (bb, i, p // 2))
        kxspec = pl.BlockSpec((None, t_k, LANES), lambda bb, p, i: (bb, 0, 0))
    if vt.ndim == 2:
        vspec = pl.BlockSpec((LANES, t_k), lambda bb, p, i: (p, bb))
    else:
        vspec = pl.BlockSpec((None, LANES, t_k), lambda bb, p, i: (bb, p, 0))
    return pl.pallas_call(
        functools.partial(_attn_kernel, fox=fox, tq=tq, tk=tk, q_off=q_off, n_meta=n_meta, n_ktiles=t_k // tk),
        grid=(b, 4, t_q // tq),
        in_specs=[qspec, qxspec, kspec, kxspec, vspec],
        out_specs=qspec,
        out_shape=jax.ShapeDtypeStruct((b, t_q, 512), BF16),
        scratch_shapes=[pltpu.VMEM((tk, 2 * tq), F32)] * ATTN_SLOTS,
        compiler_params=_params("parallel", "parallel", "arbitrary"),
        name="fox_attention" if fox else "mla_attention",
    )(q, qx, k, kx, vt)


def _group_partner(x, lane, d, width):
    fwd = pltpu.roll(x, LANES - d, 1)
    back = pltpu.roll(x, width - d, 1)
    return jnp.where(lane % width + d < width, fwd, back)


def _route(logits, b_router, lane):
    scores = jax.nn.sigmoid(logits)
    sel = scores + b_router
    e = EXPERTS_PER_GROUP
    rank = jnp.zeros(sel.shape, jnp.int32)
    for d in range(1, e):
        other = _group_partner(sel, lane, d, e)
        other_first = lane % e + d >= e
        beats = (other > sel) | ((other == sel) & other_first)
        rank = rank + beats.astype(jnp.int32)
    top2 = rank < 2
    kept = jnp.where(top2, sel, 0.0)
    gscore = kept
    for d in range(1, e):
        gscore = gscore + _group_partner(kept, lane, d, e)
    grank = jnp.zeros_like(rank)
    for gstep in range(1, N_GROUPS):
        d = gstep * e
        other = _group_partner(gscore, lane, d, N_EXPERTS)
        other_first = lane % N_EXPERTS + d >= N_EXPERTS
        beats = (other > gscore) | ((other == gscore) & other_first)
        grank = grank + beats.astype(jnp.int32)
    chosen = top2 & (grank == 0) & (lane < N_EXPERTS)
    w = jnp.where(chosen, scores, 0.0)
    return w / jnp.sum(w, axis=1, keepdims=True)


def _post_kernel(x_ref, fo_ref, mo_ref, gate_ref, pa_ref, pb_ref, wo_ref, g1_ref, b1_ref, wr_ref, br_ref,
                 x1_ref, gates_ref, *, alpha):
    d = x_ref.shape[1]
    merged = gate_ref[:, 0:d].astype(F32) * _dot(fo_ref[...], pa_ref[...])
    merged = merged + gate_ref[:, d:2 * d].astype(F32) * _dot(mo_ref[...], pb_ref[...])
    mix = _dot(merged.astype(BF16), wo_ref[...])
    x1 = _layer_norm(alpha * x_ref[...] + mix, g1_ref[...], b1_ref[...])
    x1_ref[...] = x1
    hi, mid, lo = _split3(x1)
    whi, wmid, wlo = wr_ref[0], wr_ref[1], wr_ref[2]
    logits = _dot(lo, whi) + _dot(mid, wmid) + _dot(hi, wlo) + _dot(mid, whi) + _dot(hi, wmid) + _dot(hi, whi)
    lane = lax.broadcasted_iota(jnp.int32, logits.shape, 1)
    gates_ref[...] = _route(logits, br_ref[...], lane)


def _post_attention(x, fo, mo, gate, pa, pb, wo, g1, b1, wr3, br, alpha):
    n, d = x.shape
    tm = ROW_TILE
    row = lambda width: pl.BlockSpec((tm, width), lambda i: (i, 0))
    full = lambda a: pl.BlockSpec(a.shape, lambda i: (0,) * a.ndim)
    return pl.pallas_call(
        functools.partial(_post_kernel, alpha=alpha),
        grid=(n // tm,),
        in_specs=[row(d), row(512), row(512), row(2 * d), full(pa), full(pb), full(wo), full(g1), full(b1), full(wr3), full(br)],
        out_specs=[row(d), row(LANES)],
        out_shape=[jax.ShapeDtypeStruct((n, d), F32), jax.ShapeDtypeStruct((n, LANES), F32)],
        compiler_params=_params("parallel"),
        name="post_attention",
    )(x, fo, mo, gate, pa, pb, wo, g1, b1, wr3, br)


def _moe_kernel(x_ref, gates_ref, wg_ref, wu_ref, wd_ref, g2_ref, b2_ref, o_ref, xb_ref, acc_ref, *, alpha):
    e = pl.program_id(1)

    @pl.when(e == 0)
    def _():
        xb_ref[...] = x_ref[...].astype(BF16)
        acc_ref[...] = jnp.zeros_like(acc_ref)

    lane = lax.broadcasted_iota(jnp.int32, (1, LANES), 1)
    gate = jnp.sum(jnp.where(lane == e, gates_ref[...], 0.0), axis=1, keepdims=True)
    xb = xb_ref[...]
    hid = jax.nn.silu(_dot(xb, wg_ref[0])) * _dot(xb, wu_ref[0])
    acc_ref[...] += _dot((hid * gate).astype(BF16), wd_ref[0])

    @pl.when(e == pl.num_programs(1) - 1)
    def _():
        o_ref[...] = _layer_norm(alpha * x_ref[...] + acc_ref[...], g2_ref[...], b2_ref[...])


def _moe(x, gates, wg, wu, wd, g2, b2, alpha):
    n, d = x.shape
    n_exp, _, f = wg.shape
    tm = MOE_ROW_TILE
    return pl.pallas_call(
        functools.partial(_moe_kernel, alpha=alpha),
        grid=(n // tm, n_exp),
        in_specs=[
            pl.BlockSpec((tm, d), lambda i, e: (i, 0)),
            pl.BlockSpec((tm, LANES), lambda i, e: (i, 0)),
            pl.BlockSpec((1, d, f), lambda i, e: (e, 0, 0)),
            pl.BlockSpec((1, d, f), lambda i, e: (e, 0, 0)),
            pl.BlockSpec((1, f, d), lambda i, e: (e, 0, 0)),
            pl.BlockSpec((1, d), lambda i, e: (0, 0)),
            pl.BlockSpec((1, d), lambda i, e: (0, 0)),
        ],
        out_specs=pl.BlockSpec((tm, d), lambda i, e: (i, 0)),
        out_shape=jax.ShapeDtypeStruct((n, d), F32),
        scratch_shapes=[pltpu.VMEM((tm, d), BF16), pltpu.VMEM((tm, d), F32)],
        compiler_params=_params("parallel", "arbitrary"),
        name="moe_ln2",
    )(x, gates, wg, wu, wd, g2, b2)


def _rope_tables(pos):
    half = MLA_ROPE_DIM // 2
    inv_freq = ROPE_THETA ** (-jnp.arange(half, dtype=F32) / half)
    ang = pos.astype(F32)[:, None] * inv_freq[None, :]
    cos, sin = jnp.cos(ang), jnp.sin(ang)
    cos32 = jnp.concatenate([cos, cos], axis=1)
    sin32 = jnp.concatenate([-sin, sin], axis=1)
    return jnp.tile(cos32, (1, LANES // MLA_ROPE_DIM)), jnp.tile(sin32, (1, LANES // MLA_ROPE_DIM))


def _swap_halves(w):
    half = MLA_ROPE_DIM // 2
    return jnp.concatenate([w[..., half:], w[..., :half]], axis=-1)


def _prepare_w_in(w_in):
    depth, d, _ = w_in.shape
    o = [0]
    for sz in (FOX_WIDTH, FOX_WIDTH, FOX_WIDTH, FOX_HEADS, 256, 128, MLA_ROPE_DIM, 2 * d):
        o.append(o[-1] + sz)
    qkv = w_in[..., o[0]:o[3]]
    wf = w_in[..., o[3]:o[4]]
    wcq = w_in[..., o[4]:o[5]]
    wckv = w_in[..., o[5]:o[6]]
    wkr = w_in[..., o[6]:o[7]]
    wg = w_in[..., o[7]:o[8]]
    reps = LANES // MLA_ROPE_DIM
    wf_pad = jnp.concatenate([wf, jnp.zeros((depth, d, LANES - FOX_HEADS), w_in.dtype)], axis=-1)
    return jnp.concatenate(
        [qkv, wcq, wckv, wf_pad, jnp.tile(wkr, (1, 1, reps)), jnp.tile(_swap_halves(wkr), (1, 1, reps)), wg], axis=-1
    ).astype(BF16)


def _prepare_w_q_up(w):
    depth, r, _ = w.shape
    w = w.reshape(depth, r, MLA_HEADS, MLA_QK_DIM)
    nope = w[..., :MLA_NOPE_DIM].reshape(depth, r, MLA_HEADS * MLA_NOPE_DIM)
    rope = w[..., MLA_NOPE_DIM:]
    rope_sw = _swap_halves(rope).reshape(depth, r, MLA_HEADS * MLA_ROPE_DIM)
    rope = rope.reshape(depth, r, MLA_HEADS * MLA_ROPE_DIM)
    return jnp.concatenate([nope, rope, rope_sw], axis=-1).astype(BF16)


def _prepare_w_kv_up(w):
    depth, r, _ = w.shape
    w = w.reshape(depth, r, MLA_HEADS, MLA_NOPE_DIM + MLA_V_DIM)
    kn = w[..., :MLA_NOPE_DIM].reshape(depth, r, MLA_HEADS * MLA_NOPE_DIM)
    vm = w[..., MLA_NOPE_DIM:].reshape(depth, r, MLA_HEADS * MLA_V_DIM)
    return jnp.concatenate([kn, vm], axis=-1).astype(BF16)


def _pad_lanes(x, width=LANES):
    return jnp.pad(x, [(0, 0)] * (x.ndim - 1) + [(0, width - x.shape[-1])])


def kernel(x_prompt, x_sample, cache_fox_k, cache_fox_v, cache_fox_logf, cache_mla_ckv, cache_mla_krope, meta_tokens, ln_in_g, ln_in_b, w_in, b_fgt, g_q_norm, w_q_up, g_kv_norm, w_kv_up, w_proj_fox, w_proj_mla, b_gate, w_out, ln1_g, ln1_b, ln2_g, ln2_b, w_router, b_router, w_exp_gate, w_exp_up, w_exp_down):
    bp, seq, d = x_prompt.shape
    bs, tn, _ = x_sample.shape
    depth = w_in.shape[0]
    past = cache_fox_k.shape[2]
    alpha = (2 * depth) ** 0.25
    t_real = N_META + seq
    tp = -(-t_real // ATTN_TILE) * ATTN_TILE
    n_p = bp * tp
    n_s = bs * tn
    n = n_p + n_s
    n_pad = -(-n // MOE_ROW_TILE) * MOE_ROW_TILE
    tk_s = -(-(past + tn) // SAMPLE_KEY_TILE) * SAMPLE_KEY_TILE
    tq_s = SAMPLE_QUERY_TILE

    meta = jnp.broadcast_to(meta_tokens[None].astype(F32), (bp, N_META, d))
    xp = jnp.concatenate([meta, x_prompt, jnp.zeros((bp, tp - t_real, d), F32)], axis=1)
    x = jnp.concatenate([xp.reshape(n_p, d), x_sample.reshape(n_s, d), jnp.zeros((n_pad - n, d), F32)], axis=0)
    pos = jnp.concatenate([
        jnp.tile(jnp.arange(tp, dtype=jnp.int32), bp),
        jnp.tile(past + jnp.arange(tn, dtype=jnp.int32), bs),
        jnp.zeros((n_pad - n,), jnp.int32),
    ])
    cos, sin = _rope_tables(pos)

    w_in_r = _prepare_w_in(w_in)
    w_q_r = _prepare_w_q_up(w_q_up)
    w_kv_r = _prepare_w_kv_up(w_kv_up)
    bf_r = _pad_lanes(b_fgt)[:, None, :]
    pa, pb, wo = w_proj_fox.astype(BF16), w_proj_mla.astype(BF16), w_out.astype(BF16)
    wg, wu, wd = w_exp_gate.astype(BF16), w_exp_up.astype(BF16), w_exp_down.astype(BF16)
    wr = _pad_lanes(w_router)
    wr_hi = wr.astype(BF16)
    wr_mid = (wr - wr_hi.astype(F32)).astype(BF16)
    wr_lo = (wr - wr_hi.astype(F32) - wr_mid.astype(F32)).astype(BF16)
    wr3 = jnp.stack([wr_hi, wr_mid, wr_lo])
    br = _pad_lanes(b_router[None, :])

    zpad = tk_s - past - tn
    kn_c, vmt_c = _kv_up(cache_mla_ckv.reshape(depth, bs * past, -1), w_kv_r)
    kn_c = kn_c.reshape(depth, bs, past, 512)
    vmt_c = jnp.transpose(vmt_c.reshape(depth, 512, bs, past), (0, 2, 1, 3))
    kf_c = cache_fox_k.reshape(depth, bs, past, FOX_WIDTH).astype(BF16)
    vft_c = jnp.transpose(cache_fox_v.reshape(depth, bs, past, FOX_WIDTH), (0, 1, 3, 2)).astype(BF16)
    kr_c = jnp.tile(cache_mla_krope, (1, 1, 1, LANES // MLA_ROPE_DIM)).astype(BF16)
    logf_c = _pad_lanes(cache_fox_logf)

    def with_past(cached, new):
        new = new.reshape(bs, tn, new.shape[-1])
        return jnp.concatenate([cached, new, jnp.zeros((bs, zpad, new.shape[-1]), new.dtype)], axis=1)

    def with_past_t(cached_t, new_t):
        new_t = jnp.transpose(new_t.reshape(512, bs, tn), (1, 0, 2))
        return jnp.concatenate([cached_t, new_t, jnp.zeros((bs, 512, zpad), new_t.dtype)], axis=2)

    def sample_queries(a):
        a = a.reshape(bs, tn, a.shape[-1])
        return jnp.pad(a, ((0, 0), (0, tq_s - tn), (0, 0)))

    x = _input_ln(x, ln_in_g[None, :], ln_in_b[None, :])
    rows = []
    for l in range(depth):
        pre = _pre_attention(
            x, w_in_r[l], bf_r[l], g_q_norm[l][None, :], w_q_r[l], g_kv_norm[l][None, :], w_kv_r[l],
            b_gate[l][None, :], cos, sin,
        )
        grp_p = lambda a: a[:n_p].reshape(bp, tp, a.shape[-1])

        qx_p, kx_p = _bias_operands(_cumsum_time(grp_p(pre["logf"]), ATTN_TILE))
        fox_p = _attention(
            grp_p(pre["qf"]), qx_p, grp_p(pre["kf"]), kx_p, pre["vft"],
            fox=True, tq=ATTN_TILE, tk=ATTN_TILE, q_off=0, n_meta=N_META,
        )
        mla_p = _attention(
            grp_p(pre["qn"]), grp_p(pre["qr"]), grp_p(pre["kn"]), grp_p(pre["kr"]), pre["vmt"],
            fox=False, tq=ATTN_TILE, tk=ATTN_TILE, q_off=0, n_meta=N_META,
        )

        qx_s, kx_s = _bias_operands(_cumsum_time(with_past(logf_c[l], pre["logf"][n_p:n]), SAMPLE_KEY_TILE))
        qx_s = jnp.pad(qx_s[:, :, past:past + tn], ((0, 0), (0, 0), (0, tq_s - tn), (0, 0)))
        fox_s = _attention(
            sample_queries(pre["qf"][n_p:n]), qx_s, with_past(kf_c[l], pre["kf"][n_p:n]), kx_s,
            with_past_t(vft_c[l], pre["vft"][:, n_p:n]),
            fox=True, tq=tq_s, tk=SAMPLE_KEY_TILE, q_off=past, n_meta=0,
        )
        mla_s = _attention(
            sample_queries(pre["qn"][n_p:n]), sample_queries(pre["qr"][n_p:n]), with_past(kn_c[l], pre["kn"][n_p:n]),
            with_past(kr_c[l], pre["kr"][n_p:n]), with_past_t(vmt_c[l], pre["vmt"][:, n_p:n]),
            fox=False, tq=tq_s, tk=SAMPLE_KEY_TILE, q_off=past, n_meta=0,
        )

        tail = jnp.zeros((n_pad - n, 512), BF16)
        fo = jnp.concatenate([fox_p.reshape(n_p, 512), fox_s[:, :tn].reshape(n_s, 512), tail], axis=0)
        mo = jnp.concatenate([mla_p.reshape(n_p, 512), mla_s[:, :tn].reshape(n_s, 512), tail], axis=0)
        x1, gates = _post_attention(
            x, fo, mo, pre["gate"], pa[l], pb[l], wo[l], ln1_g[l][None, :], ln1_b[l][None, :], wr3, br, alpha
        )
        x = _moe(x1, gates, wg[l], wu[l], wd[l], ln2_g[l][None, :], ln2_b[l][None, :], alpha)
        rows.append(pre)

    def stack_p(name, width):
        return jnp.stack([r[name][:n_p].reshape(bp, tp, -1)[:, :t_real, :width] for r in rows])

    def stack_s(name, width):
        return jnp.stack([r[name][n_p:n].reshape(bs, tn, -1)[:, :, :width] for r in rows])

    heads = (FOX_HEADS, FOX_HEAD_DIM)
    y_prompt = x[:n_p].reshape(bp, tp, d)[:, N_META:t_real]
    y_sample = x[n_p:n].reshape(bs, tn, d)
    return (
        y_prompt,
        y_sample,
        stack_p("kf32", 512).reshape(depth, bp, t_real, *heads),
        stack_p("vf32", 512).reshape(depth, bp, t_real, *heads),
        stack_p("logf", FOX_HEADS),
        stack_p("ckv", 128),
        stack_p("kr32", MLA_ROPE_DIM),
        stack_s("kf32", 512).reshape(depth, bs, tn, *heads),
        stack_s("vf32", 512).reshape(depth, bs, tn, *heads),
        stack_s("logf", FOX_HEADS),
        stack_s("ckv", 128),
        stack_s("kr32", MLA_ROPE_DIM),
    )
```

```python
import functools

import jax
import jax.numpy as jnp
from jax import lax
from jax.experimental import pallas as pl
from jax.experimental.pallas import tpu as pltpu

F32 = jnp.float32
BF16 = jnp.bfloat16

CHUNK = 64
N_META = 16
FOX_HEADS = 8
FOX_HEAD_DIM = 64
FOX_WIDTH = FOX_HEADS * FOX_HEAD_DIM
MLA_HEADS = 8
MLA_NOPE_DIM = 64
MLA_ROPE_DIM = 32
MLA_QK_DIM = MLA_NOPE_DIM + MLA_ROPE_DIM
MLA_V_DIM = 64
N_EXPERTS = 16
N_GROUPS = 4
EXPERTS_PER_GROUP = N_EXPERTS // N_GROUPS
ROPE_THETA = 10000.0
LN_EPS = 1e-5
RMS_EPS = 1e-6
LOG2E = 1.4426950408889634

LANES = 128
ROW_TILE = 256
MOE_ROW_TILE = 512
ATTN_TILE = 256
ATTN_SLOTS = 4
QK_GROUP = 2
SAMPLE_QUERY_TILE = 128
SAMPLE_KEY_TILE = 128
AUG_GROUP = 8
ONES_ROWS = 16
MASK_VALUE = -1e30
VMEM_LIMIT = 56 * 1024 * 1024

C_QKV = 0
C_CQ = 3 * FOX_WIDTH
C_CKV = C_CQ + 256
C_LOGF = C_CKV + 128
C_KR = C_LOGF + LANES
C_KRS = C_KR + LANES
C_GATE = C_KRS + LANES


def _params(*sem):
    return pltpu.CompilerParams(dimension_semantics=sem, vmem_limit_bytes=VMEM_LIMIT)


def _layer_norm(x, g, b):
    mu = jnp.mean(x, axis=-1, keepdims=True)
    xc = x - mu
    var = jnp.mean(xc * xc, axis=-1, keepdims=True)
    return xc * lax.rsqrt(var + LN_EPS) * g + b


def _rms_norm(x, g):
    return x * lax.rsqrt(jnp.mean(x * x, axis=-1, keepdims=True) + RMS_EPS) * g


def _dot(a, b):
    return jnp.dot(a, b, preferred_element_type=F32)


def _split3(x):
    hi = x.astype(BF16)
    r1 = x - hi.astype(F32)
    mid = r1.astype(BF16)
    lo = (r1 - mid.astype(F32)).astype(BF16)
    return hi, mid, lo


def _ln_kernel(x_ref, g_ref, b_ref, o_ref):
    o_ref[...] = _layer_norm(x_ref[...], g_ref[...], b_ref[...])


def _input_ln(x, g, b):
    n, d = x.shape
    return pl.pallas_call(
        _ln_kernel,
        grid=(n // ROW_TILE,),
        in_specs=[
            pl.BlockSpec((ROW_TILE, d), lambda i: (i, 0)),
            pl.BlockSpec((1, d), lambda i: (0, 0)),
            pl.BlockSpec((1, d), lambda i: (0, 0)),
        ],
        out_specs=pl.BlockSpec((ROW_TILE, d), lambda i: (i, 0)),
        out_shape=jax.ShapeDtypeStruct((n, d), F32),
        compiler_params=_params("parallel"),
        name="input_ln",
    )(x, g, b)


def _pre_kernel(
    x_ref, w_ref, bf_ref, gq_ref, wq_ref, gkv_ref, wkv_ref, bg_ref, cos_ref, sin_ref,
    qft_ref, kf32_ref, vf32_ref, kf_ref, vft_ref, logf_ref, qnt_ref, qrt_ref, ckv_ref, kr32_ref, kr_ref,
    kn_ref, vmt_ref, gate_ref, *, fox_scale, mla_scale,
):
    xb = x_ref[...].astype(BF16)
    cos = cos_ref[...]
    sin = sin_ref[...]

    qkv = _dot(xb, w_ref[:, C_QKV:C_CQ])
    qft_ref[...] = (qkv[:, 0:FOX_WIDTH] * fox_scale).T.astype(BF16)
    k = qkv[:, FOX_WIDTH:2 * FOX_WIDTH]
    v = qkv[:, 2 * FOX_WIDTH:3 * FOX_WIDTH]
    kf32_ref[...] = k
    vf32_ref[...] = v
    kf_ref[...] = k.astype(BF16)
    vft_ref[...] = v.T.astype(BF16)

    z = _dot(xb, w_ref[:, C_LOGF:C_KR]) + bf_ref[...]
    logf_ref[...] = jnp.minimum(z, 0.0) - jnp.log1p(jnp.exp(-jnp.abs(z)))

    cq = _rms_norm(_dot(xb, w_ref[:, C_CQ:C_CKV]), gq_ref[...]).astype(BF16)
    qm = _dot(cq, wq_ref[...])
    qnt_ref[...] = (qm[:, 0:512] * mla_scale).T.astype(BF16)
    cos2 = jnp.concatenate([cos, cos], axis=1)
    sin2 = jnp.concatenate([sin, sin], axis=1)
    qrt_ref[...] = ((qm[:, 512:768] * cos2 + qm[:, 768:1024] * sin2) * mla_scale).T.astype(BF16)

    ckv = _rms_norm(_dot(xb, w_ref[:, C_CKV:C_LOGF]), gkv_ref[...])
    ckv_ref[...] = ckv
    kv = _dot(ckv.astype(BF16), wkv_ref[...])
    kn_ref[...] = kv[:, 0:512].astype(BF16)
    vmt_ref[...] = kv[:, 512:1024].T.astype(BF16)
    kr = _dot(xb, w_ref[:, C_KR:C_KRS]) * cos + _dot(xb, w_ref[:, C_KRS:C_GATE]) * sin
    kr32_ref[...] = kr
    kr_ref[...] = kr.astype(BF16)

    g = _dot(xb, w_ref[:, C_GATE:]) + bg_ref[...]
    gate_ref[...] = jax.nn.sigmoid(g).astype(BF16)


def _pre_attention(x, w, bf, gq, wq, gkv, wkv, bg, cos, sin):
    n, d = x.shape
    tm = ROW_TILE
    row = lambda width: pl.BlockSpec((tm, width), lambda i: (i, 0))
    col = lambda height: pl.BlockSpec((height, tm), lambda i: (0, i))
    full = lambda a: pl.BlockSpec(a.shape, lambda i: (0,) * a.ndim)
    outs = [
        ("qft", 512, BF16, True), ("kf32", 512, F32, False), ("vf32", 512, F32, False), ("kf", 512, BF16, False),
        ("vft", 512, BF16, True), ("logf", LANES, F32, False), ("qnt", 512, BF16, True), ("qrt", 256, BF16, True),
        ("ckv", 128, F32, False), ("kr32", LANES, F32, False), ("kr", LANES, BF16, False), ("kn", 512, BF16, False),
        ("vmt", 512, BF16, True), ("gate", 2 * d, BF16, False),
    ]
    res = pl.pallas_call(
        functools.partial(
            _pre_kernel, fox_scale=FOX_HEAD_DIM ** -0.5 * LOG2E, mla_scale=MLA_QK_DIM ** -0.5 * LOG2E
        ),
        grid=(n // tm,),
        in_specs=[row(d), full(w), full(bf), full(gq), full(wq), full(gkv), full(wkv), full(bg), row(LANES), row(LANES)],
        out_specs=[col(wd) if tr else row(wd) for _, wd, _, tr in outs],
        out_shape=[jax.ShapeDtypeStruct((wd, n) if tr else (n, wd), dt) for _, wd, dt, tr in outs],
        compiler_params=_params("parallel"),
        name="pre_attention",
    )(x, w, bf, gq, wq, gkv, wkv, bg, cos, sin)
    return {name: r for (name, _, _, _), r in zip(outs, res)}


def _kvup_kernel(c_ref, w_ref, kn_ref, vmt_ref):
    kv = _dot(c_ref[0].astype(BF16), w_ref[0])
    kn_ref[0] = kv[:, 0:512].astype(BF16)
    vmt_ref[0] = kv[:, 512:1024].T.astype(BF16)


def _kv_up(ckv, wkv):
    depth, n, r = ckv.shape
    tm = ROW_TILE
    return pl.pallas_call(
        _kvup_kernel,
        grid=(depth, n // tm),
        in_specs=[pl.BlockSpec((1, tm, r), lambda l, i: (l, i, 0)), pl.BlockSpec((1, r, 1024), lambda l, i: (l, 0, 0))],
        out_specs=[pl.BlockSpec((1, tm, 512), lambda l, i: (l, i, 0)), pl.BlockSpec((1, 512, tm), lambda l, i: (l, 0, i))],
        out_shape=[jax.ShapeDtypeStruct((depth, n, 512), BF16), jax.ShapeDtypeStruct((depth, 512, n), BF16)],
        compiler_params=_params("parallel", "parallel"),
        name="kv_up_cached",
    )(ckv, wkv)


def _cumsum_kernel(x_ref, o_ref, carry_ref, *, tt):
    @pl.when(pl.program_id(1) == 0)
    def _():
        carry_ref[...] = jnp.zeros_like(carry_ref)

    r = lax.broadcasted_iota(jnp.int32, (tt, tt), 0)
    c = lax.broadcasted_iota(jnp.int32, (tt, tt), 1)
    tri = jnp.where(c <= r, 1.0, 0.0).astype(BF16)
    hi, mid, lo = _split3(x_ref[0])
    cum = _dot(tri, lo) + _dot(tri, mid) + _dot(tri, hi) + carry_ref[...]
    carry_ref[...] = cum[tt - 1:tt, :]
    hi, mid, lo = _split3(cum * LOG2E)
    o_ref[0] = jnp.concatenate([hi, mid, lo], axis=1)


def _cumsum_time(x, tt):
    s, t, w = x.shape
    return pl.pallas_call(
        functools.partial(_cumsum_kernel, tt=tt),
        grid=(s, t // tt),
        in_specs=[pl.BlockSpec((1, tt, w), lambda b, i: (b, i, 0))],
        out_specs=pl.BlockSpec((1, tt, 3 * w), lambda b, i: (b, i, 0)),
        out_shape=jax.ShapeDtypeStruct((s, t, 3 * w), BF16),
        scratch_shapes=[pltpu.VMEM((1, w), F32)],
        compiler_params=_params("parallel", "arbitrary"),
        name="cumsum_time",
    )(x)


def _bias_operands(cum3):
    s, t, _ = cum3.shape
    c = cum3.reshape(s, t, 3, LANES)[..., :FOX_HEADS]
    c = jnp.transpose(c, (0, 3, 1, 2)).reshape(s, FOX_HEADS // 2, 2, t, 3)
    c = jnp.transpose(c, (0, 1, 3, 2, 4))
    one = jnp.ones_like(c)
    zero = jnp.zeros(c.shape[:-1] + (AUG_GROUP - 6,), c.dtype)
    pad = lambda a: _pad_lanes(a.reshape(s, FOX_HEADS // 2, t, 2 * AUG_GROUP))
    q_op = jnp.transpose(pad(jnp.concatenate([c, one, zero], axis=-1)), (0, 1, 3, 2))
    return q_op, pad(jnp.concatenate([one, -c, zero], axis=-1))


def _chunk_end(pos, n_meta):
    body = n_meta + ((pos - n_meta) // CHUNK) * CHUNK + CHUNK - 1
    if n_meta == 0:
        return body
    return jnp.where(pos < n_meta, n_meta - 1, body)


def _chunk_id(pos, n_meta):
    return jnp.where(pos < n_meta, 0, 1 + (pos - n_meta) // CHUNK)


def _attn_kernel(qt_ref, qxt_ref, k_ref, kx_ref, vt_ref, o_ref, *slots, fox, tq, tk, q_off, n_meta, n_ktiles):
    p = pl.program_id(1)
    i = pl.program_id(2)
    q_lo = q_off + i * tq
    q_hi = q_lo + tq - 1
    if fox:
        vis_lo, vis_hi = q_lo, q_hi
    else:
        vis_lo, vis_hi = _chunk_end(q_lo, n_meta), _chunk_end(q_hi, n_meta)
    n_tot = jnp.minimum(vis_hi // tk + 1, n_ktiles)
    n_full = jnp.minimum((vis_lo + 1) // tk, n_tot - 1)

    q_pos = q_lo + lax.broadcasted_iota(jnp.int32, (1, tq), 1)
    row = lax.broadcasted_iota(jnp.int32, (LANES, 1), 0)
    qt2 = qt_ref[...]
    qxt2 = qxt_ref[...]
    cols = []
    for hh in range(2):
        x_group = row // AUG_GROUP == hh if fox else row // MLA_ROPE_DIM == 2 * (p % 2) + hh
        cols.append(jnp.concatenate([
            jnp.where(row // FOX_HEAD_DIM == hh, qt2, jnp.zeros_like(qt2)),
            jnp.where(x_group, qxt2, jnp.zeros_like(qxt2)),
        ], axis=0))
    qcbt = jnp.concatenate(cols, axis=1)
    last_visible = q_pos if fox else _chunk_end(q_pos, n_meta)
    last_visible = jnp.concatenate([last_visible, last_visible], axis=1)
    key_iota = lax.broadcasted_iota(jnp.int32, (tk, 2 * tq), 0)
    ones = jnp.ones((ONES_ROWS, tk), BF16)
    n_slots = len(slots)

    def key_rows(j):
        return pl.ds(pl.multiple_of(jnp.minimum(j, n_ktiles - 1) * tk, tk), tk)

    def produce(slot_refs, j, masked):
        kcs = []
        for r in range(len(slot_refs)):
            ks = key_rows(j + r)
            kcs.append(jnp.concatenate([k_ref[ks, :], kx_ref[ks, :]], axis=1))
        st_all = _dot(jnp.concatenate(kcs, axis=0), qcbt)
        cmax = []
        for r, slot_ref in enumerate(slot_refs):
            st = st_all[r * tk:(r + 1) * tk]
            if masked:
                st = jnp.where(key_iota <= last_visible - (j + r) * tk, st, MASK_VALUE)
            slot_ref[...] = st
            cmax.append(jnp.max(st, axis=0, keepdims=True))
        return cmax

    def accumulate(j, pt, alpha, accs):
        ks = key_rows(j)
        new = []
        for hh in range(2):
            va = jnp.concatenate([vt_ref[FOX_HEAD_DIM * hh:FOX_HEAD_DIM * (hh + 1), ks], ones], axis=0)
            sl = slice(hh * tq, (hh + 1) * tq)
            new.append(alpha[:, sl] * accs[hh] + _dot(va, pt[:, sl]))
        return tuple(new)

    def step(u, carry, mode):
        m, cmax, accs = carry
        cmax = list(cmax)
        for r0 in range(0, n_slots, QK_GROUP):
            group = range(r0, r0 + QK_GROUP)
            pts, alphas = [], []
            for r in group:
                m_new = jnp.maximum(m, cmax[r])
                pts.append(jnp.exp2(slots[r][...] - m_new).astype(BF16))
                alphas.append(jnp.exp2(m - m_new))
                m = m_new
            if mode != "drain":
                cmax[r0:r0 + QK_GROUP] = produce([slots[r] for r in group], n_slots * (u + 1) + r0, mode == "masked")
            for r, pt, alpha in zip(group, pts, alphas):
                accs = accumulate(n_slots * u + r, pt, alpha, accs)
        return m, tuple(cmax), accs

    def two_steps(v, carry):
        return step(2 * v + 1, step(2 * v, carry, "plain"), "plain")

    m0 = jnp.full((1, 2 * tq), MASK_VALUE, F32)
    a0 = jnp.zeros((FOX_HEAD_DIM + ONES_ROWS, tq), F32)
    carry = (m0, tuple(produce(list(slots), 0, True)), (a0, a0))
    n_steps = (n_tot + n_slots - 1) // n_slots
    n_plain = jnp.maximum(n_full - n_slots, 0) // n_slots
    carry = lax.fori_loop(0, n_plain // 2, two_steps, carry)
    carry = lax.fori_loop(n_plain - n_plain % 2, n_plain, functools.partial(step, mode="plain"), carry)
    carry = lax.fori_loop(n_plain, n_steps - 1, functools.partial(step, mode="masked"), carry)
    _, _, accs = step(n_steps - 1, carry, "drain")
    out_t = jnp.concatenate(
        [acc[0:FOX_HEAD_DIM] / acc[FOX_HEAD_DIM:FOX_HEAD_DIM + 1] for acc in accs], axis=0
    )
    o_ref[...] = out_t.T.astype(o_ref.dtype)


def _attention(qt, qxt, k, kx, vt, *, fox, n_batch, t_q, tq, tk, q_off, n_meta):
    t_k = k.shape[1]

    def time_major(a, rows_of, t, tile):
        tiles = t // tile
        if a.ndim == 2:
            return pl.BlockSpec((LANES, tile), lambda bb, p, i: (rows_of(p), bb * tiles + i % tiles))
        return pl.BlockSpec((None, LANES, tile), lambda bb, p, i: (bb, rows_of(p), i % tiles))

    qspec = time_major(qt, lambda p: p, t_q, tq)
    kspec = pl.BlockSpec((None, t_k, LANES), lambda bb, p, i: (bb, 0, p))
    if fox:
        qxspec = pl.BlockSpec((None, None, LANES, tq), lambda bb, p, i: (bb, p, 0, i))
        kxspec = pl.BlockSpec((None, None, t_k, LANES), lambda bb, p, i: (bb, p, 0, 0))
    else:
        qxspec = time_major(qxt, lambda p: p // 2, t_q, tq)
        kxspec = pl.BlockSpec((None, t_k, LANES), lambda bb, p, i: (bb, 0, 0))
    vspec = time_major(vt, lambda p: p, t_k, t_k)
    return pl.pallas_call(
        functools.partial(_attn_kernel, fox=fox, tq=tq, tk=tk, q_off=q_off, n_meta=n_meta, n_ktiles=t_k // tk),
        grid=(n_batch, 4, t_q // tq),
        in_specs=[qspec, qxspec, kspec, kxspec, vspec],
        out_specs=pl.BlockSpec((None, tq, LANES), lambda bb, p, i: (bb, i, p)),
        out_shape=jax.ShapeDtypeStruct((n_batch, t_q, 512), BF16),
        scratch_shapes=[pltpu.VMEM((tk, 2 * tq), F32)] * ATTN_SLOTS,
        compiler_params=_params("parallel", "parallel", "arbitrary"),
        name="fox_attention" if fox else "mla_attention",
    )(qt, qxt, k, kx, vt)


def _group_partner(x, lane, d, width):
    fwd = pltpu.roll(x, LANES - d, 1)
    back = pltpu.roll(x, width - d, 1)
    return jnp.where(lane % width + d < width, fwd, back)


def _route(logits, b_router, lane):
    scores = jax.nn.sigmoid(logits)
    sel = scores + b_router
    e = EXPERTS_PER_GROUP
    rank = jnp.zeros(sel.shape, jnp.int32)
    for d in range(1, e):
        other = _group_partner(sel, lane, d, e)
        other_first = lane % e + d >= e
        beats = (other > sel) | ((other == sel) & other_first)
        rank = rank + beats.astype(jnp.int32)
    top2 = rank < 2
    kept = jnp.where(top2, sel, 0.0)
    gscore = kept
    for d in range(1, e):
        gscore = gscore + _group_partner(kept, lane, d, e)
    grank = jnp.zeros_like(rank)
    for gstep in range(1, N_GROUPS):
        d = gstep * e
        other = _group_partner(gscore, lane, d, N_EXPERTS)
        other_first = lane % N_EXPERTS + d >= N_EXPERTS
        beats = (other > gscore) | ((other == gscore) & other_first)
        grank = grank + beats.astype(jnp.int32)
    chosen = top2 & (grank == 0) & (lane < N_EXPERTS)
    w = jnp.where(chosen, scores, 0.0)
    return w / jnp.sum(w, axis=1, keepdims=True)


def _post_kernel(x_ref, fo_ref, mo_ref, gate_ref, pa_ref, pb_ref, wo_ref, g1_ref, b1_ref, wr_ref, br_ref,
                 x1_ref, gates_ref, *, alpha):
    d = x_ref.shape[1]
    merged = gate_ref[:, 0:d].astype(F32) * _dot(fo_ref[...], pa_ref[...])
    merged = merged + gate_ref[:, d:2 * d].astype(F32) * _dot(mo_ref[...], pb_ref[...])
    mix = _dot(merged.astype(BF16), wo_ref[...])
    x1 = _layer_norm(alpha * x_ref[...] + mix, g1_ref[...], b1_ref[...])
    x1_ref[...] = x1
    hi, mid, lo = _split3(x1)
    whi, wmid, wlo = wr_ref[0], wr_ref[1], wr_ref[2]
    logits = _dot(lo, whi) + _dot(mid, wmid) + _dot(hi, wlo) + _dot(mid, whi) + _dot(hi, wmid) + _dot(hi, whi)
    lane = lax.broadcasted_iota(jnp.int32, logits.shape, 1)
    gates_ref[...] = _route(logits, br_ref[...], lane)


def _post_attention(x, fo, mo, gate, pa, pb, wo, g1, b1, wr3, br, alpha):
    n, d = x.shape
    tm = ROW_TILE
    row = lambda width: pl.BlockSpec((tm, width), lambda i: (i, 0))
    full = lambda a: pl.BlockSpec(a.shape, lambda i: (0,) * a.ndim)
    return pl.pallas_call(
        functools.partial(_post_kernel, alpha=alpha),
        grid=(n // tm,),
        in_specs=[row(d), row(512), row(512), row(2 * d), full(pa), full(pb), full(wo), full(g1), full(b1), full(wr3), full(br)],
        out_specs=[row(d), row(LANES)],
        out_shape=[jax.ShapeDtypeStruct((n, d), F32), jax.ShapeDtypeStruct((n, LANES), F32)],
        compiler_params=_params("parallel"),
        name="post_attention",
    )(x, fo, mo, gate, pa, pb, wo, g1, b1, wr3, br)


def _moe_kernel(x_ref, gates_ref, wg_ref, wu_ref, wd_ref, g2_ref, b2_ref, o_ref, xb_ref, acc_ref, *, alpha):
    e = pl.program_id(1)

    @pl.when(e == 0)
    def _():
        xb_ref[...] = x_ref[...].astype(BF16)
        acc_ref[...] = jnp.zeros_like(acc_ref)

    lane = lax.broadcasted_iota(jnp.int32, (1, LANES), 1)
    gate = jnp.sum(jnp.where(lane == e, gates_ref[...], 0.0), axis=1, keepdims=True)
    xb = xb_ref[...]
    hid = jax.nn.silu(_dot(xb, wg_ref[0])) * _dot(xb, wu_ref[0])
    acc_ref[...] += _dot((hid * gate).astype(BF16), wd_ref[0])

    @pl.when(e == pl.num_programs(1) - 1)
    def _():
        o_ref[...] = _layer_norm(alpha * x_ref[...] + acc_ref[...], g2_ref[...], b2_ref[...])


def _moe(x, gates, wg, wu, wd, g2, b2, alpha):
    n, d = x.shape
    n_exp, _, f = wg.shape
    tm = MOE_ROW_TILE
    return pl.pallas_call(
        functools.partial(_moe_kernel, alpha=alpha),
        grid=(n // tm, n_exp),
        in_specs=[
            pl.BlockSpec((tm, d), lambda i, e: (i, 0)),
            pl.BlockSpec((tm, LANES), lambda i, e: (i, 0)),
            pl.BlockSpec((1, d, f), lambda i, e: (e, 0, 0)),
            pl.BlockSpec((1, d, f), lambda i, e: (e, 0, 0)),
            pl.BlockSpec((1, f, d), lambda i, e: (e, 0, 0)),
            pl.BlockSpec((1, d), lambda i, e: (0, 0)),
            pl.BlockSpec((1, d), lambda i, e: (0, 0)),
        ],
        out_specs=pl.BlockSpec((tm, d), lambda i, e: (i, 0)),
        out_shape=jax.ShapeDtypeStruct((n, d), F32),
        scratch_shapes=[pltpu.VMEM((tm, d), BF16), pltpu.VMEM((tm, d), F32)],
        compiler_params=_params("parallel", "arbitrary"),
        name="moe_ln2",
    )(x, gates, wg, wu, wd, g2, b2)


def _rope_tables(pos):
    half = MLA_ROPE_DIM // 2
    inv_freq = ROPE_THETA ** (-jnp.arange(half, dtype=F32) / half)
    ang = pos.astype(F32)[:, None] * inv_freq[None, :]
    cos, sin = jnp.cos(ang), jnp.sin(ang)
    cos32 = jnp.concatenate([cos, cos], axis=1)
    sin32 = jnp.concatenate([-sin, sin], axis=1)
    return jnp.tile(cos32, (1, LANES // MLA_ROPE_DIM)), jnp.tile(sin32, (1, LANES // MLA_ROPE_DIM))


def _swap_halves(w):
    half = MLA_ROPE_DIM // 2
    return jnp.concatenate([w[..., half:], w[..., :half]], axis=-1)


def _prepare_w_in(w_in):
    depth, d, _ = w_in.shape
    o = [0]
    for sz in (FOX_WIDTH, FOX_WIDTH, FOX_WIDTH, FOX_HEADS, 256, 128, MLA_ROPE_DIM, 2 * d):
        o.append(o[-1] + sz)
    qkv = w_in[..., o[0]:o[3]]
    wf = w_in[..., o[3]:o[4]]
    wcq = w_in[..., o[4]:o[5]]
    wckv = w_in[..., o[5]:o[6]]
    wkr = w_in[..., o[6]:o[7]]
    wg = w_in[..., o[7]:o[8]]
    reps = LANES // MLA_ROPE_DIM
    wf_pad = jnp.concatenate([wf, jnp.zeros((depth, d, LANES - FOX_HEADS), w_in.dtype)], axis=-1)
    return jnp.concatenate(
        [qkv, wcq, wckv, wf_pad, jnp.tile(wkr, (1, 1, reps)), jnp.tile(_swap_halves(wkr), (1, 1, reps)), wg], axis=-1
    ).astype(BF16)


def _prepare_w_q_up(w):
    depth, r, _ = w.shape
    w = w.reshape(depth, r, MLA_HEADS, MLA_QK_DIM)
    nope = w[..., :MLA_NOPE_DIM].reshape(depth, r, MLA_HEADS * MLA_NOPE_DIM)
    rope = w[..., MLA_NOPE_DIM:]
    rope_sw = _swap_halves(rope).reshape(depth, r, MLA_HEADS * MLA_ROPE_DIM)
    rope = rope.reshape(depth, r, MLA_HEADS * MLA_ROPE_DIM)
    return jnp.concatenate([nope, rope, rope_sw], axis=-1).astype(BF16)


def _prepare_w_kv_up(w):
    depth, r, _ = w.shape
    w = w.reshape(depth, r, MLA_HEADS, MLA_NOPE_DIM + MLA_V_DIM)
    kn = w[..., :MLA_NOPE_DIM].reshape(depth, r, MLA_HEADS * MLA_NOPE_DIM)
    vm = w[..., MLA_NOPE_DIM:].reshape(depth, r, MLA_HEADS * MLA_V_DIM)
    return jnp.concatenate([kn, vm], axis=-1).astype(BF16)


def _pad_lanes(x, width=LANES):
    return jnp.pad(x, [(0, 0)] * (x.ndim - 1) + [(0, width - x.shape[-1])])


def kernel(x_prompt, x_sample, cache_fox_k, cache_fox_v, cache_fox_logf, cache_mla_ckv, cache_mla_krope, meta_tokens, ln_in_g, ln_in_b, w_in, b_fgt, g_q_norm, w_q_up, g_kv_norm, w_kv_up, w_proj_fox, w_proj_mla, b_gate, w_out, ln1_g, ln1_b, ln2_g, ln2_b, w_router, b_router, w_exp_gate, w_exp_up, w_exp_down):
    bp, seq, d = x_prompt.shape
    bs, tn, _ = x_sample.shape
    depth = w_in.shape[0]
    past = cache_fox_k.shape[2]
    alpha = (2 * depth) ** 0.25
    t_real = N_META + seq
    tp = -(-t_real // ATTN_TILE) * ATTN_TILE
    n_p = bp * tp
    n_s = bs * tn
    n = n_p + n_s
    n_pad = -(-n // MOE_ROW_TILE) * MOE_ROW_TILE
    tk_s = -(-(past + tn) // SAMPLE_KEY_TILE) * SAMPLE_KEY_TILE
    tq_s = SAMPLE_QUERY_TILE

    meta = jnp.broadcast_to(meta_tokens[None].astype(F32), (bp, N_META, d))
    xp = jnp.concatenate([meta, x_prompt, jnp.zeros((bp, tp - t_real, d), F32)], axis=1)
    x = jnp.concatenate([xp.reshape(n_p, d), x_sample.reshape(n_s, d), jnp.zeros((n_pad - n, d), F32)], axis=0)
    pos = jnp.concatenate([
        jnp.tile(jnp.arange(tp, dtype=jnp.int32), bp),
        jnp.tile(past + jnp.arange(tn, dtype=jnp.int32), bs),
        jnp.zeros((n_pad - n,), jnp.int32),
    ])
    cos, sin = _rope_tables(pos)

    w_in_r = _prepare_w_in(w_in)
    w_q_r = _prepare_w_q_up(w_q_up)
    w_kv_r = _prepare_w_kv_up(w_kv_up)
    bf_r = _pad_lanes(b_fgt)[:, None, :]
    pa, pb, wo = w_proj_fox.astype(BF16), w_proj_mla.astype(BF16), w_out.astype(BF16)
    wg, wu, wd = w_exp_gate.astype(BF16), w_exp_up.astype(BF16), w_exp_down.astype(BF16)
    wr = _pad_lanes(w_router)
    wr_hi = wr.astype(BF16)
    wr_mid = (wr - wr_hi.astype(F32)).astype(BF16)
    wr_lo = (wr - wr_hi.astype(F32) - wr_mid.astype(F32)).astype(BF16)
    wr3 = jnp.stack([wr_hi, wr_mid, wr_lo])
    br = _pad_lanes(b_router[None, :])

    zpad = tk_s - past - tn
    kn_c, vmt_c = _kv_up(cache_mla_ckv.reshape(depth, bs * past, -1), w_kv_r)
    kn_c = kn_c.reshape(depth, bs, past, 512)
    vmt_c = jnp.transpose(vmt_c.reshape(depth, 512, bs, past), (0, 2, 1, 3))
    kf_c = cache_fox_k.reshape(depth, bs, past, FOX_WIDTH).astype(BF16)
    vft_c = jnp.transpose(cache_fox_v.reshape(depth, bs, past, FOX_WIDTH), (0, 1, 3, 2)).astype(BF16)
    kr_c = jnp.tile(cache_mla_krope, (1, 1, 1, LANES // MLA_ROPE_DIM)).astype(BF16)
    logf_c = _pad_lanes(cache_fox_logf)

    def with_past(cached, new):
        new = new.reshape(bs, tn, new.shape[-1])
        return jnp.concatenate([cached, new, jnp.zeros((bs, zpad, new.shape[-1]), new.dtype)], axis=1)

    def with_past_t(cached_t, new_t):
        new_t = jnp.transpose(new_t.reshape(512, bs, tn), (1, 0, 2))
        return jnp.concatenate([cached_t, new_t, jnp.zeros((bs, 512, zpad), new_t.dtype)], axis=2)

    def sample_queries_t(a_t):
        a_t = jnp.transpose(a_t[:, n_p:n].reshape(a_t.shape[0], bs, tn), (1, 0, 2))
        return jnp.pad(a_t, ((0, 0), (0, 0), (0, tq_s - tn)))

    prompt = dict(n_batch=bp, t_q=tp, tq=ATTN_TILE, tk=ATTN_TILE, q_off=0, n_meta=N_META)
    sample = dict(n_batch=bs, t_q=tq_s, tq=tq_s, tk=SAMPLE_KEY_TILE, q_off=past, n_meta=0)

    x = _input_ln(x, ln_in_g[None, :], ln_in_b[None, :])
    rows = []
    for l in range(depth):
        pre = _pre_attention(
            x, w_in_r[l], bf_r[l], g_q_norm[l][None, :], w_q_r[l], g_kv_norm[l][None, :], w_kv_r[l],
            b_gate[l][None, :], cos, sin,
        )
        grp_p = lambda a: a[:n_p].reshape(bp, tp, a.shape[-1])

        qx_p, kx_p = _bias_operands(_cumsum_time(grp_p(pre["logf"]), ATTN_TILE))
        fox_p = _attention(pre["qft"], qx_p, grp_p(pre["kf"]), kx_p, pre["vft"], fox=True, **prompt)
        mla_p = _attention(pre["qnt"], pre["qrt"], grp_p(pre["kn"]), grp_p(pre["kr"]), pre["vmt"], fox=False, **prompt)

        qx_s, kx_s = _bias_operands(_cumsum_time(with_past(logf_c[l], pre["logf"][n_p:n]), SAMPLE_KEY_TILE))
        qx_s = jnp.pad(qx_s[:, :, :, past:past + tn], ((0, 0), (0, 0), (0, 0), (0, tq_s - tn)))
        fox_s = _attention(
            sample_queries_t(pre["qft"]), qx_s, with_past(kf_c[l], pre["kf"][n_p:n]), kx_s,
            with_past_t(vft_c[l], pre["vft"][:, n_p:n]), fox=True, **sample,
        )
        mla_s = _attention(
            sample_queries_t(pre["qnt"]), sample_queries_t(pre["qrt"]), with_past(kn_c[l], pre["kn"][n_p:n]),
            with_past(kr_c[l], pre["kr"][n_p:n]), with_past_t(vmt_c[l], pre["vmt"][:, n_p:n]), fox=False, **sample,
        )

        tail = jnp.zeros((n_pad - n, 512), BF16)
        fo = jnp.concatenate([fox_p.reshape(n_p, 512), fox_s[:, :tn].reshape(n_s, 512), tail], axis=0)
        mo = jnp.concatenate([mla_p.reshape(n_p, 512), mla_s[:, :tn].reshape(n_s, 512), tail], axis=0)
        x1, gates = _post_attention(
            x, fo, mo, pre["gate"], pa[l], pb[l], wo[l], ln1_g[l][None, :], ln1_b[l][None, :], wr3, br, alpha
        )
        x = _moe(x1, gates, wg[l], wu[l], wd[l], ln2_g[l][None, :], ln2_b[l][None, :], alpha)
        rows.append(pre)

    def stack_p(name, width):
        return jnp.stack([r[name][:n_p].reshape(bp, tp, -1)[:, :t_real, :width] for r in rows])

    def stack_s(name, width):
        return jnp.stack([r[name][n_p:n].reshape(bs, tn, -1)[:, :, :width] for r in rows])

    heads = (FOX_HEADS, FOX_HEAD_DIM)
    y_prompt = x[:n_p].reshape(bp, tp, d)[:, N_META:t_real]
    y_sample = x[n_p:n].reshape(bs, tn, d)
    return (
        y_prompt,
        y_sample,
        stack_p("kf32", 512).reshape(depth, bp, t_real, *heads),
        stack_p("vf32", 512).reshape(depth, bp, t_real, *heads),
        stack_p("logf", FOX_HEADS),
        stack_p("ckv", 128),
        stack_p("kr32", MLA_ROPE_DIM),
        stack_s("kf32", 512).reshape(depth, bs, tn, *heads),
        stack_s("vf32", 512).reshape(depth, bs, tn, *heads),
        stack_s("logf", FOX_HEADS),
        stack_s("ckv", 128),
        stack_s("kr32", MLA_ROPE_DIM),
    )
```

```python
import functools

import jax
import jax.numpy as jnp
import numpy as np
from jax import lax
from jax.experimental import pallas as pl
from jax.experimental.pallas import tpu as pltpu

F32 = jnp.float32
BF16 = jnp.bfloat16

CHUNK = 64
N_META = 16
FOX_HEADS = 8
FOX_HEAD_DIM = 64
FOX_WIDTH = FOX_HEADS * FOX_HEAD_DIM
MLA_HEADS = 8
MLA_NOPE_DIM = 64
MLA_ROPE_DIM = 32
MLA_QK_DIM = MLA_NOPE_DIM + MLA_ROPE_DIM
MLA_V_DIM = 64
N_EXPERTS = 16
N_GROUPS = 4
EXPERTS_PER_GROUP = N_EXPERTS // N_GROUPS
ROPE_THETA = 10000.0
LN_EPS = 1e-5
RMS_EPS = 1e-6
LOG2E = 1.4426950408889634

LANES = 128
ROW_TILE = 256
MOE_ROW_TILE = 512
ATTN_TILE = 256
ATTN_SLOTS = 4
QK_GROUP = 2
SAMPLE_QUERY_TILE = 128
SAMPLE_KEY_TILE = 128
AUG_GROUP = 8
ONES_ROWS = 16
MASK_VALUE = -1e30
VMEM_LIMIT = 56 * 1024 * 1024

C_QKV = 0
C_CQ = 3 * FOX_WIDTH
C_CKV = C_CQ + 256
C_LOGF = C_CKV + 128
C_KR = C_LOGF + LANES
C_KRS = C_KR + LANES
C_GATE = C_KRS + LANES


def _params(*sem):
    return pltpu.CompilerParams(dimension_semantics=sem, vmem_limit_bytes=VMEM_LIMIT)


def _layer_norm(x, g, b):
    mu = jnp.mean(x, axis=-1, keepdims=True)
    xc = x - mu
    var = jnp.mean(xc * xc, axis=-1, keepdims=True)
    return xc * lax.rsqrt(var + LN_EPS) * g + b


def _rms_norm(x, g):
    return x * lax.rsqrt(jnp.mean(x * x, axis=-1, keepdims=True) + RMS_EPS) * g


def _dot(a, b):
    return jnp.dot(a, b, preferred_element_type=F32)


def _split3(x):
    hi = x.astype(BF16)
    r1 = x - hi.astype(F32)
    mid = r1.astype(BF16)
    lo = (r1 - mid.astype(F32)).astype(BF16)
    return hi, mid, lo


def _ln_kernel(x_ref, g_ref, b_ref, o_ref):
    o_ref[...] = _layer_norm(x_ref[...], g_ref[...], b_ref[...])


def _input_ln(x, g, b):
    n, d = x.shape
    return pl.pallas_call(
        _ln_kernel,
        grid=(n // ROW_TILE,),
        in_specs=[
            pl.BlockSpec((ROW_TILE, d), lambda i: (i, 0)),
            pl.BlockSpec((1, d), lambda i: (0, 0)),
            pl.BlockSpec((1, d), lambda i: (0, 0)),
        ],
        out_specs=pl.BlockSpec((ROW_TILE, d), lambda i: (i, 0)),
        out_shape=jax.ShapeDtypeStruct((n, d), F32),
        compiler_params=_params("parallel"),
        name="input_ln",
    )(x, g, b)


N_ROW_STATES = 5


def _pre_kernel(x_ref, w_ref, bf_ref, gq_ref, wq_ref, gkv_ref, wkv_ref, bg_ref, cos_ref, sin_ref, *rest,
                fox_scale, mla_scale, n_prompt_tiles):
    outs = rest[len(rest) - 10 - 2 * N_ROW_STATES:]
    qft_ref, kf_ref, vft_ref, logf_ref, qnt_ref, qrt_ref, kr_ref, kn_ref, vmt_ref, gate_ref = outs[:10]
    prompt_refs = outs[10:10 + N_ROW_STATES]
    sample_refs = outs[10 + N_ROW_STATES:]
    xb = x_ref[...].astype(BF16)
    cos = cos_ref[...]
    sin = sin_ref[...]

    qkv = _dot(xb, w_ref[:, C_QKV:C_CQ])
    qft_ref[...] = (qkv[:, 0:FOX_WIDTH] * fox_scale).T.astype(BF16)
    k = qkv[:, FOX_WIDTH:2 * FOX_WIDTH]
    v = qkv[:, 2 * FOX_WIDTH:3 * FOX_WIDTH]
    kf_ref[...] = k.astype(BF16)
    vft_ref[...] = v.T.astype(BF16)

    z = _dot(xb, w_ref[:, C_LOGF:C_KR]) + bf_ref[...]
    logf = jnp.minimum(z, 0.0) - jnp.log1p(jnp.exp(-jnp.abs(z)))
    logf_ref[...] = logf

    cq = _rms_norm(_dot(xb, w_ref[:, C_CQ:C_CKV]), gq_ref[...]).astype(BF16)
    qm = _dot(cq, wq_ref[...])
    qnt_ref[...] = (qm[:, 0:512] * mla_scale).T.astype(BF16)
    cos2 = jnp.concatenate([cos, cos], axis=1)
    sin2 = jnp.concatenate([sin, sin], axis=1)
    qrt_ref[...] = ((qm[:, 512:768] * cos2 + qm[:, 768:1024] * sin2) * mla_scale).T.astype(BF16)

    ckv = _rms_norm(_dot(xb, w_ref[:, C_CKV:C_LOGF]), gkv_ref[...])
    kv = _dot(ckv.astype(BF16), wkv_ref[...])
    kn_ref[...] = kv[:, 0:512].astype(BF16)
    vmt_ref[...] = kv[:, 512:1024].T.astype(BF16)
    kr = _dot(xb, w_ref[:, C_KR:C_KRS]) * cos + _dot(xb, w_ref[:, C_KRS:C_GATE]) * sin
    kr_ref[...] = kr.astype(BF16)

    g = _dot(xb, w_ref[:, C_GATE:]) + bg_ref[...]
    gate_ref[...] = jax.nn.sigmoid(g).astype(BF16)

    states = (k, v, logf[:, :FOX_HEADS], ckv, kr[:, :MLA_ROPE_DIM])
    i = pl.program_id(0)

    @pl.when(i < n_prompt_tiles)
    def _():
        for ref, val in zip(prompt_refs, states):
            ref[...] = val

    @pl.when(i >= n_prompt_tiles)
    def _():
        for ref, val in zip(sample_refs, states):
            ref[...] = val.reshape(ref.shape)


def _pre_attention(x, w, bf, gq, wq, gkv, wkv, bg, cos, sin, *, layer, depth, prompt_dims, sample_dims, states):
    n, d = x.shape
    tm = ROW_TILE
    bp, tp = prompt_dims
    bs, tn = sample_dims
    tiles_per_stream = tp // tm
    n_prompt_tiles = bp * tiles_per_stream
    streams_per_tile = tm // tn
    row = lambda width: pl.BlockSpec((tm, width), lambda i: (i, 0))
    col = lambda height: pl.BlockSpec((height, tm), lambda i: (0, i))
    full = lambda a: pl.BlockSpec(a.shape, lambda i: (0,) * a.ndim)
    outs = [
        ("qft", 512, BF16, True), ("kf", 512, BF16, False), ("vft", 512, BF16, True), ("logf", LANES, F32, False),
        ("qnt", 512, BF16, True), ("qrt", 256, BF16, True), ("kr", LANES, BF16, False), ("kn", 512, BF16, False),
        ("vmt", 512, BF16, True), ("gate", 2 * d, BF16, False),
    ]
    state_widths = (FOX_WIDTH, FOX_WIDTH, FOX_HEADS, 128, MLA_ROPE_DIM)

    def prompt_block(i):
        ip = jnp.minimum(i, n_prompt_tiles - 1)
        return layer, ip // tiles_per_stream, ip % tiles_per_stream, 0

    def sample_block(i):
        return layer, jnp.maximum(i - n_prompt_tiles, 0), 0, 0

    out_specs = [col(wd) if tr else row(wd) for _, wd, _, tr in outs]
    out_specs += [pl.BlockSpec((None, None, tm, wd), prompt_block) for wd in state_widths]
    out_specs += [pl.BlockSpec((None, streams_per_tile, tn, wd), sample_block) for wd in state_widths]
    out_shape = [jax.ShapeDtypeStruct((wd, n) if tr else (n, wd), dt) for _, wd, dt, tr in outs]
    out_shape += [jax.ShapeDtypeStruct((depth, bp, tp, wd), F32) for wd in state_widths]
    out_shape += [jax.ShapeDtypeStruct((depth, bs, tn, wd), F32) for wd in state_widths]
    args = [x, w, bf, gq, wq, gkv, wkv, bg, cos, sin]
    in_specs = [row(d), full(w), full(bf), full(gq), full(wq), full(gkv), full(wkv), full(bg), row(LANES), row(LANES)]
    aliases = {}
    if states is not None:
        aliases = {len(args) + j: len(outs) + j for j in range(len(states))}
        args += list(states)
        in_specs += [pl.BlockSpec(memory_space=pl.ANY)] * len(states)
    res = pl.pallas_call(
        functools.partial(
            _pre_kernel, fox_scale=FOX_HEAD_DIM ** -0.5 * LOG2E, mla_scale=MLA_QK_DIM ** -0.5 * LOG2E,
            n_prompt_tiles=n_prompt_tiles,
        ),
        grid=(n // tm,),
        in_specs=in_specs,
        out_specs=out_specs,
        out_shape=out_shape,
        input_output_aliases=aliases,
        compiler_params=_params("arbitrary"),
        name="pre_attention",
    )(*args)
    return {name: r for (name, _, _, _), r in zip(outs, res)}, tuple(res[len(outs):])


def _kvup_kernel(c_ref, w_ref, kn_ref, vmt_ref):
    kv = _dot(c_ref[0].astype(BF16), w_ref[0])
    kn_ref[0] = kv[:, 0:512].astype(BF16)
    vmt_ref[0] = kv[:, 512:1024].T.astype(BF16)


def _kv_up(ckv, wkv):
    depth, n, r = ckv.shape
    tm = ROW_TILE
    return pl.pallas_call(
        _kvup_kernel,
        grid=(depth, n // tm),
        in_specs=[pl.BlockSpec((1, tm, r), lambda l, i: (l, i, 0)), pl.BlockSpec((1, r, 1024), lambda l, i: (l, 0, 0))],
        out_specs=[pl.BlockSpec((1, tm, 512), lambda l, i: (l, i, 0)), pl.BlockSpec((1, 512, tm), lambda l, i: (l, 0, i))],
        out_shape=[jax.ShapeDtypeStruct((depth, n, 512), BF16), jax.ShapeDtypeStruct((depth, 512, n), BF16)],
        compiler_params=_params("parallel", "parallel"),
        name="kv_up_cached",
    )(ckv, wkv)


def _bias_layout():
    sel_q = np.zeros((3 * LANES, 4 * LANES), np.float32)
    sel_k = np.zeros((3 * LANES, 4 * LANES), np.float32)
    one_q = np.zeros((1, 4 * LANES), np.float32)
    one_k = np.zeros((1, 4 * LANES), np.float32)
    for h in range(FOX_HEADS):
        base = (h // 2) * LANES + (h % 2) * AUG_GROUP
        for piece in range(3):
            sel_q[piece * LANES + h, base + piece] = 1.0
            sel_k[piece * LANES + h, base + 3 + piece] = -1.0
            one_q[0, base + 3 + piece] = 1.0
            one_k[0, base + piece] = 1.0
    return (jnp.asarray(sel_q, BF16), jnp.asarray(one_q, F32), jnp.asarray(sel_k, BF16), jnp.asarray(one_k, F32))


def _cumsum_kernel(x_ref, sq_ref, oq_ref, sk_ref, ok_ref, qxt_ref, kx_ref, carry_ref, *, tt):
    @pl.when(pl.program_id(1) == 0)
    def _():
        carry_ref[...] = jnp.zeros_like(carry_ref)

    r = lax.broadcasted_iota(jnp.int32, (tt, tt), 0)
    c = lax.broadcasted_iota(jnp.int32, (tt, tt), 1)
    tri = jnp.where(c <= r, 1.0, 0.0).astype(BF16)
    hi, mid, lo = _split3(x_ref[...])
    cum = _dot(tri, lo) + _dot(tri, mid) + _dot(tri, hi) + carry_ref[...]
    carry_ref[...] = cum[tt - 1:tt, :]
    pieces = jnp.concatenate(_split3(cum * LOG2E), axis=1)
    kx = (_dot(pieces, sk_ref[...]) + ok_ref[...]).astype(BF16)
    qx = _dot(pieces, sq_ref[...]) + oq_ref[...]
    for p in range(FOX_HEADS // 2):
        kx_ref[p] = kx[:, p * LANES:(p + 1) * LANES]
        qxt_ref[p] = qx[:, p * LANES:(p + 1) * LANES].T.astype(BF16)


def _forget_bias_operands(logf, n_seq, t, tt):
    w = logf.shape[-1]
    tiles = t // tt
    consts = _bias_layout()
    pairs = FOX_HEADS // 2
    return pl.pallas_call(
        functools.partial(_cumsum_kernel, tt=tt),
        grid=(n_seq, tiles),
        in_specs=[pl.BlockSpec((tt, w), lambda b, i: (b * tiles + i, 0))]
        + [pl.BlockSpec(c.shape, lambda b, i: (0, 0)) for c in consts],
        out_specs=[
            pl.BlockSpec((None, pairs, LANES, tt), lambda b, i: (b, 0, 0, i)),
            pl.BlockSpec((None, pairs, tt, LANES), lambda b, i: (b, 0, i, 0)),
        ],
        out_shape=[
            jax.ShapeDtypeStruct((n_seq, pairs, LANES, t), BF16),
            jax.ShapeDtypeStruct((n_seq, pairs, t, LANES), BF16),
        ],
        scratch_shapes=[pltpu.VMEM((1, w), F32)],
        compiler_params=_params("parallel", "arbitrary"),
        name="cumsum_time",
    )(logf, *consts)


def _chunk_end(pos, n_meta):
    body = n_meta + ((pos - n_meta) // CHUNK) * CHUNK + CHUNK - 1
    if n_meta == 0:
        return body
    return jnp.where(pos < n_meta, n_meta - 1, body)


def _attn_kernel(qt_ref, qxt_ref, k_ref, kx_ref, vt_ref, *rest, fox, tq, tk, q_off, n_meta, n_ktiles):
    o_ref, *slots = rest[len(rest) - ATTN_SLOTS - 1:]
    p = pl.program_id(1)
    i = pl.program_id(2)
    q_lo = q_off + i * tq
    q_hi = q_lo + tq - 1
    if fox:
        vis_lo, vis_hi = q_lo, q_hi
    else:
        vis_lo, vis_hi = _chunk_end(q_lo, n_meta), _chunk_end(q_hi, n_meta)
    n_tot = jnp.minimum(vis_hi // tk + 1, n_ktiles)
    n_full = jnp.minimum((vis_lo + 1) // tk, n_tot - 1)

    q_pos = q_lo + lax.broadcasted_iota(jnp.int32, (1, tq), 1)
    row = lax.broadcasted_iota(jnp.int32, (LANES, 1), 0)
    qt2 = qt_ref[...]
    qxt2 = qxt_ref[...]
    cols = []
    for hh in range(2):
        x_group = row // AUG_GROUP == hh if fox else row // MLA_ROPE_DIM == 2 * (p % 2) + hh
        cols.append(jnp.concatenate([
            jnp.where(row // FOX_HEAD_DIM == hh, qt2, jnp.zeros_like(qt2)),
            jnp.where(x_group, qxt2, jnp.zeros_like(qxt2)),
        ], axis=0))
    qcbt = jnp.concatenate(cols, axis=1)
    last_visible = q_pos if fox else _chunk_end(q_pos, n_meta)
    last_visible = jnp.concatenate([last_visible, last_visible], axis=1)
    key_iota = lax.broadcasted_iota(jnp.int32, (tk, 2 * tq), 0)
    ones = jnp.ones((ONES_ROWS, tk), BF16)
    n_slots = len(slots)

    def key_rows(j):
        return pl.ds(pl.multiple_of(jnp.minimum(j, n_ktiles - 1) * tk, tk), tk)

    def produce(slot_refs, j, masked):
        kcs = []
        for r in range(len(slot_refs)):
            ks = key_rows(j + r)
            kcs.append(jnp.concatenate([k_ref[ks, :], kx_ref[ks, :]], axis=1))
        st_all = _dot(jnp.concatenate(kcs, axis=0), qcbt)
        cmax = []
        for r, slot_ref in enumerate(slot_refs):
            st = st_all[r * tk:(r + 1) * tk]
            if masked:
                st = jnp.where(key_iota <= last_visible - (j + r) * tk, st, MASK_VALUE)
            slot_ref[...] = st
            cmax.append(jnp.max(st, axis=0, keepdims=True))
        return cmax

    def accumulate(j, pt, alpha, accs):
        ks = key_rows(j)
        new = []
        for hh in range(2):
            va = jnp.concatenate([vt_ref[FOX_HEAD_DIM * hh:FOX_HEAD_DIM * (hh + 1), ks], ones], axis=0)
            sl = slice(hh * tq, (hh + 1) * tq)
            new.append(alpha[:, sl] * accs[hh] + _dot(va, pt[:, sl]))
        return tuple(new)

    def step(u, carry, mode):
        m, cmax, accs = carry
        cmax = list(cmax)
        for r0 in range(0, n_slots, QK_GROUP):
            group = range(r0, r0 + QK_GROUP)
            pts, alphas = [], []
            for r in group:
                m_new = jnp.maximum(m, cmax[r])
                pts.append(jnp.exp2(slots[r][...] - m_new).astype(BF16))
                alphas.append(jnp.exp2(m - m_new))
                m = m_new
            if mode != "drain":
                cmax[r0:r0 + QK_GROUP] = produce([slots[r] for r in group], n_slots * (u + 1) + r0, mode == "masked")
            for r, pt, alpha in zip(group, pts, alphas):
                accs = accumulate(n_slots * u + r, pt, alpha, accs)
        return m, tuple(cmax), accs

    def two_steps(v, carry):
        return step(2 * v + 1, step(2 * v, carry, "plain"), "plain")

    m0 = jnp.full((1, 2 * tq), MASK_VALUE, F32)
    a0 = jnp.zeros((FOX_HEAD_DIM + ONES_ROWS, tq), F32)
    carry = (m0, tuple(produce(list(slots), 0, True)), (a0, a0))
    n_steps = (n_tot + n_slots - 1) // n_slots
    n_plain = jnp.maximum(n_full - n_slots, 0) // n_slots
    carry = lax.fori_loop(0, n_plain // 2, two_steps, carry)
    carry = lax.fori_loop(n_plain - n_plain % 2, n_plain, functools.partial(step, mode="plain"), carry)
    carry = lax.fori_loop(n_plain, n_steps - 1, functools.partial(step, mode="masked"), carry)
    _, _, accs = step(n_steps - 1, carry, "drain")
    out_t = jnp.concatenate(
        [acc[0:FOX_HEAD_DIM] / acc[FOX_HEAD_DIM:FOX_HEAD_DIM + 1] for acc in accs], axis=0
    )
    o_ref[...] = out_t.T[:o_ref.shape[0]].astype(o_ref.dtype)


def _attention(qt, qxt, k, kx, vt, *, fox, n_batch, t_q, t_k, tq, tk, q_off, n_meta, out_rows, out_row0=0, out=None):
    nq = t_q // tq

    def time_major(a, rows_of, t, tile):
        tiles = t // tile
        if a.ndim == 2:
            return pl.BlockSpec((LANES, tile), lambda bb, p, i: (rows_of(p), bb * tiles + i % tiles))
        return pl.BlockSpec((None, LANES, tile), lambda bb, p, i: (bb, rows_of(p), i % tiles))

    def row_major(a, col_of):
        if a.ndim == 2:
            return pl.BlockSpec((t_k, LANES), lambda bb, p, i: (bb, col_of(p)))
        return pl.BlockSpec((None, t_k, LANES), lambda bb, p, i: (bb, 0, col_of(p)))

    qspec = time_major(qt, lambda p: p, t_q, tq)
    kspec = row_major(k, lambda p: p)
    if fox:
        qxspec = pl.BlockSpec((None, None, LANES, tq), lambda bb, p, i: (bb, p, 0, i))
        kxspec = pl.BlockSpec((None, None, t_k, LANES), lambda bb, p, i: (bb, p, 0, 0))
    else:
        qxspec = time_major(qxt, lambda p: p // 2, t_q, tq)
        kxspec = row_major(kx, lambda p: 0)
    vspec = time_major(vt, lambda p: p, t_k, t_k)
    args = [qt, qxt, k, kx, vt]
    in_specs = [qspec, qxspec, kspec, kxspec, vspec]
    if isinstance(out, int):
        n_rows, aliases = out, {}
    else:
        n_rows, aliases = out.shape[0], {len(args): 0}
        args.append(out)
        in_specs.append(pl.BlockSpec(memory_space=pl.ANY))
    block0 = out_row0 // out_rows
    return pl.pallas_call(
        functools.partial(_attn_kernel, fox=fox, tq=tq, tk=tk, q_off=q_off, n_meta=n_meta, n_ktiles=t_k // tk),
        grid=(n_batch, 4, nq),
        in_specs=in_specs,
        out_specs=pl.BlockSpec((out_rows, LANES), lambda bb, p, i: (block0 + bb * nq + i, p)),
        out_shape=jax.ShapeDtypeStruct((n_rows, 512), BF16),
        input_output_aliases=aliases,
        scratch_shapes=[pltpu.VMEM((tk, 2 * tq), F32)] * ATTN_SLOTS,
        compiler_params=_params("parallel", "parallel", "arbitrary"),
        name="fox_attention" if fox else "mla_attention",
    )(*args)


def _group_partner(x, lane, d, width):
    fwd = pltpu.roll(x, LANES - d, 1)
    back = pltpu.roll(x, width - d, 1)
    return jnp.where(lane % width + d < width, fwd, back)


def _route(logits, b_router, lane):
    scores = jax.nn.sigmoid(logits)
    sel = scores + b_router
    e = EXPERTS_PER_GROUP
    rank = jnp.zeros(sel.shape, jnp.int32)
    for d in range(1, e):
        other = _group_partner(sel, lane, d, e)
        other_first = lane % e + d >= e
        beats = (other > sel) | ((other == sel) & other_first)
        rank = rank + beats.astype(jnp.int32)
    top2 = rank < 2
    kept = jnp.where(top2, sel, 0.0)
    gscore = kept
    for d in range(1, e):
        gscore = gscore + _group_partner(kept, lane, d, e)
    grank = jnp.zeros_like(rank)
    for gstep in range(1, N_GROUPS):
        d = gstep * e
        other = _group_partner(gscore, lane, d, N_EXPERTS)
        other_first = lane % N_EXPERTS + d >= N_EXPERTS
        beats = (other > gscore) | ((other == gscore) & other_first)
        grank = grank + beats.astype(jnp.int32)
    chosen = top2 & (grank == 0) & (lane < N_EXPERTS)
    w = jnp.where(chosen, scores, 0.0)
    return w / jnp.sum(w, axis=1, keepdims=True)


def _post_kernel(x_ref, fo_ref, mo_ref, gate_ref, pa_ref, pb_ref, wo_ref, g1_ref, b1_ref, wr_ref, br_ref,
                 x1_ref, gates_ref, *, alpha):
    d = x_ref.shape[1]
    merged = gate_ref[:, 0:d].astype(F32) * _dot(fo_ref[...], pa_ref[...])
    merged = merged + gate_ref[:, d:2 * d].astype(F32) * _dot(mo_ref[...], pb_ref[...])
    mix = _dot(merged.astype(BF16), wo_ref[...])
    x1 = _layer_norm(alpha * x_ref[...] + mix, g1_ref[...], b1_ref[...])
    x1_ref[...] = x1
    hi, mid, lo = _split3(x1)
    whi, wmid, wlo = wr_ref[0], wr_ref[1], wr_ref[2]
    logits = _dot(lo, whi) + _dot(mid, wmid) + _dot(hi, wlo) + _dot(mid, whi) + _dot(hi, wmid) + _dot(hi, whi)
    lane = lax.broadcasted_iota(jnp.int32, logits.shape, 1)
    gates_ref[...] = _route(logits, br_ref[...], lane)


def _post_attention(x, fo, mo, gate, pa, pb, wo, g1, b1, wr3, br, alpha):
    n, d = x.shape
    tm = ROW_TILE
    row = lambda width: pl.BlockSpec((tm, width), lambda i: (i, 0))
    full = lambda a: pl.BlockSpec(a.shape, lambda i: (0,) * a.ndim)
    return pl.pallas_call(
        functools.partial(_post_kernel, alpha=alpha),
        grid=(n // tm,),
        in_specs=[row(d), row(512), row(512), row(2 * d), full(pa), full(pb), full(wo), full(g1), full(b1), full(wr3), full(br)],
        out_specs=[row(d), row(LANES)],
        out_shape=[jax.ShapeDtypeStruct((n, d), F32), jax.ShapeDtypeStruct((n, LANES), F32)],
        compiler_params=_params("parallel"),
        name="post_attention",
    )(x, fo, mo, gate, pa, pb, wo, g1, b1, wr3, br)


def _moe_kernel(x_ref, gates_ref, wg_ref, wu_ref, wd_ref, g2_ref, b2_ref, o_ref, xb_ref, acc_ref, *, alpha):
    e = pl.program_id(1)

    @pl.when(e == 0)
    def _():
        xb_ref[...] = x_ref[...].astype(BF16)
        acc_ref[...] = jnp.zeros_like(acc_ref)

    lane = lax.broadcasted_iota(jnp.int32, (1, LANES), 1)
    gate = jnp.sum(jnp.where(lane == e, gates_ref[...], 0.0), axis=1, keepdims=True)
    xb = xb_ref[...]
    hid = jax.nn.silu(_dot(xb, wg_ref[0])) * _dot(xb, wu_ref[0])
    acc_ref[...] += _dot((hid * gate).astype(BF16), wd_ref[0])

    @pl.when(e == pl.num_programs(1) - 1)
    def _():
        o_ref[...] = _layer_norm(alpha * x_ref[...] + acc_ref[...], g2_ref[...], b2_ref[...])


def _moe(x, gates, wg, wu, wd, g2, b2, alpha):
    n, d = x.shape
    n_exp, _, f = wg.shape
    tm = MOE_ROW_TILE if n % MOE_ROW_TILE == 0 else ROW_TILE
    return pl.pallas_call(
        functools.partial(_moe_kernel, alpha=alpha),
        grid=(n // tm, n_exp),
        in_specs=[
            pl.BlockSpec((tm, d), lambda i, e: (i, 0)),
            pl.BlockSpec((tm, LANES), lambda i, e: (i, 0)),
            pl.BlockSpec((1, d, f), lambda i, e: (e, 0, 0)),
            pl.BlockSpec((1, d, f), lambda i, e: (e, 0, 0)),
            pl.BlockSpec((1, f, d), lambda i, e: (e, 0, 0)),
            pl.BlockSpec((1, d), lambda i, e: (0, 0)),
            pl.BlockSpec((1, d), lambda i, e: (0, 0)),
        ],
        out_specs=pl.BlockSpec((tm, d), lambda i, e: (i, 0)),
        out_shape=jax.ShapeDtypeStruct((n, d), F32),
        scratch_shapes=[pltpu.VMEM((tm, d), BF16), pltpu.VMEM((tm, d), F32)],
        compiler_params=_params("parallel", "arbitrary"),
        name="moe_ln2",
    )(x, gates, wg, wu, wd, g2, b2)


def _rope_tables(pos):
    half = MLA_ROPE_DIM // 2
    inv_freq = ROPE_THETA ** (-jnp.arange(half, dtype=F32) / half)
    ang = pos.astype(F32)[:, None] * inv_freq[None, :]
    cos, sin = jnp.cos(ang), jnp.sin(ang)
    cos32 = jnp.concatenate([cos, cos], axis=1)
    sin32 = jnp.concatenate([-sin, sin], axis=1)
    return jnp.tile(cos32, (1, LANES // MLA_ROPE_DIM)), jnp.tile(sin32, (1, LANES // MLA_ROPE_DIM))


def _swap_halves(w):
    half = MLA_ROPE_DIM // 2
    return jnp.concatenate([w[..., half:], w[..., :half]], axis=-1)


def _prepare_w_in(w_in):
    depth, d, _ = w_in.shape
    o = [0]
    for sz in (FOX_WIDTH, FOX_WIDTH, FOX_WIDTH, FOX_HEADS, 256, 128, MLA_ROPE_DIM, 2 * d):
        o.append(o[-1] + sz)
    qkv = w_in[..., o[0]:o[3]]
    wf = w_in[..., o[3]:o[4]]
    wcq = w_in[..., o[4]:o[5]]
    wckv = w_in[..., o[5]:o[6]]
    wkr = w_in[..., o[6]:o[7]]
    wg = w_in[..., o[7]:o[8]]
    reps = LANES // MLA_ROPE_DIM
    wf_pad = jnp.concatenate([wf, jnp.zeros((depth, d, LANES - FOX_HEADS), w_in.dtype)], axis=-1)
    return jnp.concatenate(
        [qkv, wcq, wckv, wf_pad, jnp.tile(wkr, (1, 1, reps)), jnp.tile(_swap_halves(wkr), (1, 1, reps)), wg], axis=-1
    ).astype(BF16)


def _prepare_w_q_up(w):
    depth, r, _ = w.shape
    w = w.reshape(depth, r, MLA_HEADS, MLA_QK_DIM)
    nope = w[..., :MLA_NOPE_DIM].reshape(depth, r, MLA_HEADS * MLA_NOPE_DIM)
    rope = w[..., MLA_NOPE_DIM:]
    rope_sw = _swap_halves(rope).reshape(depth, r, MLA_HEADS * MLA_ROPE_DIM)
    rope = rope.reshape(depth, r, MLA_HEADS * MLA_ROPE_DIM)
    return jnp.concatenate([nope, rope, rope_sw], axis=-1).astype(BF16)


def _prepare_w_kv_up(w):
    depth, r, _ = w.shape
    w = w.reshape(depth, r, MLA_HEADS, MLA_NOPE_DIM + MLA_V_DIM)
    kn = w[..., :MLA_NOPE_DIM].reshape(depth, r, MLA_HEADS * MLA_NOPE_DIM)
    vm = w[..., MLA_NOPE_DIM:].reshape(depth, r, MLA_HEADS * MLA_V_DIM)
    return jnp.concatenate([kn, vm], axis=-1).astype(BF16)


def _pad_lanes(x, width=LANES):
    return jnp.pad(x, [(0, 0)] * (x.ndim - 1) + [(0, width - x.shape[-1])])


def kernel(x_prompt, x_sample, cache_fox_k, cache_fox_v, cache_fox_logf, cache_mla_ckv, cache_mla_krope, meta_tokens, ln_in_g, ln_in_b, w_in, b_fgt, g_q_norm, w_q_up, g_kv_norm, w_kv_up, w_proj_fox, w_proj_mla, b_gate, w_out, ln1_g, ln1_b, ln2_g, ln2_b, w_router, b_router, w_exp_gate, w_exp_up, w_exp_down):
    bp, seq, d = x_prompt.shape
    bs, tn, _ = x_sample.shape
    depth = w_in.shape[0]
    past = cache_fox_k.shape[2]
    alpha = (2 * depth) ** 0.25
    t_real = N_META + seq
    tp = -(-t_real // ATTN_TILE) * ATTN_TILE
    n_p = bp * tp
    n_s = bs * tn
    n = n_p + n_s
    n_pad = n
    tk_s = -(-(past + tn) // SAMPLE_KEY_TILE) * SAMPLE_KEY_TILE
    tq_s = SAMPLE_QUERY_TILE
    assert n % ROW_TILE == 0 and ROW_TILE % tn == 0 and tn <= tq_s, (n, tn)

    meta = jnp.broadcast_to(meta_tokens[None].astype(F32), (bp, N_META, d))
    xp = jnp.concatenate([meta, x_prompt, jnp.zeros((bp, tp - t_real, d), F32)], axis=1)
    x = jnp.concatenate([xp.reshape(n_p, d), x_sample.reshape(n_s, d), jnp.zeros((n_pad - n, d), F32)], axis=0)
    pos = jnp.concatenate([
        jnp.tile(jnp.arange(tp, dtype=jnp.int32), bp),
        jnp.tile(past + jnp.arange(tn, dtype=jnp.int32), bs),
        jnp.zeros((n_pad - n,), jnp.int32),
    ])
    cos, sin = _rope_tables(pos)

    w_in_r = _prepare_w_in(w_in)
    w_q_r = _prepare_w_q_up(w_q_up)
    w_kv_r = _prepare_w_kv_up(w_kv_up)
    bf_r = _pad_lanes(b_fgt)[:, None, :]
    pa, pb, wo = w_proj_fox.astype(BF16), w_proj_mla.astype(BF16), w_out.astype(BF16)
    wg, wu, wd = w_exp_gate.astype(BF16), w_exp_up.astype(BF16), w_exp_down.astype(BF16)
    wr = _pad_lanes(w_router)
    wr_hi = wr.astype(BF16)
    wr_mid = (wr - wr_hi.astype(F32)).astype(BF16)
    wr_lo = (wr - wr_hi.astype(F32) - wr_mid.astype(F32)).astype(BF16)
    wr3 = jnp.stack([wr_hi, wr_mid, wr_lo])
    br = _pad_lanes(b_router[None, :])

    zpad = tk_s - past - tn
    kn_c, vmt_c = _kv_up(cache_mla_ckv.reshape(depth, bs * past, -1), w_kv_r)
    kn_c = kn_c.reshape(depth, bs, past, 512)
    vmt_c = jnp.transpose(vmt_c.reshape(depth, 512, bs, past), (0, 2, 1, 3))
    kf_c = cache_fox_k.reshape(depth, bs, past, FOX_WIDTH).astype(BF16)
    vft_c = jnp.transpose(cache_fox_v.reshape(depth, bs, past, FOX_WIDTH), (0, 1, 3, 2)).astype(BF16)
    kr_c = jnp.tile(cache_mla_krope, (1, 1, 1, LANES // MLA_ROPE_DIM)).astype(BF16)
    logf_c = _pad_lanes(cache_fox_logf)

    def with_past(cached, new):
        new = new.reshape(bs, tn, new.shape[-1])
        return jnp.concatenate([cached, new, jnp.zeros((bs, zpad, new.shape[-1]), new.dtype)], axis=1)

    def with_past_t(cached_t, new_t):
        new_t = jnp.transpose(new_t.reshape(512, bs, tn), (1, 0, 2))
        return jnp.concatenate([cached_t, new_t, jnp.zeros((bs, 512, zpad), new_t.dtype)], axis=2)

    def sample_queries_t(a_t):
        a_t = jnp.transpose(a_t[:, n_p:n].reshape(a_t.shape[0], bs, tn), (1, 0, 2))
        return jnp.pad(a_t, ((0, 0), (0, 0), (0, tq_s - tn)))

    prompt = dict(
        n_batch=bp, t_q=tp, t_k=tp, tq=ATTN_TILE, tk=ATTN_TILE, q_off=0, n_meta=N_META, out_rows=ATTN_TILE, out=n_pad
    )
    sample = dict(
        n_batch=bs, t_q=tq_s, t_k=tk_s, tq=tq_s, tk=SAMPLE_KEY_TILE, q_off=past, n_meta=0, out_rows=tn, out_row0=n_p
    )

    x = _input_ln(x, ln_in_g[None, :], ln_in_b[None, :])
    states = None
    for l in range(depth):
        pre, states = _pre_attention(
            x, w_in_r[l], bf_r[l], g_q_norm[l][None, :], w_q_r[l], g_kv_norm[l][None, :], w_kv_r[l],
            b_gate[l][None, :], cos, sin,
            layer=l, depth=depth, prompt_dims=(bp, tp), sample_dims=(bs, tn), states=states,
        )
        qx_p, kx_p = _forget_bias_operands(pre["logf"], bp, tp, ATTN_TILE)
        fo = _attention(pre["qft"], qx_p, pre["kf"], kx_p, pre["vft"], fox=True, **prompt)
        mo = _attention(pre["qnt"], pre["qrt"], pre["kn"], pre["kr"], pre["vmt"], fox=False, **prompt)

        logf_s = with_past(logf_c[l], pre["logf"][n_p:n]).reshape(bs * tk_s, LANES)
        qx_s, kx_s = _forget_bias_operands(logf_s, bs, tk_s, SAMPLE_KEY_TILE)
        qx_s = jnp.pad(qx_s[:, :, :, past:past + tn], ((0, 0), (0, 0), (0, 0), (0, tq_s - tn)))
        fo = _attention(
            sample_queries_t(pre["qft"]), qx_s, with_past(kf_c[l], pre["kf"][n_p:n]), kx_s,
            with_past_t(vft_c[l], pre["vft"][:, n_p:n]), fox=True, out=fo, **sample,
        )
        mo = _attention(
            sample_queries_t(pre["qnt"]), sample_queries_t(pre["qrt"]), with_past(kn_c[l], pre["kn"][n_p:n]),
            with_past(kr_c[l], pre["kr"][n_p:n]), with_past_t(vmt_c[l], pre["vmt"][:, n_p:n]), fox=False, out=mo,
            **sample,
        )
        x1, gates = _post_attention(
            x, fo, mo, pre["gate"], pa[l], pb[l], wo[l], ln1_g[l][None, :], ln1_b[l][None, :], wr3, br, alpha
        )
        x = _moe(x1, gates, wg[l], wu[l], wd[l], ln2_g[l][None, :], ln2_b[l][None, :], alpha)

    heads = (FOX_HEADS, FOX_HEAD_DIM)
    k_p, v_p, logf_p, ckv_p, kr_p, k_s, v_s, logf_s, ckv_s, kr_s = states
    y_prompt = x[:n_p].reshape(bp, tp, d)[:, N_META:t_real]
    y_sample = x[n_p:n].reshape(bs, tn, d)
    return (
        y_prompt,
        y_sample,
        k_p[:, :, :t_real].reshape(depth, bp, t_real, *heads),
        v_p[:, :, :t_real].reshape(depth, bp, t_real, *heads),
        logf_p[:, :, :t_real],
        ckv_p[:, :, :t_real],
        kr_p[:, :, :t_real],
        k_s.reshape(depth, bs, tn, *heads),
        v_s.reshape(depth, bs, tn, *heads),
        logf_s,
        ckv_s,
        kr_s,
    )
```

```python
import functools

import jax
import jax.numpy as jnp
import numpy as np
from jax import lax
from jax.experimental import pallas as pl
from jax.experimental.pallas import tpu as pltpu

F32 = jnp.float32
BF16 = jnp.bfloat16

CHUNK = 64
N_META = 16
FOX_HEADS = 8
FOX_HEAD_DIM = 64
FOX_WIDTH = FOX_HEADS * FOX_HEAD_DIM
MLA_HEADS = 8
MLA_NOPE_DIM = 64
MLA_ROPE_DIM = 32
MLA_QK_DIM = MLA_NOPE_DIM + MLA_ROPE_DIM
MLA_V_DIM = 64
N_EXPERTS = 16
N_GROUPS = 4
EXPERTS_PER_GROUP = N_EXPERTS // N_GROUPS
ROPE_THETA = 10000.0
LN_EPS = 1e-5
RMS_EPS = 1e-6
LOG2E = 1.4426950408889634

LANES = 128
ROW_TILE = 256
MOE_ROW_TILE = 512
ATTN_TILE = 256
ATTN_Q_TILE = 256
ATTN_SLOTS = 4
QK_GROUP = 2
SAMPLE_QUERY_TILE = 128
SAMPLE_KEY_TILE = 384
AUG_GROUP = 8
ONES_ROWS = 16
MASK_VALUE = -1e30
UNDERFLOW_MARGIN = 170.0
VMEM_LIMIT = 56 * 1024 * 1024

C_QKV = 0
C_CQ = 3 * FOX_WIDTH
C_CKV = C_CQ + 256
C_LOGF = C_CKV + 128
C_KR = C_LOGF + LANES
C_KRS = C_KR + LANES
C_GATE = C_KRS + LANES


def _params(*sem):
    return pltpu.CompilerParams(dimension_semantics=sem, vmem_limit_bytes=VMEM_LIMIT)


def _layer_norm(x, g, b):
    mu = jnp.mean(x, axis=-1, keepdims=True)
    xc = x - mu
    var = jnp.mean(xc * xc, axis=-1, keepdims=True)
    return xc * lax.rsqrt(var + LN_EPS) * g + b


def _rms_norm(x, g):
    return x * lax.rsqrt(jnp.mean(x * x, axis=-1, keepdims=True) + RMS_EPS) * g


def _dot(a, b):
    return jnp.dot(a, b, preferred_element_type=F32)


def _split3(x):
    hi = x.astype(BF16)
    r1 = x - hi.astype(F32)
    mid = r1.astype(BF16)
    lo = (r1 - mid.astype(F32)).astype(BF16)
    return hi, mid, lo


def _ln_kernel(x_ref, g_ref, b_ref, o_ref):
    o_ref[...] = _layer_norm(x_ref[...], g_ref[...], b_ref[...])


def _input_ln(x, g, b):
    n, d = x.shape
    return pl.pallas_call(
        _ln_kernel,
        grid=(n // ROW_TILE,),
        in_specs=[
            pl.BlockSpec((ROW_TILE, d), lambda i: (i, 0)),
            pl.BlockSpec((1, d), lambda i: (0, 0)),
            pl.BlockSpec((1, d), lambda i: (0, 0)),
        ],
        out_specs=pl.BlockSpec((ROW_TILE, d), lambda i: (i, 0)),
        out_shape=jax.ShapeDtypeStruct((n, d), F32),
        compiler_params=_params("parallel"),
        name="input_ln",
    )(x, g, b)


N_ROW_STATES = 5


def _pre_kernel(x_ref, w_ref, bf_ref, gq_ref, wq_ref, gkv_ref, wkv_ref, bg_ref, cos_ref, sin_ref, *rest,
                fox_scale, mla_scale, n_prompt_tiles):
    outs = rest[len(rest) - 10 - 2 * N_ROW_STATES:]
    qft_ref, kf_ref, vft_ref, logf_ref, qnt_ref, qrt_ref, kr_ref, kn_ref, vmt_ref, gate_ref = outs[:10]
    prompt_refs = outs[10:10 + N_ROW_STATES]
    sample_refs = outs[10 + N_ROW_STATES:]
    xb = x_ref[...].astype(BF16)
    cos = cos_ref[...]
    sin = sin_ref[...]

    qkv = _dot(xb, w_ref[:, C_QKV:C_CQ])
    qft_ref[...] = (qkv[:, 0:FOX_WIDTH] * fox_scale).T.astype(BF16)
    k = qkv[:, FOX_WIDTH:2 * FOX_WIDTH]
    v = qkv[:, 2 * FOX_WIDTH:3 * FOX_WIDTH]
    kf_ref[...] = k.astype(BF16)
    vft_ref[...] = v.T.astype(BF16)

    z = _dot(xb, w_ref[:, C_LOGF:C_KR]) + bf_ref[...]
    logf = jnp.minimum(z, 0.0) - jnp.log1p(jnp.exp(-jnp.abs(z)))
    logf_ref[...] = logf

    cq = _rms_norm(_dot(xb, w_ref[:, C_CQ:C_CKV]), gq_ref[...]).astype(BF16)
    qm = _dot(cq, wq_ref[...])
    qnt_ref[...] = (qm[:, 0:512] * mla_scale).T.astype(BF16)
    cos2 = jnp.concatenate([cos, cos], axis=1)
    sin2 = jnp.concatenate([sin, sin], axis=1)
    qrt_ref[...] = ((qm[:, 512:768] * cos2 + qm[:, 768:1024] * sin2) * mla_scale).T.astype(BF16)

    ckv = _rms_norm(_dot(xb, w_ref[:, C_CKV:C_LOGF]), gkv_ref[...])
    kv = _dot(ckv.astype(BF16), wkv_ref[...])
    kn_ref[...] = kv[:, 0:512].astype(BF16)
    vmt_ref[...] = kv[:, 512:1024].T.astype(BF16)
    kr = _dot(xb, w_ref[:, C_KR:C_KRS]) * cos + _dot(xb, w_ref[:, C_KRS:C_GATE]) * sin
    kr_ref[...] = kr.astype(BF16)

    g = _dot(xb, w_ref[:, C_GATE:]) + bg_ref[...]
    gate_ref[...] = jax.nn.sigmoid(g).astype(BF16)

    states = (k, v, logf[:, :FOX_HEADS], ckv, kr[:, :MLA_ROPE_DIM])
    i = pl.program_id(0)

    @pl.when(i < n_prompt_tiles)
    def _():
        for ref, val in zip(prompt_refs, states):
            ref[...] = val

    @pl.when(i >= n_prompt_tiles)
    def _():
        for ref, val in zip(sample_refs, states):
            ref[...] = val.reshape(ref.shape)


def _pre_attention(x, w, bf, gq, wq, gkv, wkv, bg, cos, sin, *, layer, depth, prompt_dims, sample_dims, states):
    n, d = x.shape
    tm = ROW_TILE
    bp, tp = prompt_dims
    bs, tn = sample_dims
    tiles_per_stream = tp // tm
    n_prompt_tiles = bp * tiles_per_stream
    streams_per_tile = tm // tn
    row = lambda width: pl.BlockSpec((tm, width), lambda i: (i, 0))
    col = lambda height: pl.BlockSpec((height, tm), lambda i: (0, i))
    full = lambda a: pl.BlockSpec(a.shape, lambda i: (0,) * a.ndim)
    outs = [
        ("qft", 512, BF16, True), ("kf", 512, BF16, False), ("vft", 512, BF16, True), ("logf", LANES, F32, False),
        ("qnt", 512, BF16, True), ("qrt", 256, BF16, True), ("kr", LANES, BF16, False), ("kn", 512, BF16, False),
        ("vmt", 512, BF16, True), ("gate", 2 * d, BF16, False),
    ]
    state_widths = (FOX_WIDTH, FOX_WIDTH, FOX_HEADS, 128, MLA_ROPE_DIM)

    def prompt_block(i):
        ip = jnp.minimum(i, n_prompt_tiles - 1)
        return layer, ip // tiles_per_stream, ip % tiles_per_stream, 0

    def sample_block(i):
        return layer, jnp.maximum(i - n_prompt_tiles, 0), 0, 0

    out_specs = [col(wd) if tr else row(wd) for _, wd, _, tr in outs]
    out_specs += [pl.BlockSpec((None, None, tm, wd), prompt_block) for wd in state_widths]
    out_specs += [pl.BlockSpec((None, streams_per_tile, tn, wd), sample_block) for wd in state_widths]
    out_shape = [jax.ShapeDtypeStruct((wd, n) if tr else (n, wd), dt) for _, wd, dt, tr in outs]
    out_shape += [jax.ShapeDtypeStruct((depth, bp, tp, wd), F32) for wd in state_widths]
    out_shape += [jax.ShapeDtypeStruct((depth, bs, tn, wd), F32) for wd in state_widths]
    args = [x, w, bf, gq, wq, gkv, wkv, bg, cos, sin]
    in_specs = [row(d), full(w), full(bf), full(gq), full(wq), full(gkv), full(wkv), full(bg), row(LANES), row(LANES)]
    aliases = {}
    if states is not None:
        aliases = {len(args) + j: len(outs) + j for j in range(len(states))}
        args += list(states)
        in_specs += [pl.BlockSpec(memory_space=pl.ANY)] * len(states)
    res = pl.pallas_call(
        functools.partial(
            _pre_kernel, fox_scale=FOX_HEAD_DIM ** -0.5 * LOG2E, mla_scale=MLA_QK_DIM ** -0.5 * LOG2E,
            n_prompt_tiles=n_prompt_tiles,
        ),
        grid=(n // tm,),
        in_specs=in_specs,
        out_specs=out_specs,
        out_shape=out_shape,
        input_output_aliases=aliases,
        compiler_params=_params("arbitrary"),
        name="pre_attention",
    )(*args)
    return {name: r for (name, _, _, _), r in zip(outs, res)}, tuple(res[len(outs):])


def _kvup_kernel(c_ref, w_ref, kn_ref, vmt_ref):
    kv = _dot(c_ref[0].astype(BF16), w_ref[0])
    kn_ref[0] = kv[:, 0:512].astype(BF16)
    vmt_ref[0] = kv[:, 512:1024].T.astype(BF16)


def _kv_up(ckv, wkv):
    depth, n, r = ckv.shape
    tm = ROW_TILE
    return pl.pallas_call(
        _kvup_kernel,
        grid=(depth, n // tm),
        in_specs=[pl.BlockSpec((1, tm, r), lambda l, i: (l, i, 0)), pl.BlockSpec((1, r, 1024), lambda l, i: (l, 0, 0))],
        out_specs=[pl.BlockSpec((1, tm, 512), lambda l, i: (l, i, 0)), pl.BlockSpec((1, 512, tm), lambda l, i: (l, 0, i))],
        out_shape=[jax.ShapeDtypeStruct((depth, n, 512), BF16), jax.ShapeDtypeStruct((depth, 512, n), BF16)],
        compiler_params=_params("parallel", "parallel"),
        name="kv_up_cached",
    )(ckv, wkv)


def _bias_layout():
    sel_q = np.zeros((3 * LANES, 4 * LANES), np.float32)
    sel_k = np.zeros((3 * LANES, 4 * LANES), np.float32)
    one_q = np.zeros((1, 4 * LANES), np.float32)
    one_k = np.zeros((1, 4 * LANES), np.float32)
    for h in range(FOX_HEADS):
        base = (h // 2) * LANES + (h % 2) * AUG_GROUP
        for piece in range(3):
            sel_q[piece * LANES + h, base + piece] = 1.0
            sel_k[piece * LANES + h, base + 3 + piece] = -1.0
            one_q[0, base + 3 + piece] = 1.0
            one_k[0, base + piece] = 1.0
    return (jnp.asarray(sel_q, BF16), jnp.asarray(one_q, F32), jnp.asarray(sel_k, BF16), jnp.asarray(one_k, F32))


def _cumsum_kernel(x_ref, sq_ref, oq_ref, sk_ref, ok_ref, qxt_ref, kx_ref, carry_ref, *, tt):
    @pl.when(pl.program_id(1) == 0)
    def _():
        carry_ref[...] = jnp.zeros_like(carry_ref)

    r = lax.broadcasted_iota(jnp.int32, (tt, tt), 0)
    c = lax.broadcasted_iota(jnp.int32, (tt, tt), 1)
    tri = jnp.where(c <= r, 1.0, 0.0).astype(BF16)
    hi, mid, lo = _split3(x_ref[...])
    cum = _dot(tri, lo) + _dot(tri, mid) + _dot(tri, hi) + carry_ref[...]
    carry_ref[...] = cum[tt - 1:tt, :]
    pieces = jnp.concatenate(_split3(cum * LOG2E), axis=1)
    kx = (_dot(pieces, sk_ref[...]) + ok_ref[...]).astype(BF16)
    qx = _dot(pieces, sq_ref[...]) + oq_ref[...]
    for p in range(FOX_HEADS // 2):
        kx_ref[p] = kx[:, p * LANES:(p + 1) * LANES]
        qxt_ref[p] = qx[:, p * LANES:(p + 1) * LANES].T.astype(BF16)


def _forget_bias_operands(logf, n_seq, t, tt):
    w = logf.shape[-1]
    tiles = t // tt
    consts = _bias_layout()
    pairs = FOX_HEADS // 2
    return pl.pallas_call(
        functools.partial(_cumsum_kernel, tt=tt),
        grid=(n_seq, tiles),
        in_specs=[pl.BlockSpec((tt, w), lambda b, i: (b * tiles + i, 0))]
        + [pl.BlockSpec(c.shape, lambda b, i: (0, 0)) for c in consts],
        out_specs=[
            pl.BlockSpec((None, pairs, LANES, tt), lambda b, i: (b, 0, 0, i)),
            pl.BlockSpec((None, pairs, tt, LANES), lambda b, i: (b, 0, i, 0)),
        ],
        out_shape=[
            jax.ShapeDtypeStruct((n_seq, pairs, LANES, t), BF16),
            jax.ShapeDtypeStruct((n_seq, pairs, t, LANES), BF16),
        ],
        scratch_shapes=[pltpu.VMEM((1, w), F32)],
        compiler_params=_params("parallel", "arbitrary"),
        name="cumsum_time",
    )(logf, *consts)


def _chunk_end(pos, n_meta):
    body = n_meta + ((pos - n_meta) // CHUNK) * CHUNK + CHUNK - 1
    if n_meta == 0:
        return body
    return jnp.where(pos < n_meta, n_meta - 1, body)


def _attn_kernel(first_ref, qt_ref, qxt_ref, k_ref, kx_ref, vt_ref, *rest, fox, tq, tk, q_off, n_meta, n_ktiles):
    o_ref, *slots = rest[len(rest) - ATTN_SLOTS - 1:]
    p = pl.program_id(1)
    i = pl.program_id(2)
    q_lo = q_off + i * tq
    q_hi = q_lo + tq - 1
    if fox:
        vis_lo, vis_hi = q_lo, q_hi
    else:
        vis_lo, vis_hi = _chunk_end(q_lo, n_meta), _chunk_end(q_hi, n_meta)
    n_tot = jnp.minimum(vis_hi // tk + 1, n_ktiles)
    n_full = jnp.minimum((vis_lo + 1) // tk, n_tot - 1)
    first = jnp.clip(first_ref[pl.program_id(0), p, i], 0, n_full)
    n_tot = n_tot - first
    n_full = n_full - first

    q_pos = q_lo + lax.broadcasted_iota(jnp.int32, (1, tq), 1)
    row = lax.broadcasted_iota(jnp.int32, (LANES, 1), 0)
    qt2 = qt_ref[...]
    qxt2 = qxt_ref[...]
    cols = []
    for hh in range(2):
        x_group = row // AUG_GROUP == hh if fox else row // MLA_ROPE_DIM == 2 * (p % 2) + hh
        cols.append(jnp.concatenate([
            jnp.where(row // FOX_HEAD_DIM == hh, qt2, jnp.zeros_like(qt2)),
            jnp.where(x_group, qxt2, jnp.zeros_like(qxt2)),
        ], axis=0))
    qcbt = jnp.concatenate(cols, axis=1)
    last_visible = q_pos if fox else _chunk_end(q_pos, n_meta)
    last_visible = jnp.concatenate([last_visible, last_visible], axis=1)
    key_iota = lax.broadcasted_iota(jnp.int32, (tk, 2 * tq), 0)
    ones = jnp.ones((ONES_ROWS, tk), BF16)
    n_slots = len(slots)

    def key_rows(j):
        return pl.ds(pl.multiple_of(jnp.minimum(first + j, n_ktiles - 1) * tk, tk), tk)

    def produce(slot_refs, j, masked):
        kcs = []
        for r in range(len(slot_refs)):
            ks = key_rows(j + r)
            kcs.append(jnp.concatenate([k_ref[ks, :], kx_ref[ks, :]], axis=1))
        st_all = _dot(jnp.concatenate(kcs, axis=0), qcbt)
        cmax = []
        for r, slot_ref in enumerate(slot_refs):
            st = st_all[r * tk:(r + 1) * tk]
            if masked:
                st = jnp.where(key_iota <= last_visible - (first + j + r) * tk, st, MASK_VALUE)
            slot_ref[...] = st
            cmax.append(jnp.max(st, axis=0, keepdims=True))
        return cmax

    def accumulate(j, pt, alpha, accs):
        ks = key_rows(j)
        new = []
        for hh in range(2):
            va = jnp.concatenate([vt_ref[FOX_HEAD_DIM * hh:FOX_HEAD_DIM * (hh + 1), ks], ones], axis=0)
            sl = slice(hh * tq, (hh + 1) * tq)
            new.append(alpha[:, sl] * accs[hh] + _dot(va, pt[:, sl]))
        return tuple(new)

    def step(u, carry, mode):
        m, cmax, accs = carry
        cmax = list(cmax)
        for r0 in range(0, n_slots, QK_GROUP):
            group = range(r0, r0 + QK_GROUP)
            pts, alphas = [], []
            for r in group:
                m_new = jnp.maximum(m, cmax[r])
                pts.append(jnp.exp2(slots[r][...] - m_new).astype(BF16))
                alphas.append(jnp.exp2(m - m_new))
                m = m_new
            if mode != "drain":
                cmax[r0:r0 + QK_GROUP] = produce([slots[r] for r in group], n_slots * (u + 1) + r0, mode == "masked")
            for r, pt, alpha in zip(group, pts, alphas):
                accs = accumulate(n_slots * u + r, pt, alpha, accs)
        return m, tuple(cmax), accs

    def two_steps(v, carry):
        return step(2 * v + 1, step(2 * v, carry, "plain"), "plain")

    m0 = jnp.full((1, 2 * tq), MASK_VALUE, F32)
    a0 = jnp.zeros((FOX_HEAD_DIM + ONES_ROWS, tq), F32)
    carry = (m0, tuple(produce(list(slots), 0, True)), (a0, a0))
    n_steps = (n_tot + n_slots - 1) // n_slots
    n_plain = jnp.maximum(n_full - n_slots, 0) // n_slots
    carry = lax.fori_loop(0, n_plain // 2, two_steps, carry)
    carry = lax.fori_loop(n_plain - n_plain % 2, n_plain, functools.partial(step, mode="plain"), carry)
    carry = lax.fori_loop(n_plain, n_steps - 1, functools.partial(step, mode="masked"), carry)
    _, _, accs = step(n_steps - 1, carry, "drain")
    out_t = jnp.concatenate(
        [acc[0:FOX_HEAD_DIM] / acc[FOX_HEAD_DIM:FOX_HEAD_DIM + 1] for acc in accs], axis=0
    )
    o_ref[...] = out_t.T[:o_ref.shape[0]].astype(o_ref.dtype)


def _attention(qt, qxt, k, kx, vt, *, fox, n_batch, t_q, t_k, tq, tk, q_off, n_meta, out_rows, out_row0=0, out=None,
               first=None):
    nq = t_q // tq

    def time_major(a, rows_of, t, tile):
        tiles = t // tile
        if a.ndim == 2:
            return pl.BlockSpec((LANES, tile), lambda bb, p, i, _: (rows_of(p), bb * tiles + i % tiles))
        return pl.BlockSpec((None, LANES, tile), lambda bb, p, i, _: (bb, rows_of(p), i % tiles))

    def row_major(a, col_of):
        if a.ndim == 2:
            return pl.BlockSpec((t_k, LANES), lambda bb, p, i, _: (bb, col_of(p)))
        return pl.BlockSpec((None, t_k, LANES), lambda bb, p, i, _: (bb, 0, col_of(p)))

    qspec = time_major(qt, lambda p: p, t_q, tq)
    kspec = row_major(k, lambda p: p)
    if fox:
        qxspec = pl.BlockSpec((None, None, LANES, tq), lambda bb, p, i, _: (bb, p, 0, i))
        kxspec = pl.BlockSpec((None, None, t_k, LANES), lambda bb, p, i, _: (bb, p, 0, 0))
    else:
        qxspec = time_major(qxt, lambda p: p // 2, t_q, tq)
        kxspec = row_major(kx, lambda p: 0)
    vspec = time_major(vt, lambda p: p, t_k, t_k)
    if first is None:
        first = jnp.zeros((n_batch, 4, nq), jnp.int32)
    args = [first, qt, qxt, k, kx, vt]
    in_specs = [qspec, qxspec, kspec, kxspec, vspec]
    if isinstance(out, int):
        n_rows, aliases = out, {}
    else:
        n_rows, aliases = out.shape[0], {len(args): 0}
        args.append(out)
        in_specs.append(pl.BlockSpec(memory_space=pl.ANY))
    block0 = out_row0 // out_rows
    return pl.pallas_call(
        functools.partial(_attn_kernel, fox=fox, tq=tq, tk=tk, q_off=q_off, n_meta=n_meta, n_ktiles=t_k // tk),
        grid_spec=pltpu.PrefetchScalarGridSpec(
            num_scalar_prefetch=1,
            grid=(n_batch, 4, nq),
            in_specs=in_specs,
            out_specs=pl.BlockSpec((out_rows, LANES), lambda bb, p, i, _: (block0 + bb * nq + i, p)),
            scratch_shapes=[pltpu.VMEM((tk, 2 * tq), F32)] * ATTN_SLOTS,
        ),
        out_shape=jax.ShapeDtypeStruct((n_rows, 512), BF16),
        input_output_aliases=aliases,
        compiler_params=_params("parallel", "parallel", "arbitrary"),
        name="fox_attention" if fox else "mla_attention",
    )(*args)


def _first_live_key_tile(qft, kf, logf, n_seq, t, tile):
    n = n_seq * t
    tiles = t // tile
    h, dh = FOX_HEADS, FOX_HEAD_DIM
    q = qft[:, :n].astype(F32).reshape(h, dh, n_seq, tiles, tile)
    qmax = jnp.transpose(jnp.max(jnp.sqrt(jnp.sum(q * q, axis=1)), axis=-1), (1, 2, 0))
    k = kf[:n].astype(F32).reshape(n_seq, tiles, tile, h, dh)
    kmax = jnp.max(jnp.sqrt(jnp.sum(k * k, axis=-1)), axis=2)
    cum2 = jnp.cumsum(logf[:n, :h].reshape(n_seq, t, h), axis=1) * LOG2E
    cum_q = cum2[:, 0::tile]
    cum_k = cum2[:, tile - 1::tile]
    bound = qmax[:, :, None] * lax.cummax(kmax, axis=1)[:, None] + cum_q[:, :, None] - cum_k[:, None]
    own = -qmax * kmax
    dead = bound < own[:, :, None] - UNDERFLOW_MARGIN
    dead = jnp.all(dead.reshape(n_seq, tiles, tiles, h // 2, 2), axis=-1)
    first = jnp.sum(jnp.cumprod(dead.astype(jnp.int32), axis=2), axis=2)
    return jnp.transpose(first, (0, 2, 1))


def _group_partner(x, lane, d, width):
    fwd = pltpu.roll(x, LANES - d, 1)
    back = pltpu.roll(x, width - d, 1)
    return jnp.where(lane % width + d < width, fwd, back)


def _route(logits, b_router, lane):
    scores = jax.nn.sigmoid(logits)
    sel = scores + b_router
    e = EXPERTS_PER_GROUP
    rank = jnp.zeros(sel.shape, jnp.int32)
    for d in range(1, e):
        other = _group_partner(sel, lane, d, e)
        other_first = lane % e + d >= e
        beats = (other > sel) | ((other == sel) & other_first)
        rank = rank + beats.astype(jnp.int32)
    top2 = rank < 2
    kept = jnp.where(top2, sel, 0.0)
    gscore = kept
    for d in range(1, e):
        gscore = gscore + _group_partner(kept, lane, d, e)
    grank = jnp.zeros_like(rank)
    for gstep in range(1, N_GROUPS):
        d = gstep * e
        other = _group_partner(gscore, lane, d, N_EXPERTS)
        other_first = lane % N_EXPERTS + d >= N_EXPERTS
        beats = (other > gscore) | ((other == gscore) & other_first)
        grank = grank + beats.astype(jnp.int32)
    chosen = top2 & (grank == 0) & (lane < N_EXPERTS)
    w = jnp.where(chosen, scores, 0.0)
    return w / jnp.sum(w, axis=1, keepdims=True)


def _post_kernel(x_ref, fo_ref, mo_ref, gate_ref, pa_ref, pb_ref, wo_ref, g1_ref, b1_ref, wr_ref, br_ref,
                 x1_ref, gates_ref, *, alpha):
    d = x_ref.shape[1]
    merged = gate_ref[:, 0:d].astype(F32) * _dot(fo_ref[...], pa_ref[...])
    merged = merged + gate_ref[:, d:2 * d].astype(F32) * _dot(mo_ref[...], pb_ref[...])
    mix = _dot(merged.astype(BF16), wo_ref[...])
    x1 = _layer_norm(alpha * x_ref[...] + mix, g1_ref[...], b1_ref[...])
    x1_ref[...] = x1
    hi, mid, lo = _split3(x1)
    a = _dot(hi, wr_ref[:, 0:2 * LANES])
    b = _dot(mid, wr_ref[:, 0:2 * LANES])
    logits = _dot(lo, wr_ref[:, 0:LANES]) + _dot(hi, wr_ref[:, 2 * LANES:3 * LANES])
    logits = logits + b[:, LANES:] + a[:, LANES:] + b[:, :LANES] + a[:, :LANES]
    lane = lax.broadcasted_iota(jnp.int32, logits.shape, 1)
    gates_ref[...] = _route(logits, br_ref[...], lane)


def _post_attention(x, fo, mo, gate, pa, pb, wo, g1, b1, wr3, br, alpha):
    n, d = x.shape
    tm = ROW_TILE
    row = lambda width: pl.BlockSpec((tm, width), lambda i: (i, 0))
    full = lambda a: pl.BlockSpec(a.shape, lambda i: (0,) * a.ndim)
    return pl.pallas_call(
        functools.partial(_post_kernel, alpha=alpha),
        grid=(n // tm,),
        in_specs=[row(d), row(512), row(512), row(2 * d), full(pa), full(pb), full(wo), full(g1), full(b1), full(wr3), full(br)],
        out_specs=[row(d), row(LANES)],
        out_shape=[jax.ShapeDtypeStruct((n, d), F32), jax.ShapeDtypeStruct((n, LANES), F32)],
        compiler_params=_params("parallel"),
        name="post_attention",
    )(x, fo, mo, gate, pa, pb, wo, g1, b1, wr3, br)


def _moe_kernel(x_ref, gates_ref, wg_ref, wu_ref, wd_ref, g2_ref, b2_ref, o_ref, xb_ref, acc_ref, *, alpha):
    e = pl.program_id(1)

    @pl.when(e == 0)
    def _():
        xb_ref[...] = x_ref[...].astype(BF16)
        acc_ref[...] = jnp.zeros_like(acc_ref)

    lane = lax.broadcasted_iota(jnp.int32, (1, LANES), 1)
    gate = jnp.sum(jnp.where(lane == e, gates_ref[...], 0.0), axis=1, keepdims=True)
    xb = xb_ref[...]
    hid = jax.nn.silu(_dot(xb, wg_ref[0])) * _dot(xb, wu_ref[0])
    acc_ref[...] += _dot((hid * gate).astype(BF16), wd_ref[0])

    @pl.when(e == pl.num_programs(1) - 1)
    def _():
        o_ref[...] = _layer_norm(alpha * x_ref[...] + acc_ref[...], g2_ref[...], b2_ref[...])


def _moe(x, gates, wg, wu, wd, g2, b2, alpha):
    n, d = x.shape
    n_exp, _, f = wg.shape
    tm = MOE_ROW_TILE if n % MOE_ROW_TILE == 0 else ROW_TILE
    return pl.pallas_call(
        functools.partial(_moe_kernel, alpha=alpha),
        grid=(n // tm, n_exp),
        in_specs=[
            pl.BlockSpec((tm, d), lambda i, e: (i, 0)),
            pl.BlockSpec((tm, LANES), lambda i, e: (i, 0)),
            pl.BlockSpec((1, d, f), lambda i, e: (e, 0, 0)),
            pl.BlockSpec((1, d, f), lambda i, e: (e, 0, 0)),
            pl.BlockSpec((1, f, d), lambda i, e: (e, 0, 0)),
            pl.BlockSpec((1, d), lambda i, e: (0, 0)),
            pl.BlockSpec((1, d), lambda i, e: (0, 0)),
        ],
        out_specs=pl.BlockSpec((tm, d), lambda i, e: (i, 0)),
        out_shape=jax.ShapeDtypeStruct((n, d), F32),
        scratch_shapes=[pltpu.VMEM((tm, d), BF16), pltpu.VMEM((tm, d), F32)],
        compiler_params=_params("parallel", "arbitrary"),
        name="moe_ln2",
    )(x, gates, wg, wu, wd, g2, b2)


def _rope_tables(pos):
    half = MLA_ROPE_DIM // 2
    inv_freq = ROPE_THETA ** (-jnp.arange(half, dtype=F32) / half)
    ang = pos.astype(F32)[:, None] * inv_freq[None, :]
    cos, sin = jnp.cos(ang), jnp.sin(ang)
    cos32 = jnp.concatenate([cos, cos], axis=1)
    sin32 = jnp.concatenate([-sin, sin], axis=1)
    return jnp.tile(cos32, (1, LANES // MLA_ROPE_DIM)), jnp.tile(sin32, (1, LANES // MLA_ROPE_DIM))


def _swap_halves(w):
    half = MLA_ROPE_DIM // 2
    return jnp.concatenate([w[..., half:], w[..., :half]], axis=-1)


def _prepare_w_in(w_in):
    depth, d, _ = w_in.shape
    o = [0]
    for sz in (FOX_WIDTH, FOX_WIDTH, FOX_WIDTH, FOX_HEADS, 256, 128, MLA_ROPE_DIM, 2 * d):
        o.append(o[-1] + sz)
    qkv = w_in[..., o[0]:o[3]]
    wf = w_in[..., o[3]:o[4]]
    wcq = w_in[..., o[4]:o[5]]
    wckv = w_in[..., o[5]:o[6]]
    wkr = w_in[..., o[6]:o[7]]
    wg = w_in[..., o[7]:o[8]]
    reps = LANES // MLA_ROPE_DIM
    wf_pad = jnp.concatenate([wf, jnp.zeros((depth, d, LANES - FOX_HEADS), w_in.dtype)], axis=-1)
    return jnp.concatenate(
        [qkv, wcq, wckv, wf_pad, jnp.tile(wkr, (1, 1, reps)), jnp.tile(_swap_halves(wkr), (1, 1, reps)), wg], axis=-1
    ).astype(BF16)


def _prepare_w_q_up(w):
    depth, r, _ = w.shape
    w = w.reshape(depth, r, MLA_HEADS, MLA_QK_DIM)
    nope = w[..., :MLA_NOPE_DIM].reshape(depth, r, MLA_HEADS * MLA_NOPE_DIM)
    rope = w[..., MLA_NOPE_DIM:]
    rope_sw = _swap_halves(rope).reshape(depth, r, MLA_HEADS * MLA_ROPE_DIM)
    rope = rope.reshape(depth, r, MLA_HEADS * MLA_ROPE_DIM)
    return jnp.concatenate([nope, rope, rope_sw], axis=-1).astype(BF16)


def _prepare_w_kv_up(w):
    depth, r, _ = w.shape
    w = w.reshape(depth, r, MLA_HEADS, MLA_NOPE_DIM + MLA_V_DIM)
    kn = w[..., :MLA_NOPE_DIM].reshape(depth, r, MLA_HEADS * MLA_NOPE_DIM)
    vm = w[..., MLA_NOPE_DIM:].reshape(depth, r, MLA_HEADS * MLA_V_DIM)
    return jnp.concatenate([kn, vm], axis=-1).astype(BF16)


def _pad_lanes(x, width=LANES):
    return jnp.pad(x, [(0, 0)] * (x.ndim - 1) + [(0, width - x.shape[-1])])


def kernel(x_prompt, x_sample, cache_fox_k, cache_fox_v, cache_fox_logf, cache_mla_ckv, cache_mla_krope, meta_tokens, ln_in_g, ln_in_b, w_in, b_fgt, g_q_norm, w_q_up, g_kv_norm, w_kv_up, w_proj_fox, w_proj_mla, b_gate, w_out, ln1_g, ln1_b, ln2_g, ln2_b, w_router, b_router, w_exp_gate, w_exp_up, w_exp_down):
    bp, seq, d = x_prompt.shape
    bs, tn, _ = x_sample.shape
    depth = w_in.shape[0]
    past = cache_fox_k.shape[2]
    alpha = (2 * depth) ** 0.25
    t_real = N_META + seq
    tp = -(-t_real // ATTN_Q_TILE) * ATTN_Q_TILE
    n_p = bp * tp
    n_s = bs * tn
    n = n_p + n_s
    n_pad = n
    tk_s = -(-(past + tn) // SAMPLE_KEY_TILE) * SAMPLE_KEY_TILE
    tq_s = SAMPLE_QUERY_TILE
    assert n % ROW_TILE == 0 and ROW_TILE % tn == 0 and tn <= tq_s, (n, tn)
    assert ATTN_Q_TILE == ATTN_TILE

    meta = jnp.broadcast_to(meta_tokens[None].astype(F32), (bp, N_META, d))
    xp = jnp.concatenate([meta, x_prompt, jnp.zeros((bp, tp - t_real, d), F32)], axis=1)
    x = jnp.concatenate([xp.reshape(n_p, d), x_sample.reshape(n_s, d), jnp.zeros((n_pad - n, d), F32)], axis=0)
    pos = jnp.concatenate([
        jnp.tile(jnp.arange(tp, dtype=jnp.int32), bp),
        jnp.tile(past + jnp.arange(tn, dtype=jnp.int32), bs),
        jnp.zeros((n_pad - n,), jnp.int32),
    ])
    cos, sin = _rope_tables(pos)

    w_in_r = _prepare_w_in(w_in)
    w_q_r = _prepare_w_q_up(w_q_up)
    w_kv_r = _prepare_w_kv_up(w_kv_up)
    bf_r = _pad_lanes(b_fgt)[:, None, :]
    pa, pb, wo = w_proj_fox.astype(BF16), w_proj_mla.astype(BF16), w_out.astype(BF16)
    wg, wu, wd = w_exp_gate.astype(BF16), w_exp_up.astype(BF16), w_exp_down.astype(BF16)
    wr = _pad_lanes(w_router)
    keep_bf16_bits = lambda a: lax.bitcast_convert_type(
        lax.bitcast_convert_type(a, jnp.uint32) & jnp.uint32(0xFFFF0000), F32
    )
    wr_hi = keep_bf16_bits(wr)
    wr_mid = keep_bf16_bits(wr - wr_hi)
    wr_lo = wr - wr_hi - wr_mid
    wr3 = jnp.concatenate([wr_hi, wr_mid, wr_lo], axis=1).astype(BF16)
    br = _pad_lanes(b_router[None, :])

    zpad = tk_s - past - tn
    kn_c, vmt_c = _kv_up(cache_mla_ckv.reshape(depth, bs * past, -1), w_kv_r)
    kn_c = kn_c.reshape(depth, bs, past, 512)
    vmt_c = jnp.transpose(vmt_c.reshape(depth, 512, bs, past), (0, 2, 1, 3))
    kf_c = cache_fox_k.reshape(depth, bs, past, FOX_WIDTH).astype(BF16)
    vft_c = jnp.transpose(cache_fox_v.reshape(depth, bs, past, FOX_WIDTH), (0, 1, 3, 2)).astype(BF16)
    kr_c = jnp.tile(cache_mla_krope, (1, 1, 1, LANES // MLA_ROPE_DIM)).astype(BF16)
    logf_c = _pad_lanes(cache_fox_logf)

    def with_past(cached, new):
        new = new.reshape(bs, tn, new.shape[-1])
        return jnp.concatenate([cached, new, jnp.zeros((bs, zpad, new.shape[-1]), new.dtype)], axis=1)

    def with_past_t(cached_t, new_t):
        new_t = jnp.transpose(new_t.reshape(512, bs, tn), (1, 0, 2))
        return jnp.concatenate([cached_t, new_t, jnp.zeros((bs, 512, zpad), new_t.dtype)], axis=2)

    def sample_queries_t(a_t):
        a_t = jnp.transpose(a_t[:, n_p:n].reshape(a_t.shape[0], bs, tn), (1, 0, 2))
        return jnp.pad(a_t, ((0, 0), (0, 0), (0, tq_s - tn)))

    prompt = dict(
        n_batch=bp, t_q=tp, t_k=tp, tq=ATTN_Q_TILE, tk=ATTN_TILE, q_off=0, n_meta=N_META, out_rows=ATTN_Q_TILE,
        out=n_pad,
    )
    sample = dict(
        n_batch=bs, t_q=tq_s, t_k=tk_s, tq=tq_s, tk=SAMPLE_KEY_TILE, q_off=past, n_meta=0, out_rows=tn, out_row0=n_p
    )

    x = _input_ln(x, ln_in_g[None, :], ln_in_b[None, :])
    states = None
    for l in range(depth):
        pre, states = _pre_attention(
            x, w_in_r[l], bf_r[l], g_q_norm[l][None, :], w_q_r[l], g_kv_norm[l][None, :], w_kv_r[l],
            b_gate[l][None, :], cos, sin,
            layer=l, depth=depth, prompt_dims=(bp, tp), sample_dims=(bs, tn), states=states,
        )
        qx_p, kx_p = _forget_bias_operands(pre["logf"], bp, tp, ATTN_TILE)
        first_p = _first_live_key_tile(pre["qft"], pre["kf"], pre["logf"], bp, tp, ATTN_TILE)
        fo = _attention(pre["qft"], qx_p, pre["kf"], kx_p, pre["vft"], fox=True, first=first_p, **prompt)
        mo = _attention(pre["qnt"], pre["qrt"], pre["kn"], pre["kr"], pre["vmt"], fox=False, **prompt)

        logf_s = with_past(logf_c[l], pre["logf"][n_p:n]).reshape(bs * tk_s, LANES)
        qx_s, kx_s = _forget_bias_operands(logf_s, bs, tk_s, SAMPLE_KEY_TILE)
        qx_s = jnp.pad(qx_s[:, :, :, past:past + tn], ((0, 0), (0, 0), (0, 0), (0, tq_s - tn)))
        fo = _attention(
            sample_queries_t(pre["qft"]), qx_s, with_past(kf_c[l], pre["kf"][n_p:n]), kx_s,
            with_past_t(vft_c[l], pre["vft"][:, n_p:n]), fox=True, out=fo, **sample,
        )
        mo = _attention(
            sample_queries_t(pre["qnt"]), sample_queries_t(pre["qrt"]), with_past(kn_c[l], pre["kn"][n_p:n]),
            with_past(kr_c[l], pre["kr"][n_p:n]), with_past_t(vmt_c[l], pre["vmt"][:, n_p:n]), fox=False, out=mo,
            **sample,
        )
        x1, gates = _post_attention(
            x, fo, mo, pre["gate"], pa[l], pb[l], wo[l], ln1_g[l][None, :], ln1_b[l][None, :], wr3, br, alpha
        )
        x = _moe(x1, gates, wg[l], wu[l], wd[l], ln2_g[l][None, :], ln2_b[l][None, :], alpha)

    heads = (FOX_HEADS, FOX_HEAD_DIM)
    k_p, v_p, logf_p, ckv_p, kr_p, k_s, v_s, logf_s, ckv_s, kr_s = states
    y_prompt = x[:n_p].reshape(bp, tp, d)[:, N_META:t_real]
    y_sample = x[n_p:n].reshape(bs, tn, d)
    return (
        y_prompt,
        y_sample,
        k_p[:, :, :t_real].reshape(depth, bp, t_real, *heads),
        v_p[:, :, :t_real].reshape(depth, bp, t_real, *heads),
        logf_p[:, :, :t_real],
        ckv_p[:, :, :t_real],
        kr_p[:, :, :t_real],
        k_s.reshape(depth, bs, tn, *heads),
        v_s.reshape(depth, bs, tn, *heads),
        logf_s,
        ckv_s,
        kr_s,
    )
```

```python
import functools

import jax
import jax.numpy as jnp
import numpy as np
from jax import lax
from jax.experimental import pallas as pl
from jax.experimental.pallas import tpu as pltpu

F32 = jnp.float32
BF16 = jnp.bfloat16

CHUNK = 64
N_META = 16
FOX_HEADS = 8
FOX_HEAD_DIM = 64
FOX_WIDTH = FOX_HEADS * FOX_HEAD_DIM
MLA_HEADS = 8
MLA_NOPE_DIM = 64
MLA_ROPE_DIM = 32
MLA_QK_DIM = MLA_NOPE_DIM + MLA_ROPE_DIM
MLA_V_DIM = 64
N_EXPERTS = 16
N_GROUPS = 4
EXPERTS_PER_GROUP = N_EXPERTS // N_GROUPS
ROPE_THETA = 10000.0
LN_EPS = 1e-5
RMS_EPS = 1e-6
LOG2E = 1.4426950408889634

LANES = 128
ROW_TILE = 256
MOE_ROW_TILE = 512
ATTN_TILE = 256
ATTN_Q_TILE = 256
ATTN_SLOTS = 4
QK_GROUP = 2
SAMPLE_QUERY_TILE = 128
SAMPLE_KEY_TILE = 384
AUG_GROUP = 8
ONES_ROWS = 16
MASK_VALUE = -1e30
UNDERFLOW_MARGIN = 170.0
NORM_SLACK = 1.01
VMEM_LIMIT = 56 * 1024 * 1024

C_QKV = 0
C_CQ = 3 * FOX_WIDTH
C_CKV = C_CQ + 256
C_LOGF = C_CKV + 128
C_KR = C_LOGF + LANES
C_KRS = C_KR + LANES
C_GATE = C_KRS + LANES


def _params(*sem):
    return pltpu.CompilerParams(dimension_semantics=sem, vmem_limit_bytes=VMEM_LIMIT)


def _layer_norm(x, g, b):
    mu = jnp.mean(x, axis=-1, keepdims=True)
    xc = x - mu
    var = jnp.mean(xc * xc, axis=-1, keepdims=True)
    return xc * lax.rsqrt(var + LN_EPS) * g + b


def _rms_norm(x, g):
    return x * lax.rsqrt(jnp.mean(x * x, axis=-1, keepdims=True) + RMS_EPS) * g


def _dot(a, b):
    return jnp.dot(a, b, preferred_element_type=F32)


def _split3(x):
    hi = x.astype(BF16)
    r1 = x - hi.astype(F32)
    mid = r1.astype(BF16)
    lo = (r1 - mid.astype(F32)).astype(BF16)
    return hi, mid, lo


def _ln_kernel(x_ref, g_ref, b_ref, o_ref):
    o_ref[...] = _layer_norm(x_ref[...], g_ref[...], b_ref[...])


def _input_ln(x, g, b):
    n, d = x.shape
    return pl.pallas_call(
        _ln_kernel,
        grid=(n // ROW_TILE,),
        in_specs=[
            pl.BlockSpec((ROW_TILE, d), lambda i: (i, 0)),
            pl.BlockSpec((1, d), lambda i: (0, 0)),
            pl.BlockSpec((1, d), lambda i: (0, 0)),
        ],
        out_specs=pl.BlockSpec((ROW_TILE, d), lambda i: (i, 0)),
        out_shape=jax.ShapeDtypeStruct((n, d), F32),
        compiler_params=_params("parallel"),
        name="input_ln",
    )(x, g, b)


N_ROW_STATES = 5


def _pre_kernel(x_ref, w_ref, bf_ref, gq_ref, wq_ref, gkv_ref, wkv_ref, bg_ref, cos_ref, sin_ref, *rest,
                fox_scale, mla_scale, n_prompt_tiles):
    outs = rest[len(rest) - 10 - 2 * N_ROW_STATES:]
    qft_ref, kf_ref, vft_ref, logf_ref, qnt_ref, qrt_ref, kr_ref, kn_ref, vmt_ref, gate_ref = outs[:10]
    prompt_refs = outs[10:10 + N_ROW_STATES]
    sample_refs = outs[10 + N_ROW_STATES:]
    xb = x_ref[...].astype(BF16)
    cos = cos_ref[...]
    sin = sin_ref[...]

    qkv = _dot(xb, w_ref[:, C_QKV:C_CQ])
    qft_ref[...] = (qkv[:, 0:FOX_WIDTH] * fox_scale).T.astype(BF16)
    k = qkv[:, FOX_WIDTH:2 * FOX_WIDTH]
    v = qkv[:, 2 * FOX_WIDTH:3 * FOX_WIDTH]
    kf_ref[...] = k.astype(BF16)
    vft_ref[...] = v.T.astype(BF16)

    z = _dot(xb, w_ref[:, C_LOGF:C_KR]) + bf_ref[...]
    logf = jnp.minimum(z, 0.0) - jnp.log1p(jnp.exp(-jnp.abs(z)))
    logf_ref[...] = logf

    cq = _rms_norm(_dot(xb, w_ref[:, C_CQ:C_CKV]), gq_ref[...]).astype(BF16)
    qm = _dot(cq, wq_ref[...])
    qnt_ref[...] = (qm[:, 0:512] * mla_scale).T.astype(BF16)
    cos2 = jnp.concatenate([cos, cos], axis=1)
    sin2 = jnp.concatenate([sin, sin], axis=1)
    qrt_ref[...] = ((qm[:, 512:768] * cos2 + qm[:, 768:1024] * sin2) * mla_scale).T.astype(BF16)

    ckv = _rms_norm(_dot(xb, w_ref[:, C_CKV:C_LOGF]), gkv_ref[...])
    kv = _dot(ckv.astype(BF16), wkv_ref[...])
    kn_ref[...] = kv[:, 0:512].astype(BF16)
    vmt_ref[...] = kv[:, 512:1024].T.astype(BF16)
    kr = _dot(xb, w_ref[:, C_KR:C_KRS]) * cos + _dot(xb, w_ref[:, C_KRS:C_GATE]) * sin
    kr_ref[...] = kr.astype(BF16)

    g = _dot(xb, w_ref[:, C_GATE:]) + bg_ref[...]
    gate_ref[...] = jax.nn.sigmoid(g).astype(BF16)

    states = (k, v, logf[:, :FOX_HEADS], ckv, kr[:, :MLA_ROPE_DIM])
    i = pl.program_id(0)

    @pl.when(i < n_prompt_tiles)
    def _():
        for ref, val in zip(prompt_refs, states):
            ref[...] = val

    @pl.when(i >= n_prompt_tiles)
    def _():
        for ref, val in zip(sample_refs, states):
            ref[...] = val.reshape(ref.shape)


def _pre_attention(x, w, bf, gq, wq, gkv, wkv, bg, cos, sin, *, layer, depth, prompt_dims, sample_dims, states):
    n, d = x.shape
    tm = ROW_TILE
    bp, tp = prompt_dims
    bs, tn = sample_dims
    tiles_per_stream = tp // tm
    n_prompt_tiles = bp * tiles_per_stream
    streams_per_tile = tm // tn
    row = lambda width: pl.BlockSpec((tm, width), lambda i: (i, 0))
    col = lambda height: pl.BlockSpec((height, tm), lambda i: (0, i))
    full = lambda a: pl.BlockSpec(a.shape, lambda i: (0,) * a.ndim)
    outs = [
        ("qft", 512, BF16, True), ("kf", 512, BF16, False), ("vft", 512, BF16, True), ("logf", LANES, F32, False),
        ("qnt", 512, BF16, True), ("qrt", 256, BF16, True), ("kr", LANES, BF16, False), ("kn", 512, BF16, False),
        ("vmt", 512, BF16, True), ("gate", 2 * d, BF16, False),
    ]
    state_widths = (FOX_WIDTH, FOX_WIDTH, FOX_HEADS, 128, MLA_ROPE_DIM)

    def prompt_block(i):
        ip = jnp.minimum(i, n_prompt_tiles - 1)
        return layer, ip // tiles_per_stream, ip % tiles_per_stream, 0

    def sample_block(i):
        return layer, jnp.maximum(i - n_prompt_tiles, 0), 0, 0

    out_specs = [col(wd) if tr else row(wd) for _, wd, _, tr in outs]
    out_specs += [pl.BlockSpec((None, None, tm, wd), prompt_block) for wd in state_widths]
    out_specs += [pl.BlockSpec((None, streams_per_tile, tn, wd), sample_block) for wd in state_widths]
    out_shape = [jax.ShapeDtypeStruct((wd, n) if tr else (n, wd), dt) for _, wd, dt, tr in outs]
    out_shape += [jax.ShapeDtypeStruct((depth, bp, tp, wd), F32) for wd in state_widths]
    out_shape += [jax.ShapeDtypeStruct((depth, bs, tn, wd), F32) for wd in state_widths]
    args = [x, w, bf, gq, wq, gkv, wkv, bg, cos, sin]
    in_specs = [row(d), full(w), full(bf), full(gq), full(wq), full(gkv), full(wkv), full(bg), row(LANES), row(LANES)]
    aliases = {len(args) + j: len(outs) + j for j in range(len(states))}
    args += list(states)
    in_specs += [pl.BlockSpec(memory_space=pl.ANY)] * len(states)
    res = pl.pallas_call(
        functools.partial(
            _pre_kernel, fox_scale=FOX_HEAD_DIM ** -0.5 * LOG2E, mla_scale=MLA_QK_DIM ** -0.5 * LOG2E,
            n_prompt_tiles=n_prompt_tiles,
        ),
        grid=(n // tm,),
        in_specs=in_specs,
        out_specs=out_specs,
        out_shape=out_shape,
        input_output_aliases=aliases,
        compiler_params=_params("arbitrary"),
        name="pre_attention",
    )(*args)
    return {name: r for (name, _, _, _), r in zip(outs, res)}, tuple(res[len(outs):])


def _kvup_kernel(c_ref, w_ref, kn_ref, vmt_ref):
    kv = _dot(c_ref[0].astype(BF16), w_ref[0])
    kn_ref[0] = kv[:, 0:512].astype(BF16)
    vmt_ref[0] = kv[:, 512:1024].T.astype(BF16)


def _kv_up(ckv, wkv):
    depth, n, r = ckv.shape
    tm = ROW_TILE
    return pl.pallas_call(
        _kvup_kernel,
        grid=(depth, n // tm),
        in_specs=[pl.BlockSpec((1, tm, r), lambda l, i: (l, i, 0)), pl.BlockSpec((1, r, 1024), lambda l, i: (l, 0, 0))],
        out_specs=[pl.BlockSpec((1, tm, 512), lambda l, i: (l, i, 0)), pl.BlockSpec((1, 512, tm), lambda l, i: (l, 0, i))],
        out_shape=[jax.ShapeDtypeStruct((depth, n, 512), BF16), jax.ShapeDtypeStruct((depth, 512, n), BF16)],
        compiler_params=_params("parallel", "parallel"),
        name="kv_up_cached",
    )(ckv, wkv)


def _bias_layout():
    sel_q = np.zeros((3 * LANES, 4 * LANES), np.float32)
    sel_k = np.zeros((3 * LANES, 4 * LANES), np.float32)
    one_q = np.zeros((1, 4 * LANES), np.float32)
    one_k = np.zeros((1, 4 * LANES), np.float32)
    for h in range(FOX_HEADS):
        base = (h // 2) * LANES + (h % 2) * AUG_GROUP
        for piece in range(3):
            sel_q[piece * LANES + h, base + piece] = 1.0
            sel_k[piece * LANES + h, base + 3 + piece] = -1.0
            one_q[0, base + 3 + piece] = 1.0
            one_k[0, base + piece] = 1.0
    return (jnp.asarray(sel_q, BF16), jnp.asarray(one_q, F32), jnp.asarray(sel_k, BF16), jnp.asarray(one_k, F32))


def _cumsum_kernel(x_ref, sq_ref, oq_ref, sk_ref, ok_ref, qxt_ref, kx_ref, carry_ref, *, tt):
    @pl.when(pl.program_id(1) == 0)
    def _():
        carry_ref[...] = jnp.zeros_like(carry_ref)

    r = lax.broadcasted_iota(jnp.int32, (tt, tt), 0)
    c = lax.broadcasted_iota(jnp.int32, (tt, tt), 1)
    tri = jnp.where(c <= r, 1.0, 0.0).astype(BF16)
    hi, mid, lo = _split3(x_ref[...])
    cum = _dot(tri, lo) + _dot(tri, mid) + _dot(tri, hi) + carry_ref[...]
    carry_ref[...] = cum[tt - 1:tt, :]
    pieces = jnp.concatenate(_split3(cum * LOG2E), axis=1)
    kx = (_dot(pieces, sk_ref[...]) + ok_ref[...]).astype(BF16)
    qx = _dot(pieces, sq_ref[...]) + oq_ref[...]
    for p in range(FOX_HEADS // 2):
        kx_ref[p] = kx[:, p * LANES:(p + 1) * LANES]
        qxt_ref[p] = qx[:, p * LANES:(p + 1) * LANES].T.astype(BF16)


def _forget_bias_operands(logf, n_seq, t, tt):
    w = logf.shape[-1]
    tiles = t // tt
    consts = _bias_layout()
    pairs = FOX_HEADS // 2
    return pl.pallas_call(
        functools.partial(_cumsum_kernel, tt=tt),
        grid=(n_seq, tiles),
        in_specs=[pl.BlockSpec((tt, w), lambda b, i: (b * tiles + i, 0))]
        + [pl.BlockSpec(c.shape, lambda b, i: (0, 0)) for c in consts],
        out_specs=[
            pl.BlockSpec((None, pairs, LANES, tt), lambda b, i: (b, 0, 0, i)),
            pl.BlockSpec((None, pairs, tt, LANES), lambda b, i: (b, 0, i, 0)),
        ],
        out_shape=[
            jax.ShapeDtypeStruct((n_seq, pairs, LANES, t), BF16),
            jax.ShapeDtypeStruct((n_seq, pairs, t, LANES), BF16),
        ],
        scratch_shapes=[pltpu.VMEM((1, w), F32)],
        compiler_params=_params("parallel", "arbitrary"),
        name="cumsum_time",
    )(logf, *consts)


def _chunk_end(pos, n_meta):
    body = n_meta + ((pos - n_meta) // CHUNK) * CHUNK + CHUNK - 1
    if n_meta == 0:
        return body
    return jnp.where(pos < n_meta, n_meta - 1, body)


def _attn_kernel(first_ref, qt_ref, qxt_ref, k_ref, kx_ref, vt_ref, *rest, fox, tq, tk, q_off, n_meta, n_ktiles):
    o_ref, *slots = rest[len(rest) - ATTN_SLOTS - 1:]
    p = pl.program_id(1)
    i = pl.program_id(2)
    q_lo = q_off + i * tq
    q_hi = q_lo + tq - 1
    if fox:
        vis_lo, vis_hi = q_lo, q_hi
    else:
        vis_lo, vis_hi = _chunk_end(q_lo, n_meta), _chunk_end(q_hi, n_meta)
    n_tot = jnp.minimum(vis_hi // tk + 1, n_ktiles)
    n_full = jnp.minimum((vis_lo + 1) // tk, n_tot - 1)
    first = jnp.clip(first_ref[pl.program_id(0), p, i], 0, n_full)
    n_tot = n_tot - first
    n_full = n_full - first

    q_pos = q_lo + lax.broadcasted_iota(jnp.int32, (1, tq), 1)
    row = lax.broadcasted_iota(jnp.int32, (LANES, 1), 0)
    qt2 = qt_ref[...]
    qxt2 = qxt_ref[...]
    cols = []
    for hh in range(2):
        x_group = row // AUG_GROUP == hh if fox else row // MLA_ROPE_DIM == 2 * (p % 2) + hh
        cols.append(jnp.concatenate([
            jnp.where(row // FOX_HEAD_DIM == hh, qt2, jnp.zeros_like(qt2)),
            jnp.where(x_group, qxt2, jnp.zeros_like(qxt2)),
        ], axis=0))
    qcbt = jnp.concatenate(cols, axis=1)
    last_visible = q_pos if fox else _chunk_end(q_pos, n_meta)
    last_visible = jnp.concatenate([last_visible, last_visible], axis=1)
    key_iota = lax.broadcasted_iota(jnp.int32, (tk, 2 * tq), 0)
    ones = jnp.ones((ONES_ROWS, tk), BF16)
    n_slots = len(slots)

    def key_rows(j):
        return pl.ds(pl.multiple_of(jnp.minimum(first + j, n_ktiles - 1) * tk, tk), tk)

    def produce(slot_refs, j, masked):
        kcs = []
        for r in range(len(slot_refs)):
            ks = key_rows(j + r)
            kcs.append(jnp.concatenate([k_ref[ks, :], kx_ref[ks, :]], axis=1))
        st_all = _dot(jnp.concatenate(kcs, axis=0), qcbt)
        cmax = []
        for r, slot_ref in enumerate(slot_refs):
            st = st_all[r * tk:(r + 1) * tk]
            if masked:
                st = jnp.where(key_iota <= last_visible - (first + j + r) * tk, st, MASK_VALUE)
            slot_ref[...] = st
            cmax.append(jnp.max(st, axis=0, keepdims=True))
        return cmax

    def accumulate(j, pt, alpha, accs):
        ks = key_rows(j)
        new = []
        for hh in range(2):
            va = jnp.concatenate([vt_ref[FOX_HEAD_DIM * hh:FOX_HEAD_DIM * (hh + 1), ks], ones], axis=0)
            sl = slice(hh * tq, (hh + 1) * tq)
            new.append(alpha[:, sl] * accs[hh] + _dot(va, pt[:, sl]))
        return tuple(new)

    def step(u, carry, mode):
        m, cmax, accs = carry
        cmax = list(cmax)
        for r0 in range(0, n_slots, QK_GROUP):
            group = range(r0, r0 + QK_GROUP)
            pts, alphas = [], []
            for r in group:
                m_new = jnp.maximum(m, cmax[r])
                pts.append(jnp.exp2(slots[r][...] - m_new).astype(BF16))
                alphas.append(jnp.exp2(m - m_new))
                m = m_new
            if mode != "drain":
                cmax[r0:r0 + QK_GROUP] = produce([slots[r] for r in group], n_slots * (u + 1) + r0, mode == "masked")
            for r, pt, alpha in zip(group, pts, alphas):
                accs = accumulate(n_slots * u + r, pt, alpha, accs)
        return m, tuple(cmax), accs

    def two_steps(v, carry):
        return step(2 * v + 1, step(2 * v, carry, "plain"), "plain")

    m0 = jnp.full((1, 2 * tq), MASK_VALUE, F32)
    a0 = jnp.zeros((FOX_HEAD_DIM + ONES_ROWS, tq), F32)
    carry = (m0, tuple(produce(list(slots), 0, True)), (a0, a0))
    n_steps = (n_tot + n_slots - 1) // n_slots
    n_plain = jnp.maximum(n_full - n_slots, 0) // n_slots
    carry = lax.fori_loop(0, n_plain // 2, two_steps, carry)
    carry = lax.fori_loop(n_plain - n_plain % 2, n_plain, functools.partial(step, mode="plain"), carry)
    carry = lax.fori_loop(n_plain, n_steps - 1, functools.partial(step, mode="masked"), carry)
    _, _, accs = step(n_steps - 1, carry, "drain")
    out_t = jnp.concatenate(
        [acc[0:FOX_HEAD_DIM] / acc[FOX_HEAD_DIM:FOX_HEAD_DIM + 1] for acc in accs], axis=0
    )
    o_ref[...] = out_t.T[:o_ref.shape[0]].astype(o_ref.dtype)


def _attention(qt, qxt, k, kx, vt, *, fox, n_batch, t_q, t_k, tq, tk, q_off, n_meta, out_rows, out_row0=0, out=None,
               first=None):
    nq = t_q // tq

    def time_major(a, rows_of, t, tile):
        tiles = t // tile
        if a.ndim == 2:
            return pl.BlockSpec((LANES, tile), lambda bb, p, i, _: (rows_of(p), bb * tiles + i % tiles))
        return pl.BlockSpec((None, LANES, tile), lambda bb, p, i, _: (bb, rows_of(p), i % tiles))

    def row_major(a, col_of):
        if a.ndim == 2:
            return pl.BlockSpec((t_k, LANES), lambda bb, p, i, _: (bb, col_of(p)))
        return pl.BlockSpec((None, t_k, LANES), lambda bb, p, i, _: (bb, 0, col_of(p)))

    qspec = time_major(qt, lambda p: p, t_q, tq)
    kspec = row_major(k, lambda p: p)
    if fox:
        qxspec = pl.BlockSpec((None, None, LANES, tq), lambda bb, p, i, _: (bb, p, 0, i))
        kxspec = pl.BlockSpec((None, None, t_k, LANES), lambda bb, p, i, _: (bb, p, 0, 0))
    else:
        qxspec = time_major(qxt, lambda p: p // 2, t_q, tq)
        kxspec = row_major(kx, lambda p: 0)
    vspec = time_major(vt, lambda p: p, t_k, t_k)
    if first is None:
        first = jnp.zeros((n_batch, 4, nq), jnp.int32)
    args = [first, qt, qxt, k, kx, vt]
    in_specs = [qspec, qxspec, kspec, kxspec, vspec]
    if isinstance(out, int):
        n_rows, aliases = out, {}
    else:
        n_rows, aliases = out.shape[0], {len(args): 0}
        args.append(out)
        in_specs.append(pl.BlockSpec(memory_space=pl.ANY))
    block0 = out_row0 // out_rows
    return pl.pallas_call(
        functools.partial(_attn_kernel, fox=fox, tq=tq, tk=tk, q_off=q_off, n_meta=n_meta, n_ktiles=t_k // tk),
        grid_spec=pltpu.PrefetchScalarGridSpec(
            num_scalar_prefetch=1,
            grid=(n_batch, 4, nq),
            in_specs=in_specs,
            out_specs=pl.BlockSpec((out_rows, LANES), lambda bb, p, i, _: (block0 + bb * nq + i, p)),
            scratch_shapes=[pltpu.VMEM((tk, 2 * tq), F32)] * ATTN_SLOTS,
        ),
        out_shape=jax.ShapeDtypeStruct((n_rows, 512), BF16),
        input_output_aliases=aliases,
        compiler_params=_params("parallel", "parallel", "arbitrary"),
        name="fox_attention" if fox else "mla_attention",
    )(*args)


def _first_live_key_tile(qft, kf, logf, n_seq, t, tile):
    n = n_seq * t
    tiles = t // tile
    h, dh = FOX_HEADS, FOX_HEAD_DIM
    head_of = jnp.asarray(np.arange(h * dh)[:, None] // dh == np.arange(h)[None, :], F32)
    sumsq = functools.partial(jnp.dot, precision=lax.Precision.HIGHEST)
    q = qft[:, :n].astype(F32)
    k = kf[:n].astype(F32)
    qn = jnp.sqrt(sumsq(head_of.T, q * q)) * NORM_SLACK
    kn = jnp.sqrt(sumsq(k * k, head_of)) * NORM_SLACK
    qmax = jnp.transpose(jnp.max(qn.reshape(h, n_seq, tiles, tile), axis=-1), (1, 2, 0))
    kmax = jnp.max(kn.reshape(n_seq, tiles, tile, h), axis=2)
    cum2 = jnp.cumsum(logf[:n, :h].reshape(n_seq, t, h), axis=1) * LOG2E
    cum_q = cum2[:, 0::tile]
    cum_k = cum2[:, tile - 1::tile]
    bound = qmax[:, :, None] * lax.cummax(kmax, axis=1)[:, None] + cum_q[:, :, None] - cum_k[:, None]
    own = -qmax * kmax
    dead = bound < own[:, :, None] - UNDERFLOW_MARGIN
    dead = jnp.all(dead.reshape(n_seq, tiles, tiles, h // 2, 2), axis=-1)
    first = jnp.sum(jnp.cumprod(dead.astype(jnp.int32), axis=2), axis=2)
    return jnp.transpose(first, (0, 2, 1))


def _group_partner(x, lane, d, width):
    fwd = pltpu.roll(x, LANES - d, 1)
    back = pltpu.roll(x, width - d, 1)
    return jnp.where(lane % width + d < width, fwd, back)


def _route(logits, b_router, lane):
    scores = jax.nn.sigmoid(logits)
    sel = scores + b_router
    e = EXPERTS_PER_GROUP
    rank = jnp.zeros(sel.shape, jnp.int32)
    for d in range(1, e):
        other = _group_partner(sel, lane, d, e)
        other_first = lane % e + d >= e
        beats = (other > sel) | ((other == sel) & other_first)
        rank = rank + beats.astype(jnp.int32)
    top2 = rank < 2
    kept = jnp.where(top2, sel, 0.0)
    gscore = kept
    for d in range(1, e):
        gscore = gscore + _group_partner(kept, lane, d, e)
    grank = jnp.zeros_like(rank)
    for gstep in range(1, N_GROUPS):
        d = gstep * e
        other = _group_partner(gscore, lane, d, N_EXPERTS)
        other_first = lane % N_EXPERTS + d >= N_EXPERTS
        beats = (other > gscore) | ((other == gscore) & other_first)
        grank = grank + beats.astype(jnp.int32)
    chosen = top2 & (grank == 0) & (lane < N_EXPERTS)
    w = jnp.where(chosen, scores, 0.0)
    return w / jnp.sum(w, axis=1, keepdims=True)


def _post_kernel(x_ref, fo_ref, mo_ref, gate_ref, pa_ref, pb_ref, wo_ref, g1_ref, b1_ref, wr_ref, br_ref,
                 x1_ref, gates_ref, *, alpha):
    d = x_ref.shape[1]
    merged = gate_ref[:, 0:d].astype(F32) * _dot(fo_ref[...], pa_ref[...])
    merged = merged + gate_ref[:, d:2 * d].astype(F32) * _dot(mo_ref[...], pb_ref[...])
    mix = _dot(merged.astype(BF16), wo_ref[...])
    x1 = _layer_norm(alpha * x_ref[...] + mix, g1_ref[...], b1_ref[...])
    x1_ref[...] = x1
    hi, mid, lo = _split3(x1)
    a = _dot(hi, wr_ref[:, 0:2 * LANES])
    b = _dot(mid, wr_ref[:, 0:2 * LANES])
    logits = _dot(lo, wr_ref[:, 0:LANES]) + _dot(hi, wr_ref[:, 2 * LANES:3 * LANES])
    logits = logits + b[:, LANES:] + a[:, LANES:] + b[:, :LANES] + a[:, :LANES]
    lane = lax.broadcasted_iota(jnp.int32, logits.shape, 1)
    gates_ref[...] = _route(logits, br_ref[...], lane)


def _post_attention(x, fo, mo, gate, pa, pb, wo, g1, b1, wr3, br, alpha):
    n, d = x.shape
    tm = ROW_TILE
    row = lambda width: pl.BlockSpec((tm, width), lambda i: (i, 0))
    full = lambda a: pl.BlockSpec(a.shape, lambda i: (0,) * a.ndim)
    return pl.pallas_call(
        functools.partial(_post_kernel, alpha=alpha),
        grid=(n // tm,),
        in_specs=[row(d), row(512), row(512), row(2 * d), full(pa), full(pb), full(wo), full(g1), full(b1), full(wr3), full(br)],
        out_specs=[row(d), row(LANES)],
        out_shape=[jax.ShapeDtypeStruct((n, d), F32), jax.ShapeDtypeStruct((n, LANES), F32)],
        compiler_params=_params("parallel"),
        name="post_attention",
    )(x, fo, mo, gate, pa, pb, wo, g1, b1, wr3, br)


def _moe_kernel(x_ref, gates_ref, wg_ref, wu_ref, wd_ref, g2_ref, b2_ref, o_ref, xb_ref, acc_ref, *, alpha):
    e = pl.program_id(1)

    @pl.when(e == 0)
    def _():
        xb_ref[...] = x_ref[...].astype(BF16)
        acc_ref[...] = jnp.zeros_like(acc_ref)

    lane = lax.broadcasted_iota(jnp.int32, (1, LANES), 1)
    gate = jnp.sum(jnp.where(lane == e, gates_ref[...], 0.0), axis=1, keepdims=True)
    xb = xb_ref[...]
    hid = jax.nn.silu(_dot(xb, wg_ref[0])) * _dot(xb, wu_ref[0])
    acc_ref[...] += _dot((hid * gate).astype(BF16), wd_ref[0])

    @pl.when(e == pl.num_programs(1) - 1)
    def _():
        o_ref[...] = _layer_norm(alpha * x_ref[...] + acc_ref[...], g2_ref[...], b2_ref[...])


def _moe(x, gates, wg, wu, wd, g2, b2, alpha):
    n, d = x.shape
    n_exp, _, f = wg.shape
    tm = MOE_ROW_TILE if n % MOE_ROW_TILE == 0 else ROW_TILE
    return pl.pallas_call(
        functools.partial(_moe_kernel, alpha=alpha),
        grid=(n // tm, n_exp),
        in_specs=[
            pl.BlockSpec((tm, d), lambda i, e: (i, 0)),
            pl.BlockSpec((tm, LANES), lambda i, e: (i, 0)),
            pl.BlockSpec((1, d, f), lambda i, e: (e, 0, 0)),
            pl.BlockSpec((1, d, f), lambda i, e: (e, 0, 0)),
            pl.BlockSpec((1, f, d), lambda i, e: (e, 0, 0)),
            pl.BlockSpec((1, d), lambda i, e: (0, 0)),
            pl.BlockSpec((1, d), lambda i, e: (0, 0)),
        ],
        out_specs=pl.BlockSpec((tm, d), lambda i, e: (i, 0)),
        out_shape=jax.ShapeDtypeStruct((n, d), F32),
        scratch_shapes=[pltpu.VMEM((tm, d), BF16), pltpu.VMEM((tm, d), F32)],
        compiler_params=_params("parallel", "arbitrary"),
        name="moe_ln2",
    )(x, gates, wg, wu, wd, g2, b2)


def _rope_tables(pos):
    half = MLA_ROPE_DIM // 2
    inv_freq = ROPE_THETA ** (-jnp.arange(half, dtype=F32) / half)
    ang = pos.astype(F32)[:, None] * inv_freq[None, :]
    cos, sin = jnp.cos(ang), jnp.sin(ang)
    cos32 = jnp.concatenate([cos, cos], axis=1)
    sin32 = jnp.concatenate([-sin, sin], axis=1)
    return jnp.tile(cos32, (1, LANES // MLA_ROPE_DIM)), jnp.tile(sin32, (1, LANES // MLA_ROPE_DIM))


def _swap_halves(w):
    half = MLA_ROPE_DIM // 2
    return jnp.concatenate([w[..., half:], w[..., :half]], axis=-1)


def _prepare_w_in(w_in):
    depth, d, _ = w_in.shape
    o = [0]
    for sz in (FOX_WIDTH, FOX_WIDTH, FOX_WIDTH, FOX_HEADS, 256, 128, MLA_ROPE_DIM, 2 * d):
        o.append(o[-1] + sz)
    qkv = w_in[..., o[0]:o[3]]
    wf = w_in[..., o[3]:o[4]]
    wcq = w_in[..., o[4]:o[5]]
    wckv = w_in[..., o[5]:o[6]]
    wkr = w_in[..., o[6]:o[7]]
    wg = w_in[..., o[7]:o[8]]
    reps = LANES // MLA_ROPE_DIM
    wf_pad = jnp.concatenate([wf, jnp.zeros((depth, d, LANES - FOX_HEADS), w_in.dtype)], axis=-1)
    return jnp.concatenate(
        [qkv, wcq, wckv, wf_pad, jnp.tile(wkr, (1, 1, reps)), jnp.tile(_swap_halves(wkr), (1, 1, reps)), wg], axis=-1
    ).astype(BF16)


def _prepare_w_q_up(w):
    depth, r, _ = w.shape
    w = w.reshape(depth, r, MLA_HEADS, MLA_QK_DIM)
    nope = w[..., :MLA_NOPE_DIM].reshape(depth, r, MLA_HEADS * MLA_NOPE_DIM)
    rope = w[..., MLA_NOPE_DIM:]
    rope_sw = _swap_halves(rope).reshape(depth, r, MLA_HEADS * MLA_ROPE_DIM)
    rope = rope.reshape(depth, r, MLA_HEADS * MLA_ROPE_DIM)
    return jnp.concatenate([nope, rope, rope_sw], axis=-1).astype(BF16)


def _prepare_w_kv_up(w):
    depth, r, _ = w.shape
    w = w.reshape(depth, r, MLA_HEADS, MLA_NOPE_DIM + MLA_V_DIM)
    kn = w[..., :MLA_NOPE_DIM].reshape(depth, r, MLA_HEADS * MLA_NOPE_DIM)
    vm = w[..., MLA_NOPE_DIM:].reshape(depth, r, MLA_HEADS * MLA_V_DIM)
    return jnp.concatenate([kn, vm], axis=-1).astype(BF16)


def _pad_lanes(x, width=LANES):
    return jnp.pad(x, [(0, 0)] * (x.ndim - 1) + [(0, width - x.shape[-1])])


def kernel(x_prompt, x_sample, cache_fox_k, cache_fox_v, cache_fox_logf, cache_mla_ckv, cache_mla_krope, meta_tokens, ln_in_g, ln_in_b, w_in, b_fgt, g_q_norm, w_q_up, g_kv_norm, w_kv_up, w_proj_fox, w_proj_mla, b_gate, w_out, ln1_g, ln1_b, ln2_g, ln2_b, w_router, b_router, w_exp_gate, w_exp_up, w_exp_down):
    bp, seq, d = x_prompt.shape
    bs, tn, _ = x_sample.shape
    depth = w_in.shape[0]
    past = cache_fox_k.shape[2]
    alpha = (2 * depth) ** 0.25
    t_real = N_META + seq
    tp = -(-t_real // ATTN_Q_TILE) * ATTN_Q_TILE
    n_p = bp * tp
    n_s = bs * tn
    n = n_p + n_s
    n_pad = n
    tk_s = -(-(past + tn) // SAMPLE_KEY_TILE) * SAMPLE_KEY_TILE
    tq_s = SAMPLE_QUERY_TILE
    assert n % ROW_TILE == 0 and ROW_TILE % tn == 0 and tn <= tq_s, (n, tn)
    assert ATTN_Q_TILE == ATTN_TILE

    meta = jnp.broadcast_to(meta_tokens[None].astype(F32), (bp, N_META, d))
    xp = jnp.concatenate([meta, x_prompt, jnp.zeros((bp, tp - t_real, d), F32)], axis=1)
    x = jnp.concatenate([xp.reshape(n_p, d), x_sample.reshape(n_s, d), jnp.zeros((n_pad - n, d), F32)], axis=0)
    pos = jnp.concatenate([
        jnp.tile(jnp.arange(tp, dtype=jnp.int32), bp),
        jnp.tile(past + jnp.arange(tn, dtype=jnp.int32), bs),
        jnp.zeros((n_pad - n,), jnp.int32),
    ])
    cos, sin = _rope_tables(pos)

    w_in_r = _prepare_w_in(w_in)
    w_q_r = _prepare_w_q_up(w_q_up)
    w_kv_r = _prepare_w_kv_up(w_kv_up)
    bf_r = _pad_lanes(b_fgt)[:, None, :]
    pa, pb, wo = w_proj_fox.astype(BF16), w_proj_mla.astype(BF16), w_out.astype(BF16)
    wg, wu, wd = w_exp_gate.astype(BF16), w_exp_up.astype(BF16), w_exp_down.astype(BF16)
    wr = _pad_lanes(w_router)
    keep_bf16_bits = lambda a: lax.bitcast_convert_type(
        lax.bitcast_convert_type(a, jnp.uint32) & jnp.uint32(0xFFFF0000), F32
    )
    wr_hi = keep_bf16_bits(wr)
    wr_mid = keep_bf16_bits(wr - wr_hi)
    wr_lo = wr - wr_hi - wr_mid
    wr3 = jnp.concatenate([wr_hi, wr_mid, wr_lo], axis=1).astype(BF16)
    br = _pad_lanes(b_router[None, :])

    zpad = tk_s - past - tn
    kn_c, vmt_c = _kv_up(cache_mla_ckv.reshape(depth, bs * past, -1), w_kv_r)
    kn_c = kn_c.reshape(depth, bs, past, 512)
    vmt_c = jnp.transpose(vmt_c.reshape(depth, 512, bs, past), (0, 2, 1, 3))
    kf_c = cache_fox_k.reshape(depth, bs, past, FOX_WIDTH).astype(BF16)
    vft_c = jnp.transpose(cache_fox_v.reshape(depth, bs, past, FOX_WIDTH), (0, 1, 3, 2)).astype(BF16)
    kr_c = jnp.tile(cache_mla_krope, (1, 1, 1, LANES // MLA_ROPE_DIM)).astype(BF16)
    logf_c = _pad_lanes(cache_fox_logf)

    def with_past(cached, new):
        new = new.reshape(bs, tn, new.shape[-1])
        return jnp.concatenate([cached, new, jnp.zeros((bs, zpad, new.shape[-1]), new.dtype)], axis=1)

    def with_past_t(cached_t, new_t):
        new_t = jnp.transpose(new_t.reshape(512, bs, tn), (1, 0, 2))
        return jnp.concatenate([cached_t, new_t, jnp.zeros((bs, 512, zpad), new_t.dtype)], axis=2)

    def sample_queries_t(a_t):
        a_t = jnp.transpose(a_t[:, n_p:n].reshape(a_t.shape[0], bs, tn), (1, 0, 2))
        return jnp.pad(a_t, ((0, 0), (0, 0), (0, tq_s - tn)))

    prompt = dict(
        n_batch=bp, t_q=tp, t_k=tp, tq=ATTN_Q_TILE, tk=ATTN_TILE, q_off=0, n_meta=N_META, out_rows=ATTN_Q_TILE,
        out=n_pad,
    )
    sample = dict(
        n_batch=bs, t_q=tq_s, t_k=tk_s, tq=tq_s, tk=SAMPLE_KEY_TILE, q_off=past, n_meta=0, out_rows=tn, out_row0=n_p
    )

    x = _input_ln(x, ln_in_g[None, :], ln_in_b[None, :])
    state_widths = (FOX_WIDTH, FOX_WIDTH, FOX_HEADS, 128, MLA_ROPE_DIM)
    states = tuple(jnp.zeros((depth, bp, tp, wd), F32) for wd in state_widths)
    states += tuple(jnp.zeros((depth, bs, tn, wd), F32) for wd in state_widths)
    for l in range(depth):
        pre, states = _pre_attention(
            x, w_in_r[l], bf_r[l], g_q_norm[l][None, :], w_q_r[l], g_kv_norm[l][None, :], w_kv_r[l],
            b_gate[l][None, :], cos, sin,
            layer=l, depth=depth, prompt_dims=(bp, tp), sample_dims=(bs, tn), states=states,
        )
        qx_p, kx_p = _forget_bias_operands(pre["logf"], bp, tp, ATTN_TILE)
        first_p = _first_live_key_tile(pre["qft"], pre["kf"], pre["logf"], bp, tp, ATTN_TILE)
        fo = _attention(pre["qft"], qx_p, pre["kf"], kx_p, pre["vft"], fox=True, first=first_p, **prompt)
        mo = _attention(pre["qnt"], pre["qrt"], pre["kn"], pre["kr"], pre["vmt"], fox=False, **prompt)

        logf_s = with_past(logf_c[l], pre["logf"][n_p:n]).reshape(bs * tk_s, LANES)
        qx_s, kx_s = _forget_bias_operands(logf_s, bs, tk_s, SAMPLE_KEY_TILE)
        qx_s = jnp.pad(qx_s[:, :, :, past:past + tn], ((0, 0), (0, 0), (0, 0), (0, tq_s - tn)))
        fo = _attention(
            sample_queries_t(pre["qft"]), qx_s, with_past(kf_c[l], pre["kf"][n_p:n]), kx_s,
            with_past_t(vft_c[l], pre["vft"][:, n_p:n]), fox=True, out=fo, **sample,
        )
        mo = _attention(
            sample_queries_t(pre["qnt"]), sample_queries_t(pre["qrt"]), with_past(kn_c[l], pre["kn"][n_p:n]),
            with_past(kr_c[l], pre["kr"][n_p:n]), with_past_t(vmt_c[l], pre["vmt"][:, n_p:n]), fox=False, out=mo,
            **sample,
        )
        x1, gates = _post_attention(
            x, fo, mo, pre["gate"], pa[l], pb[l], wo[l], ln1_g[l][None, :], ln1_b[l][None, :], wr3, br, alpha
        )
        x = _moe(x1, gates, wg[l], wu[l], wd[l], ln2_g[l][None, :], ln2_b[l][None, :], alpha)

    heads = (FOX_HEADS, FOX_HEAD_DIM)
    k_p, v_p, logf_p, ckv_p, kr_p, k_s, v_s, logf_s, ckv_s, kr_s = states
    y_prompt = x[:n_p].reshape(bp, tp, d)[:, N_META:t_real]
    y_sample = x[n_p:n].reshape(bs, tn, d)
    return (
        y_prompt,
        y_sample,
        k_p[:, :, :t_real].reshape(depth, bp, t_real, *heads),
        v_p[:, :, :t_real].reshape(depth, bp, t_real, *heads),
        logf_p[:, :, :t_real],
        ckv_p[:, :, :t_real],
        kr_p[:, :, :t_real],
        k_s.reshape(depth, bs, tn, *heads),
        v_s.reshape(depth, bs, tn, *heads),
        logf_s,
        ckv_s,
        kr_s,
    )
```

```python
import functools

import jax
import jax.numpy as jnp
import numpy as np
from jax import lax
from jax.experimental import pallas as pl
from jax.experimental.pallas import tpu as pltpu

F32 = jnp.float32
BF16 = jnp.bfloat16

CHUNK = 64
N_META = 16
FOX_HEADS = 8
FOX_HEAD_DIM = 64
FOX_WIDTH = FOX_HEADS * FOX_HEAD_DIM
MLA_HEADS = 8
MLA_NOPE_DIM = 64
MLA_ROPE_DIM = 32
MLA_QK_DIM = MLA_NOPE_DIM + MLA_ROPE_DIM
MLA_V_DIM = 64
N_EXPERTS = 16
N_GROUPS = 4
EXPERTS_PER_GROUP = N_EXPERTS // N_GROUPS
ROPE_THETA = 10000.0
LN_EPS = 1e-5
RMS_EPS = 1e-6
LOG2E = 1.4426950408889634

LANES = 128
ROW_TILE = 256
MOE_ROW_TILE = 512
ATTN_TILE = 256
ATTN_Q_TILE = 256
ATTN_SLOTS = 4
QK_GROUP = 2
SAMPLE_QUERY_TILE = 128
SAMPLE_KEY_TILE = 384
AUG_GROUP = 8
ONES_ROWS = 16
MASK_VALUE = -1e30
UNDERFLOW_MARGIN = 170.0
NORM_SLACK = 1.01
VMEM_LIMIT = 56 * 1024 * 1024

C_QKV = 0
C_CQ = 3 * FOX_WIDTH
C_CKV = C_CQ + 256
C_LOGF = C_CKV + 128
C_KR = C_LOGF + LANES
C_KRS = C_KR + LANES
C_GATE = C_KRS + LANES


def _params(*sem):
    return pltpu.CompilerParams(dimension_semantics=sem, vmem_limit_bytes=VMEM_LIMIT)


def _layer_norm(x, g, b):
    mu = jnp.mean(x, axis=-1, keepdims=True)
    xc = x - mu
    var = jnp.mean(xc * xc, axis=-1, keepdims=True)
    return xc * lax.rsqrt(var + LN_EPS) * g + b


def _rms_norm(x, g):
    return x * lax.rsqrt(jnp.mean(x * x, axis=-1, keepdims=True) + RMS_EPS) * g


def _dot(a, b):
    return jnp.dot(a, b, preferred_element_type=F32)


def _split3(x):
    hi = x.astype(BF16)
    r1 = x - hi.astype(F32)
    mid = r1.astype(BF16)
    lo = (r1 - mid.astype(F32)).astype(BF16)
    return hi, mid, lo


def _ln_kernel(x_ref, g_ref, b_ref, o_ref):
    o_ref[...] = _layer_norm(x_ref[...], g_ref[...], b_ref[...])


def _input_ln(x, g, b):
    n, d = x.shape
    return pl.pallas_call(
        _ln_kernel,
        grid=(n // ROW_TILE,),
        in_specs=[
            pl.BlockSpec((ROW_TILE, d), lambda i: (i, 0)),
            pl.BlockSpec((1, d), lambda i: (0, 0)),
            pl.BlockSpec((1, d), lambda i: (0, 0)),
        ],
        out_specs=pl.BlockSpec((ROW_TILE, d), lambda i: (i, 0)),
        out_shape=jax.ShapeDtypeStruct((n, d), F32),
        compiler_params=_params("parallel"),
        name="input_ln",
    )(x, g, b)


N_ROW_STATES = 5
N_OPERANDS = 11


def _pre_kernel(x_ref, w_ref, bf_ref, gq_ref, wq_ref, gkv_ref, wkv_ref, bg_ref, cos_ref, sin_ref, *rest,
                fox_scale, mla_scale, n_prompt_tiles):
    outs = rest[len(rest) - N_OPERANDS - 2 * N_ROW_STATES:]
    (qft_ref, kf_ref, vft_ref, logf_ref, qnt_ref, qrt_ref, kr_ref, kn_ref, vmt_ref, gate_ref,
     norm_ref) = outs[:N_OPERANDS]
    prompt_refs = outs[N_OPERANDS:N_OPERANDS + N_ROW_STATES]
    sample_refs = outs[N_OPERANDS + N_ROW_STATES:]
    xb = x_ref[...].astype(BF16)
    cos = cos_ref[...]
    sin = sin_ref[...]

    qkv = _dot(xb, w_ref[:, C_QKV:C_CQ])
    q = qkv[:, 0:FOX_WIDTH] * fox_scale
    qft_ref[...] = q.T.astype(BF16)
    k = qkv[:, FOX_WIDTH:2 * FOX_WIDTH]
    v = qkv[:, 2 * FOX_WIDTH:3 * FOX_WIDTH]
    kf_ref[...] = k.astype(BF16)
    vft_ref[...] = v.T.astype(BF16)

    feat = lax.broadcasted_iota(jnp.int32, (FOX_WIDTH, LANES), 0)
    head = lax.broadcasted_iota(jnp.int32, (FOX_WIDTH, LANES), 1)
    head_of = jnp.where(feat // FOX_HEAD_DIM == head, 1.0, 0.0).astype(BF16)
    top = lambda a: jnp.sqrt(jnp.max(_dot((a * a).astype(BF16), head_of), axis=0, keepdims=True))
    norm_ref[...] = jnp.broadcast_to(jnp.concatenate([top(q), top(k)], axis=1), norm_ref.shape)

    z = _dot(xb, w_ref[:, C_LOGF:C_KR]) + bf_ref[...]
    logf = jnp.minimum(z, 0.0) - jnp.log1p(jnp.exp(-jnp.abs(z)))
    logf_ref[...] = logf

    cq = _rms_norm(_dot(xb, w_ref[:, C_CQ:C_CKV]), gq_ref[...]).astype(BF16)
    qm = _dot(cq, wq_ref[...])
    qnt_ref[...] = (qm[:, 0:512] * mla_scale).T.astype(BF16)
    cos2 = jnp.concatenate([cos, cos], axis=1)
    sin2 = jnp.concatenate([sin, sin], axis=1)
    qrt_ref[...] = ((qm[:, 512:768] * cos2 + qm[:, 768:1024] * sin2) * mla_scale).T.astype(BF16)

    ckv = _rms_norm(_dot(xb, w_ref[:, C_CKV:C_LOGF]), gkv_ref[...])
    kv = _dot(ckv.astype(BF16), wkv_ref[...])
    kn_ref[...] = kv[:, 0:512].astype(BF16)
    vmt_ref[...] = kv[:, 512:1024].T.astype(BF16)
    kr = _dot(xb, w_ref[:, C_KR:C_KRS]) * cos + _dot(xb, w_ref[:, C_KRS:C_GATE]) * sin
    kr_ref[...] = kr.astype(BF16)

    g = _dot(xb, w_ref[:, C_GATE:]) + bg_ref[...]
    gate_ref[...] = jax.nn.sigmoid(g).astype(BF16)

    states = (k, v, logf[:, :FOX_HEADS], ckv, kr[:, :MLA_ROPE_DIM])
    i = pl.program_id(0)

    @pl.when(i < n_prompt_tiles)
    def _():
        for ref, val in zip(prompt_refs, states):
            ref[...] = val

    @pl.when(i >= n_prompt_tiles)
    def _():
        for ref, val in zip(sample_refs, states):
            ref[...] = val.reshape(ref.shape)


def _pre_attention(x, w, bf, gq, wq, gkv, wkv, bg, cos, sin, *, layer, depth, prompt_dims, sample_dims, states):
    n, d = x.shape
    tm = ROW_TILE
    bp, tp = prompt_dims
    bs, tn = sample_dims
    tiles_per_stream = tp // tm
    n_prompt_tiles = bp * tiles_per_stream
    streams_per_tile = tm // tn
    row = lambda width: pl.BlockSpec((tm, width), lambda i: (i, 0))
    col = lambda height: pl.BlockSpec((height, tm), lambda i: (0, i))
    full = lambda a: pl.BlockSpec(a.shape, lambda i: (0,) * a.ndim)
    outs = [
        ("qft", 512, BF16, True), ("kf", 512, BF16, False), ("vft", 512, BF16, True), ("logf", LANES, F32, False),
        ("qnt", 512, BF16, True), ("qrt", 256, BF16, True), ("kr", LANES, BF16, False), ("kn", 512, BF16, False),
        ("vmt", 512, BF16, True), ("gate", 2 * d, BF16, False),
    ]
    state_widths = (FOX_WIDTH, FOX_WIDTH, FOX_HEADS, 128, MLA_ROPE_DIM)

    def prompt_block(i):
        ip = jnp.minimum(i, n_prompt_tiles - 1)
        return layer, ip // tiles_per_stream, ip % tiles_per_stream, 0

    def sample_block(i):
        return layer, jnp.maximum(i - n_prompt_tiles, 0), 0, 0

    out_specs = [col(wd) if tr else row(wd) for _, wd, _, tr in outs]
    out_shape = [jax.ShapeDtypeStruct((wd, n) if tr else (n, wd), dt) for _, wd, dt, tr in outs]
    outs.append(("norm", 2 * LANES, F32, False))
    out_specs.append(pl.BlockSpec((8, 2 * LANES), lambda i: (i, 0)))
    out_shape.append(jax.ShapeDtypeStruct((8 * (n // tm), 2 * LANES), F32))
    assert len(outs) == N_OPERANDS
    out_specs += [pl.BlockSpec((None, None, tm, wd), prompt_block) for wd in state_widths]
    out_specs += [pl.BlockSpec((None, streams_per_tile, tn, wd), sample_block) for wd in state_widths]
    out_shape += [jax.ShapeDtypeStruct((depth, bp, tp, wd), F32) for wd in state_widths]
    out_shape += [jax.ShapeDtypeStruct((depth, bs, tn, wd), F32) for wd in state_widths]
    args = [x, w, bf, gq, wq, gkv, wkv, bg, cos, sin]
    in_specs = [row(d), full(w), full(bf), full(gq), full(wq), full(gkv), full(wkv), full(bg), row(LANES), row(LANES)]
    aliases = {len(args) + j: len(outs) + j for j in range(len(states))}
    args += list(states)
    in_specs += [pl.BlockSpec(memory_space=pl.ANY)] * len(states)
    res = pl.pallas_call(
        functools.partial(
            _pre_kernel, fox_scale=FOX_HEAD_DIM ** -0.5 * LOG2E, mla_scale=MLA_QK_DIM ** -0.5 * LOG2E,
            n_prompt_tiles=n_prompt_tiles,
        ),
        grid=(n // tm,),
        in_specs=in_specs,
        out_specs=out_specs,
        out_shape=out_shape,
        input_output_aliases=aliases,
        compiler_params=_params("arbitrary"),
        name="pre_attention",
    )(*args)
    return {name: r for (name, _, _, _), r in zip(outs, res)}, tuple(res[len(outs):])


def _kvup_kernel(c_ref, w_ref, kn_ref, vmt_ref):
    kv = _dot(c_ref[0].astype(BF16), w_ref[0])
    kn_ref[0] = kv[:, 0:512].astype(BF16)
    vmt_ref[0] = kv[:, 512:1024].T.astype(BF16)


def _kv_up(ckv, wkv):
    depth, n, r = ckv.shape
    tm = ROW_TILE
    return pl.pallas_call(
        _kvup_kernel,
        grid=(depth, n // tm),
        in_specs=[pl.BlockSpec((1, tm, r), lambda l, i: (l, i, 0)), pl.BlockSpec((1, r, 1024), lambda l, i: (l, 0, 0))],
        out_specs=[pl.BlockSpec((1, tm, 512), lambda l, i: (l, i, 0)), pl.BlockSpec((1, 512, tm), lambda l, i: (l, 0, i))],
        out_shape=[jax.ShapeDtypeStruct((depth, n, 512), BF16), jax.ShapeDtypeStruct((depth, 512, n), BF16)],
        compiler_params=_params("parallel", "parallel"),
        name="kv_up_cached",
    )(ckv, wkv)


def _bias_layout():
    sel_q = np.zeros((3 * LANES, 4 * LANES), np.float32)
    sel_k = np.zeros((3 * LANES, 4 * LANES), np.float32)
    one_q = np.zeros((1, 4 * LANES), np.float32)
    one_k = np.zeros((1, 4 * LANES), np.float32)
    for h in range(FOX_HEADS):
        base = (h // 2) * LANES + (h % 2) * AUG_GROUP
        for piece in range(3):
            sel_q[piece * LANES + h, base + piece] = 1.0
            sel_k[piece * LANES + h, base + 3 + piece] = -1.0
            one_q[0, base + 3 + piece] = 1.0
            one_k[0, base + piece] = 1.0
    return (jnp.asarray(sel_q, BF16), jnp.asarray(one_q, F32), jnp.asarray(sel_k, BF16), jnp.asarray(one_k, F32))


def _cumsum_kernel(x_ref, sq_ref, oq_ref, sk_ref, ok_ref, qxt_ref, kx_ref, carry_ref, *, tt):
    @pl.when(pl.program_id(1) == 0)
    def _():
        carry_ref[...] = jnp.zeros_like(carry_ref)

    r = lax.broadcasted_iota(jnp.int32, (tt, tt), 0)
    c = lax.broadcasted_iota(jnp.int32, (tt, tt), 1)
    tri = jnp.where(c <= r, 1.0, 0.0).astype(BF16)
    hi, mid, lo = _split3(x_ref[...])
    cum = _dot(tri, lo) + _dot(tri, mid) + _dot(tri, hi) + carry_ref[...]
    carry_ref[...] = cum[tt - 1:tt, :]
    pieces = jnp.concatenate(_split3(cum * LOG2E), axis=1)
    kx = (_dot(pieces, sk_ref[...]) + ok_ref[...]).astype(BF16)
    qx = _dot(pieces, sq_ref[...]) + oq_ref[...]
    for p in range(FOX_HEADS // 2):
        kx_ref[p] = kx[:, p * LANES:(p + 1) * LANES]
        qxt_ref[p] = qx[:, p * LANES:(p + 1) * LANES].T.astype(BF16)


def _forget_bias_operands(logf, n_seq, t, tt):
    w = logf.shape[-1]
    tiles = t // tt
    consts = _bias_layout()
    pairs = FOX_HEADS // 2
    return pl.pallas_call(
        functools.partial(_cumsum_kernel, tt=tt),
        grid=(n_seq, tiles),
        in_specs=[pl.BlockSpec((tt, w), lambda b, i: (b * tiles + i, 0))]
        + [pl.BlockSpec(c.shape, lambda b, i: (0, 0)) for c in consts],
        out_specs=[
            pl.BlockSpec((None, pairs, LANES, tt), lambda b, i: (b, 0, 0, i)),
            pl.BlockSpec((None, pairs, tt, LANES), lambda b, i: (b, 0, i, 0)),
        ],
        out_shape=[
            jax.ShapeDtypeStruct((n_seq, pairs, LANES, t), BF16),
            jax.ShapeDtypeStruct((n_seq, pairs, t, LANES), BF16),
        ],
        scratch_shapes=[pltpu.VMEM((1, w), F32)],
        compiler_params=_params("parallel", "arbitrary"),
        name="cumsum_time",
    )(logf, *consts)


def _chunk_end(pos, n_meta):
    body = n_meta + ((pos - n_meta) // CHUNK) * CHUNK + CHUNK - 1
    if n_meta == 0:
        return body
    return jnp.where(pos < n_meta, n_meta - 1, body)


def _attn_kernel(first_ref, qt_ref, qxt_ref, k_ref, kx_ref, vt_ref, *rest, fox, tq, tk, q_off, n_meta, n_ktiles):
    o_ref, *slots = rest[len(rest) - ATTN_SLOTS - 1:]
    p = pl.program_id(1)
    i = pl.program_id(2)
    q_lo = q_off + i * tq
    q_hi = q_lo + tq - 1
    if fox:
        vis_lo, vis_hi = q_lo, q_hi
    else:
        vis_lo, vis_hi = _chunk_end(q_lo, n_meta), _chunk_end(q_hi, n_meta)
    n_tot = jnp.minimum(vis_hi // tk + 1, n_ktiles)
    n_full = jnp.minimum((vis_lo + 1) // tk, n_tot - 1)
    first = jnp.clip(first_ref[pl.program_id(0), p, i], 0, n_full)
    n_tot = n_tot - first
    n_full = n_full - first

    q_pos = q_lo + lax.broadcasted_iota(jnp.int32, (1, tq), 1)
    row = lax.broadcasted_iota(jnp.int32, (LANES, 1), 0)
    qt2 = qt_ref[...]
    qxt2 = qxt_ref[...]
    cols = []
    for hh in range(2):
        x_group = row // AUG_GROUP == hh if fox else row // MLA_ROPE_DIM == 2 * (p % 2) + hh
        cols.append(jnp.concatenate([
            jnp.where(row // FOX_HEAD_DIM == hh, qt2, jnp.zeros_like(qt2)),
            jnp.where(x_group, qxt2, jnp.zeros_like(qxt2)),
        ], axis=0))
    qcbt = jnp.concatenate(cols, axis=1)
    last_visible = q_pos if fox else _chunk_end(q_pos, n_meta)
    last_visible = jnp.concatenate([last_visible, last_visible], axis=1)
    key_iota = lax.broadcasted_iota(jnp.int32, (tk, 2 * tq), 0)
    ones = jnp.ones((ONES_ROWS, tk), BF16)
    n_slots = len(slots)

    def key_rows(j):
        return pl.ds(pl.multiple_of(jnp.minimum(first + j, n_ktiles - 1) * tk, tk), tk)

    def produce(slot_refs, j, masked):
        kcs = []
        for r in range(len(slot_refs)):
            ks = key_rows(j + r)
            kcs.append(jnp.concatenate([k_ref[ks, :], kx_ref[ks, :]], axis=1))
        st_all = _dot(jnp.concatenate(kcs, axis=0), qcbt)
        cmax = []
        for r, slot_ref in enumerate(slot_refs):
            st = st_all[r * tk:(r + 1) * tk]
            if masked:
                st = jnp.where(key_iota <= last_visible - (first + j + r) * tk, st, MASK_VALUE)
            slot_ref[...] = st
            cmax.append(jnp.max(st, axis=0, keepdims=True))
        return cmax

    def accumulate(j, pt, alpha, accs):
        ks = key_rows(j)
        new = []
        for hh in range(2):
            va = jnp.concatenate([vt_ref[FOX_HEAD_DIM * hh:FOX_HEAD_DIM * (hh + 1), ks], ones], axis=0)
            sl = slice(hh * tq, (hh + 1) * tq)
            new.append(alpha[:, sl] * accs[hh] + _dot(va, pt[:, sl]))
        return tuple(new)

    def step(u, carry, mode):
        m, cmax, accs = carry
        cmax = list(cmax)
        for r0 in range(0, n_slots, QK_GROUP):
            group = range(r0, r0 + QK_GROUP)
            pts, alphas = [], []
            for r in group:
                m_new = jnp.maximum(m, cmax[r])
                pts.append(jnp.exp2(slots[r][...] - m_new).astype(BF16))
                alphas.append(jnp.exp2(m - m_new))
                m = m_new
            if mode != "drain":
                cmax[r0:r0 + QK_GROUP] = produce([slots[r] for r in group], n_slots * (u + 1) + r0, mode == "masked")
            for r, pt, alpha in zip(group, pts, alphas):
                accs = accumulate(n_slots * u + r, pt, alpha, accs)
        return m, tuple(cmax), accs

    def two_steps(v, carry):
        return step(2 * v + 1, step(2 * v, carry, "plain"), "plain")

    m0 = jnp.full((1, 2 * tq), MASK_VALUE, F32)
    a0 = jnp.zeros((FOX_HEAD_DIM + ONES_ROWS, tq), F32)
    carry = (m0, tuple(produce(list(slots), 0, True)), (a0, a0))
    n_steps = (n_tot + n_slots - 1) // n_slots
    n_plain = jnp.maximum(n_full - n_slots, 0) // n_slots
    carry = lax.fori_loop(0, n_plain // 2, two_steps, carry)
    carry = lax.fori_loop(n_plain - n_plain % 2, n_plain, functools.partial(step, mode="plain"), carry)
    carry = lax.fori_loop(n_plain, n_steps - 1, functools.partial(step, mode="masked"), carry)
    _, _, accs = step(n_steps - 1, carry, "drain")
    out_t = jnp.concatenate(
        [acc[0:FOX_HEAD_DIM] / acc[FOX_HEAD_DIM:FOX_HEAD_DIM + 1] for acc in accs], axis=0
    )
    o_ref[...] = out_t.T[:o_ref.shape[0]].astype(o_ref.dtype)


def _attention(qt, qxt, k, kx, vt, *, fox, n_batch, t_q, t_k, tq, tk, q_off, n_meta, out_rows, out_row0=0, out=None,
               first=None):
    nq = t_q // tq

    def time_major(a, rows_of, t, tile):
        tiles = t // tile
        if a.ndim == 2:
            return pl.BlockSpec((LANES, tile), lambda bb, p, i, _: (rows_of(p), bb * tiles + i % tiles))
        return pl.BlockSpec((None, LANES, tile), lambda bb, p, i, _: (bb, rows_of(p), i % tiles))

    def row_major(a, col_of):
        if a.ndim == 2:
            return pl.BlockSpec((t_k, LANES), lambda bb, p, i, _: (bb, col_of(p)))
        return pl.BlockSpec((None, t_k, LANES), lambda bb, p, i, _: (bb, 0, col_of(p)))

    qspec = time_major(qt, lambda p: p, t_q, tq)
    kspec = row_major(k, lambda p: p)
    if fox:
        qxspec = pl.BlockSpec((None, None, LANES, tq), lambda bb, p, i, _: (bb, p, 0, i))
        kxspec = pl.BlockSpec((None, None, t_k, LANES), lambda bb, p, i, _: (bb, p, 0, 0))
    else:
        qxspec = time_major(qxt, lambda p: p // 2, t_q, tq)
        kxspec = row_major(kx, lambda p: 0)
    vspec = time_major(vt, lambda p: p, t_k, t_k)
    if first is None:
        first = jnp.zeros((n_batch, 4, nq), jnp.int32)
    args = [first, qt, qxt, k, kx, vt]
    in_specs = [qspec, qxspec, kspec, kxspec, vspec]
    if isinstance(out, int):
        n_rows, aliases = out, {}
    else:
        n_rows, aliases = out.shape[0], {len(args): 0}
        args.append(out)
        in_specs.append(pl.BlockSpec(memory_space=pl.ANY))
    block0 = out_row0 // out_rows
    return pl.pallas_call(
        functools.partial(_attn_kernel, fox=fox, tq=tq, tk=tk, q_off=q_off, n_meta=n_meta, n_ktiles=t_k // tk),
        grid_spec=pltpu.PrefetchScalarGridSpec(
            num_scalar_prefetch=1,
            grid=(n_batch, 4, nq),
            in_specs=in_specs,
            out_specs=pl.BlockSpec((out_rows, LANES), lambda bb, p, i, _: (block0 + bb * nq + i, p)),
            scratch_shapes=[pltpu.VMEM((tk, 2 * tq), F32)] * ATTN_SLOTS,
        ),
        out_shape=jax.ShapeDtypeStruct((n_rows, 512), BF16),
        input_output_aliases=aliases,
        compiler_params=_params("parallel", "parallel", "arbitrary"),
        name="fox_attention" if fox else "mla_attention",
    )(*args)


def _first_live_key_tile(norm, logf, n_seq, t, tile):
    n = n_seq * t
    tiles = t // tile
    h = FOX_HEADS
    per_tile = norm[0:8 * n_seq * tiles:8].reshape(n_seq, tiles, 2 * LANES) * NORM_SLACK
    qmax = per_tile[..., :h]
    kmax = per_tile[..., LANES:LANES + h]
    cum2 = jnp.cumsum(logf[:n, :h].reshape(n_seq, t, h), axis=1) * LOG2E
    cum_q = cum2[:, 0::tile]
    cum_k = cum2[:, tile - 1::tile]
    bound = qmax[:, :, None] * lax.cummax(kmax, axis=1)[:, None] + cum_q[:, :, None] - cum_k[:, None]
    own = -qmax * kmax
    dead = bound < own[:, :, None] - UNDERFLOW_MARGIN
    dead = jnp.all(dead.reshape(n_seq, tiles, tiles, h // 2, 2), axis=-1)
    first = jnp.sum(jnp.cumprod(dead.astype(jnp.int32), axis=2), axis=2)
    return jnp.transpose(first, (0, 2, 1))


def _group_partner(x, lane, d, width):
    fwd = pltpu.roll(x, LANES - d, 1)
    back = pltpu.roll(x, width - d, 1)
    return jnp.where(lane % width + d < width, fwd, back)


def _route(logits, b_router, lane):
    scores = jax.nn.sigmoid(logits)
    sel = scores + b_router
    e = EXPERTS_PER_GROUP
    rank = jnp.zeros(sel.shape, jnp.int32)
    for d in range(1, e):
        other = _group_partner(sel, lane, d, e)
        other_first = lane % e + d >= e
        beats = (other > sel) | ((other == sel) & other_first)
        rank = rank + beats.astype(jnp.int32)
    top2 = rank < 2
    kept = jnp.where(top2, sel, 0.0)
    gscore = kept
    for d in range(1, e):
        gscore = gscore + _group_partner(kept, lane, d, e)
    grank = jnp.zeros_like(rank)
    for gstep in range(1, N_GROUPS):
        d = gstep * e
        other = _group_partner(gscore, lane, d, N_EXPERTS)
        other_first = lane % N_EXPERTS + d >= N_EXPERTS
        beats = (other > gscore) | ((other == gscore) & other_first)
        grank = grank + beats.astype(jnp.int32)
    chosen = top2 & (grank == 0) & (lane < N_EXPERTS)
    w = jnp.where(chosen, scores, 0.0)
    return w / jnp.sum(w, axis=1, keepdims=True)


def _post_kernel(x_ref, fo_ref, mo_ref, gate_ref, pa_ref, pb_ref, wo_ref, g1_ref, b1_ref, wr_ref, br_ref,
                 x1_ref, gates_ref, *, alpha):
    d = x_ref.shape[1]
    merged = gate_ref[:, 0:d].astype(F32) * _dot(fo_ref[...], pa_ref[...])
    merged = merged + gate_ref[:, d:2 * d].astype(F32) * _dot(mo_ref[...], pb_ref[...])
    mix = _dot(merged.astype(BF16), wo_ref[...])
    x1 = _layer_norm(alpha * x_ref[...] + mix, g1_ref[...], b1_ref[...])
    x1_ref[...] = x1
    hi, mid, lo = _split3(x1)
    a = _dot(hi, wr_ref[:, 0:2 * LANES])
    b = _dot(mid, wr_ref[:, 0:2 * LANES])
    logits = _dot(lo, wr_ref[:, 0:LANES]) + _dot(hi, wr_ref[:, 2 * LANES:3 * LANES])
    logits = logits + b[:, LANES:] + a[:, LANES:] + b[:, :LANES] + a[:, :LANES]
    lane = lax.broadcasted_iota(jnp.int32, logits.shape, 1)
    gates_ref[...] = _route(logits, br_ref[...], lane)


def _post_attention(x, fo, mo, gate, pa, pb, wo, g1, b1, wr3, br, alpha):
    n, d = x.shape
    tm = ROW_TILE
    row = lambda width: pl.BlockSpec((tm, width), lambda i: (i, 0))
    full = lambda a: pl.BlockSpec(a.shape, lambda i: (0,) * a.ndim)
    return pl.pallas_call(
        functools.partial(_post_kernel, alpha=alpha),
        grid=(n // tm,),
        in_specs=[row(d), row(512), row(512), row(2 * d), full(pa), full(pb), full(wo), full(g1), full(b1), full(wr3), full(br)],
        out_specs=[row(d), row(LANES)],
        out_shape=[jax.ShapeDtypeStruct((n, d), F32), jax.ShapeDtypeStruct((n, LANES), F32)],
        compiler_params=_params("parallel"),
        name="post_attention",
    )(x, fo, mo, gate, pa, pb, wo, g1, b1, wr3, br)


def _moe_kernel(x_ref, gates_ref, wg_ref, wu_ref, wd_ref, g2_ref, b2_ref, o_ref, xb_ref, acc_ref, *, alpha):
    e = pl.program_id(1)

    @pl.when(e == 0)
    def _():
        xb_ref[...] = x_ref[...].astype(BF16)
        acc_ref[...] = jnp.zeros_like(acc_ref)

    lane = lax.broadcasted_iota(jnp.int32, (1, LANES), 1)
    gate = jnp.sum(jnp.where(lane == e, gates_ref[...], 0.0), axis=1, keepdims=True)
    xb = xb_ref[...]
    hid = jax.nn.silu(_dot(xb, wg_ref[0])) * _dot(xb, wu_ref[0])
    acc_ref[...] += _dot((hid * gate).astype(BF16), wd_ref[0])

    @pl.when(e == pl.num_programs(1) - 1)
    def _():
        o_ref[...] = _layer_norm(alpha * x_ref[...] + acc_ref[...], g2_ref[...], b2_ref[...])


def _moe(x, gates, wg, wu, wd, g2, b2, alpha):
    n, d = x.shape
    n_exp, _, f = wg.shape
    tm = MOE_ROW_TILE if n % MOE_ROW_TILE == 0 else ROW_TILE
    return pl.pallas_call(
        functools.partial(_moe_kernel, alpha=alpha),
        grid=(n // tm, n_exp),
        in_specs=[
            pl.BlockSpec((tm, d), lambda i, e: (i, 0)),
            pl.BlockSpec((tm, LANES), lambda i, e: (i, 0)),
            pl.BlockSpec((1, d, f), lambda i, e: (e, 0, 0)),
            pl.BlockSpec((1, d, f), lambda i, e: (e, 0, 0)),
            pl.BlockSpec((1, f, d), lambda i, e: (e, 0, 0)),
            pl.BlockSpec((1, d), lambda i, e: (0, 0)),
            pl.BlockSpec((1, d), lambda i, e: (0, 0)),
        ],
        out_specs=pl.BlockSpec((tm, d), lambda i, e: (i, 0)),
        out_shape=jax.ShapeDtypeStruct((n, d), F32),
        scratch_shapes=[pltpu.VMEM((tm, d), BF16), pltpu.VMEM((tm, d), F32)],
        compiler_params=_params("parallel", "arbitrary"),
        name="moe_ln2",
    )(x, gates, wg, wu, wd, g2, b2)


def _rope_tables(pos):
    half = MLA_ROPE_DIM // 2
    inv_freq = ROPE_THETA ** (-jnp.arange(half, dtype=F32) / half)
    ang = pos.astype(F32)[:, None] * inv_freq[None, :]
    cos, sin = jnp.cos(ang), jnp.sin(ang)
    cos32 = jnp.concatenate([cos, cos], axis=1)
    sin32 = jnp.concatenate([-sin, sin], axis=1)
    return jnp.tile(cos32, (1, LANES // MLA_ROPE_DIM)), jnp.tile(sin32, (1, LANES // MLA_ROPE_DIM))


def _swap_halves(w):
    half = MLA_ROPE_DIM // 2
    return jnp.concatenate([w[..., half:], w[..., :half]], axis=-1)


def _prepare_w_in(w_in):
    depth, d, _ = w_in.shape
    o = [0]
    for sz in (FOX_WIDTH, FOX_WIDTH, FOX_WIDTH, FOX_HEADS, 256, 128, MLA_ROPE_DIM, 2 * d):
        o.append(o[-1] + sz)
    qkv = w_in[..., o[0]:o[3]]
    wf = w_in[..., o[3]:o[4]]
    wcq = w_in[..., o[4]:o[5]]
    wckv = w_in[..., o[5]:o[6]]
    wkr = w_in[..., o[6]:o[7]]
    wg = w_in[..., o[7]:o[8]]
    reps = LANES // MLA_ROPE_DIM
    wf_pad = jnp.concatenate([wf, jnp.zeros((depth, d, LANES - FOX_HEADS), w_in.dtype)], axis=-1)
    return jnp.concatenate(
        [qkv, wcq, wckv, wf_pad, jnp.tile(wkr, (1, 1, reps)), jnp.tile(_swap_halves(wkr), (1, 1, reps)), wg], axis=-1
    ).astype(BF16)


def _prepare_w_q_up(w):
    depth, r, _ = w.shape
    w = w.reshape(depth, r, MLA_HEADS, MLA_QK_DIM)
    nope = w[..., :MLA_NOPE_DIM].reshape(depth, r, MLA_HEADS * MLA_NOPE_DIM)
    rope = w[..., MLA_NOPE_DIM:]
    rope_sw = _swap_halves(rope).reshape(depth, r, MLA_HEADS * MLA_ROPE_DIM)
    rope = rope.reshape(depth, r, MLA_HEADS * MLA_ROPE_DIM)
    return jnp.concatenate([nope, rope, rope_sw], axis=-1).astype(BF16)


def _prepare_w_kv_up(w):
    depth, r, _ = w.shape
    w = w.reshape(depth, r, MLA_HEADS, MLA_NOPE_DIM + MLA_V_DIM)
    kn = w[..., :MLA_NOPE_DIM].reshape(depth, r, MLA_HEADS * MLA_NOPE_DIM)
    vm = w[..., MLA_NOPE_DIM:].reshape(depth, r, MLA_HEADS * MLA_V_DIM)
    return jnp.concatenate([kn, vm], axis=-1).astype(BF16)


def _pad_lanes(x, width=LANES):
    return jnp.pad(x, [(0, 0)] * (x.ndim - 1) + [(0, width - x.shape[-1])])


def kernel(x_prompt, x_sample, cache_fox_k, cache_fox_v, cache_fox_logf, cache_mla_ckv, cache_mla_krope, meta_tokens, ln_in_g, ln_in_b, w_in, b_fgt, g_q_norm, w_q_up, g_kv_norm, w_kv_up, w_proj_fox, w_proj_mla, b_gate, w_out, ln1_g, ln1_b, ln2_g, ln2_b, w_router, b_router, w_exp_gate, w_exp_up, w_exp_down):
    bp, seq, d = x_prompt.shape
    bs, tn, _ = x_sample.shape
    depth = w_in.shape[0]
    past = cache_fox_k.shape[2]
    alpha = (2 * depth) ** 0.25
    t_real = N_META + seq
    tp = -(-t_real // ATTN_Q_TILE) * ATTN_Q_TILE
    n_p = bp * tp
    n_s = bs * tn
    n = n_p + n_s
    n_pad = n
    tk_s = -(-(past + tn) // SAMPLE_KEY_TILE) * SAMPLE_KEY_TILE
    tq_s = SAMPLE_QUERY_TILE
    assert n % ROW_TILE == 0 and ROW_TILE % tn == 0 and tn <= tq_s, (n, tn)
    assert ATTN_Q_TILE == ATTN_TILE == ROW_TILE

    meta = jnp.broadcast_to(meta_tokens[None].astype(F32), (bp, N_META, d))
    xp = jnp.concatenate([meta, x_prompt, jnp.zeros((bp, tp - t_real, d), F32)], axis=1)
    x = jnp.concatenate([xp.reshape(n_p, d), x_sample.reshape(n_s, d), jnp.zeros((n_pad - n, d), F32)], axis=0)
    pos = jnp.concatenate([
        jnp.tile(jnp.arange(tp, dtype=jnp.int32), bp),
        jnp.tile(past + jnp.arange(tn, dtype=jnp.int32), bs),
        jnp.zeros((n_pad - n,), jnp.int32),
    ])
    cos, sin = _rope_tables(pos)

    w_in_r = _prepare_w_in(w_in)
    w_q_r = _prepare_w_q_up(w_q_up)
    w_kv_r = _prepare_w_kv_up(w_kv_up)
    bf_r = _pad_lanes(b_fgt)[:, None, :]
    pa, pb, wo = w_proj_fox.astype(BF16), w_proj_mla.astype(BF16), w_out.astype(BF16)
    wg, wu, wd = w_exp_gate.astype(BF16), w_exp_up.astype(BF16), w_exp_down.astype(BF16)
    wr = _pad_lanes(w_router)
    keep_bf16_bits = lambda a: lax.bitcast_convert_type(
        lax.bitcast_convert_type(a, jnp.uint32) & jnp.uint32(0xFFFF0000), F32
    )
    wr_hi = keep_bf16_bits(wr)
    wr_mid = keep_bf16_bits(wr - wr_hi)
    wr_lo = wr - wr_hi - wr_mid
    wr3 = jnp.concatenate([wr_hi, wr_mid, wr_lo], axis=1).astype(BF16)
    br = _pad_lanes(b_router[None, :])

    zpad = tk_s - past - tn
    kn_c, vmt_c = _kv_up(cache_mla_ckv.reshape(depth, bs * past, -1), w_kv_r)
    kn_c = kn_c.reshape(depth, bs, past, 512)
    vmt_c = jnp.transpose(vmt_c.reshape(depth, 512, bs, past), (0, 2, 1, 3))
    kf_c = cache_fox_k.reshape(depth, bs, past, FOX_WIDTH).astype(BF16)
    vft_c = jnp.transpose(cache_fox_v.reshape(depth, bs, past, FOX_WIDTH), (0, 1, 3, 2)).astype(BF16)
    kr_c = jnp.tile(cache_mla_krope, (1, 1, 1, LANES // MLA_ROPE_DIM)).astype(BF16)
    logf_c = _pad_lanes(cache_fox_logf)

    def with_past(cached, new):
        new = new.reshape(bs, tn, new.shape[-1])
        return jnp.concatenate([cached, new, jnp.zeros((bs, zpad, new.shape[-1]), new.dtype)], axis=1)

    def with_past_t(cached_t, new_t):
        new_t = jnp.transpose(new_t.reshape(512, bs, tn), (1, 0, 2))
        return jnp.concatenate([cached_t, new_t, jnp.zeros((bs, 512, zpad), new_t.dtype)], axis=2)

    def sample_queries_t(a_t):
        a_t = jnp.transpose(a_t[:, n_p:n].reshape(a_t.shape[0], bs, tn), (1, 0, 2))
        return jnp.pad(a_t, ((0, 0), (0, 0), (0, tq_s - tn)))

    prompt = dict(
        n_batch=bp, t_q=tp, t_k=tp, tq=ATTN_Q_TILE, tk=ATTN_TILE, q_off=0, n_meta=N_META, out_rows=ATTN_Q_TILE,
        out=n_pad,
    )
    sample = dict(
        n_batch=bs, t_q=tq_s, t_k=tk_s, tq=tq_s, tk=SAMPLE_KEY_TILE, q_off=past, n_meta=0, out_rows=tn, out_row0=n_p
    )

    x = _input_ln(x, ln_in_g[None, :], ln_in_b[None, :])
    state_widths = (FOX_WIDTH, FOX_WIDTH, FOX_HEADS, 128, MLA_ROPE_DIM)
    states = tuple(jnp.zeros((depth, bp, tp, wd), F32) for wd in state_widths)
    states += tuple(jnp.zeros((depth, bs, tn, wd), F32) for wd in state_widths)
    for l in range(depth):
        pre, states = _pre_attention(
            x, w_in_r[l], bf_r[l], g_q_norm[l][None, :], w_q_r[l], g_kv_norm[l][None, :], w_kv_r[l],
            b_gate[l][None, :], cos, sin,
            layer=l, depth=depth, prompt_dims=(bp, tp), sample_dims=(bs, tn), states=states,
        )
        qx_p, kx_p = _forget_bias_operands(pre["logf"], bp, tp, ATTN_TILE)
        first_p = _first_live_key_tile(pre["norm"], pre["logf"], bp, tp, ATTN_TILE)
        fo = _attention(pre["qft"], qx_p, pre["kf"], kx_p, pre["vft"], fox=True, first=first_p, **prompt)
        mo = _attention(pre["qnt"], pre["qrt"], pre["kn"], pre["kr"], pre["vmt"], fox=False, **prompt)

        logf_s = with_past(logf_c[l], pre["logf"][n_p:n]).reshape(bs * tk_s, LANES)
        qx_s, kx_s = _forget_bias_operands(logf_s, bs, tk_s, SAMPLE_KEY_TILE)
        qx_s = jnp.pad(qx_s[:, :, :, past:past + tn], ((0, 0), (0, 0), (0, 0), (0, tq_s - tn)))
        fo = _attention(
            sample_queries_t(pre["qft"]), qx_s, with_past(kf_c[l], pre["kf"][n_p:n]), kx_s,
            with_past_t(vft_c[l], pre["vft"][:, n_p:n]), fox=True, out=fo, **sample,
        )
        mo = _attention(
            sample_queries_t(pre["qnt"]), sample_queries_t(pre["qrt"]), with_past(kn_c[l], pre["kn"][n_p:n]),
            with_past(kr_c[l], pre["kr"][n_p:n]), with_past_t(vmt_c[l], pre["vmt"][:, n_p:n]), fox=False, out=mo,
            **sample,
        )
        x1, gates = _post_attention(
            x, fo, mo, pre["gate"], pa[l], pb[l], wo[l], ln1_g[l][None, :], ln1_b[l][None, :], wr3, br, alpha
        )
        x = _moe(x1, gates, wg[l], wu[l], wd[l], ln2_g[l][None, :], ln2_b[l][None, :], alpha)

    heads = (FOX_HEADS, FOX_HEAD_DIM)
    k_p, v_p, logf_p, ckv_p, kr_p, k_s, v_s, logf_s, ckv_s, kr_s = states
    y_prompt = x[:n_p].reshape(bp, tp, d)[:, N_META:t_real]
    y_sample = x[n_p:n].reshape(bs, tn, d)
    return (
        y_prompt,
        y_sample,
        k_p[:, :, :t_real].reshape(depth, bp, t_real, *heads),
        v_p[:, :, :t_real].reshape(depth, bp, t_real, *heads),
        logf_p[:, :, :t_real],
        ckv_p[:, :, :t_real],
        kr_p[:, :, :t_real],
        k_s.reshape(depth, bs, tn, *heads),
        v_s.reshape(depth, bs, tn, *heads),
        logf_s,
        ckv_s,
        kr_s,
    )
```

```python
import functools

import jax
import jax.numpy as jnp
import numpy as np
from jax import lax
from jax.experimental import pallas as pl
from jax.experimental.pallas import tpu as pltpu

F32 = jnp.float32
BF16 = jnp.bfloat16

CHUNK = 64
N_META = 16
FOX_HEADS = 8
FOX_HEAD_DIM = 64
FOX_WIDTH = FOX_HEADS * FOX_HEAD_DIM
MLA_HEADS = 8
MLA_NOPE_DIM = 64
MLA_ROPE_DIM = 32
MLA_QK_DIM = MLA_NOPE_DIM + MLA_ROPE_DIM
MLA_V_DIM = 64
N_EXPERTS = 16
N_GROUPS = 4
EXPERTS_PER_GROUP = N_EXPERTS // N_GROUPS
ROPE_THETA = 10000.0
LN_EPS = 1e-5
RMS_EPS = 1e-6
LOG2E = 1.4426950408889634

LANES = 128
ROW_TILE = 256
MOE_ROW_TILE = 512
ATTN_TILE = 256
ATTN_Q_TILE = 256
ATTN_SLOTS = 4
QK_GROUP = 2
SAMPLE_QUERY_TILE = 128
SAMPLE_KEY_TILE = 384
AUG_GROUP = 8
ONES_ROWS = 16
MASK_VALUE = -1e30
UNDERFLOW_MARGIN = 170.0
NORM_SLACK = 1.01
VMEM_LIMIT = 56 * 1024 * 1024

C_QKV = 0
C_CQ = 3 * FOX_WIDTH
C_CKV = C_CQ + 256
C_LOGF = C_CKV + 128
C_KR = C_LOGF + LANES
C_KRS = C_KR + LANES
C_GATE = C_KRS + LANES


def _params(*sem):
    return pltpu.CompilerParams(dimension_semantics=sem, vmem_limit_bytes=VMEM_LIMIT)


def _layer_norm(x, g, b):
    mu = jnp.mean(x, axis=-1, keepdims=True)
    xc = x - mu
    var = jnp.mean(xc * xc, axis=-1, keepdims=True)
    return xc * lax.rsqrt(var + LN_EPS) * g + b


def _rms_norm(x, g):
    return x * lax.rsqrt(jnp.mean(x * x, axis=-1, keepdims=True) + RMS_EPS) * g


def _dot(a, b):
    return jnp.dot(a, b, preferred_element_type=F32)


def _split3(x):
    hi = x.astype(BF16)
    r1 = x - hi.astype(F32)
    mid = r1.astype(BF16)
    lo = (r1 - mid.astype(F32)).astype(BF16)
    return hi, mid, lo


def _ln_kernel(x_ref, g_ref, b_ref, o_ref):
    o_ref[...] = _layer_norm(x_ref[...], g_ref[...], b_ref[...])


def _input_ln(x, g, b):
    n, d = x.shape
    return pl.pallas_call(
        _ln_kernel,
        grid=(n // ROW_TILE,),
        in_specs=[
            pl.BlockSpec((ROW_TILE, d), lambda i: (i, 0)),
            pl.BlockSpec((1, d), lambda i: (0, 0)),
            pl.BlockSpec((1, d), lambda i: (0, 0)),
        ],
        out_specs=pl.BlockSpec((ROW_TILE, d), lambda i: (i, 0)),
        out_shape=jax.ShapeDtypeStruct((n, d), F32),
        compiler_params=_params("parallel"),
        name="input_ln",
    )(x, g, b)


N_ROW_STATES = 5
N_OPERANDS = 11


def _pre_kernel(x_ref, w_ref, bf_ref, gq_ref, wq_ref, gkv_ref, wkv_ref, bg_ref, cos_ref, sin_ref, *rest,
                fox_scale, mla_scale, n_prompt_tiles):
    outs = rest[len(rest) - N_OPERANDS - 2 * N_ROW_STATES:]
    (qft_ref, kf_ref, vft_ref, logf_ref, qnt_ref, qrt_ref, kr_ref, kn_ref, vmt_ref, gate_ref,
     norm_ref) = outs[:N_OPERANDS]
    prompt_refs = outs[N_OPERANDS:N_OPERANDS + N_ROW_STATES]
    sample_refs = outs[N_OPERANDS + N_ROW_STATES:]
    xb = x_ref[...].astype(BF16)
    cos = cos_ref[...]
    sin = sin_ref[...]

    qkv = _dot(xb, w_ref[:, C_QKV:C_CQ])
    q = qkv[:, 0:FOX_WIDTH] * fox_scale
    qft_ref[...] = q.T.astype(BF16)
    k = qkv[:, FOX_WIDTH:2 * FOX_WIDTH]
    v = qkv[:, 2 * FOX_WIDTH:3 * FOX_WIDTH]
    kf_ref[...] = k.astype(BF16)
    vft_ref[...] = v.T.astype(BF16)

    feat = lax.broadcasted_iota(jnp.int32, (FOX_WIDTH, LANES), 0)
    head = lax.broadcasted_iota(jnp.int32, (FOX_WIDTH, LANES), 1)
    head_of = jnp.where(feat // FOX_HEAD_DIM == head, 1.0, 0.0).astype(BF16)
    top = lambda a: jnp.sqrt(jnp.max(_dot((a * a).astype(BF16), head_of), axis=0, keepdims=True))
    norm_ref[...] = jnp.broadcast_to(jnp.concatenate([top(q), top(k)], axis=1), norm_ref.shape)

    z = _dot(xb, w_ref[:, C_LOGF:C_KR]) + bf_ref[...]
    logf = jnp.minimum(z, 0.0) - jnp.log1p(jnp.exp(-jnp.abs(z)))
    logf_ref[...] = logf

    cq = _rms_norm(_dot(xb, w_ref[:, C_CQ:C_CKV]), gq_ref[...]).astype(BF16)
    qm = _dot(cq, wq_ref[...])
    qnt_ref[...] = (qm[:, 0:512] * mla_scale).T.astype(BF16)
    cos2 = jnp.concatenate([cos, cos], axis=1)
    sin2 = jnp.concatenate([sin, sin], axis=1)
    qrt_ref[...] = ((qm[:, 512:768] * cos2 + qm[:, 768:1024] * sin2) * mla_scale).T.astype(BF16)

    ckv = _rms_norm(_dot(xb, w_ref[:, C_CKV:C_LOGF]), gkv_ref[...])
    kv = _dot(ckv.astype(BF16), wkv_ref[...])
    kn_ref[...] = kv[:, 0:512].astype(BF16)
    vmt_ref[...] = kv[:, 512:1024].T.astype(BF16)
    kr = _dot(xb, w_ref[:, C_KR:C_KRS]) * cos + _dot(xb, w_ref[:, C_KRS:C_GATE]) * sin
    kr_ref[...] = kr.astype(BF16)

    g = _dot(xb, w_ref[:, C_GATE:]) + bg_ref[...]
    gate_ref[...] = jax.nn.sigmoid(g).astype(BF16)

    states = (k, v, logf[:, :FOX_HEADS], ckv, kr[:, :MLA_ROPE_DIM])
    i = pl.program_id(0)

    @pl.when(i < n_prompt_tiles)
    def _():
        for ref, val in zip(prompt_refs, states):
            ref[...] = val

    @pl.when(i >= n_prompt_tiles)
    def _():
        for ref, val in zip(sample_refs, states):
            ref[...] = val.reshape(ref.shape)


def _pre_attention(x, w, bf, gq, wq, gkv, wkv, bg, cos, sin, *, layer, depth, prompt_dims, sample_dims, states):
    n, d = x.shape
    tm = ROW_TILE
    bp, tp = prompt_dims
    bs, tn = sample_dims
    tiles_per_stream = tp // tm
    n_prompt_tiles = bp * tiles_per_stream
    streams_per_tile = tm // tn
    row = lambda width: pl.BlockSpec((tm, width), lambda i: (i, 0))
    col = lambda height: pl.BlockSpec((height, tm), lambda i: (0, i))
    full = lambda a: pl.BlockSpec(a.shape, lambda i: (0,) * a.ndim)
    outs = [
        ("qft", 512, BF16, True), ("kf", 512, BF16, False), ("vft", 512, BF16, True), ("logf", LANES, F32, False),
        ("qnt", 512, BF16, True), ("qrt", 256, BF16, True), ("kr", LANES, BF16, False), ("kn", 512, BF16, False),
        ("vmt", 512, BF16, True), ("gate", 2 * d, BF16, False),
    ]
    state_widths = (FOX_WIDTH, FOX_WIDTH, FOX_HEADS, 128, MLA_ROPE_DIM)

    def prompt_block(i):
        ip = jnp.minimum(i, n_prompt_tiles - 1)
        return layer, ip // tiles_per_stream, ip % tiles_per_stream, 0

    def sample_block(i):
        return layer, jnp.maximum(i - n_prompt_tiles, 0), 0, 0

    out_specs = [col(wd) if tr else row(wd) for _, wd, _, tr in outs]
    out_shape = [jax.ShapeDtypeStruct((wd, n) if tr else (n, wd), dt) for _, wd, dt, tr in outs]
    outs.append(("norm", 2 * LANES, F32, False))
    out_specs.append(pl.BlockSpec((8, 2 * LANES), lambda i: (i, 0)))
    out_shape.append(jax.ShapeDtypeStruct((8 * (n // tm), 2 * LANES), F32))
    assert len(outs) == N_OPERANDS
    out_specs += [pl.BlockSpec((None, None, tm, wd), prompt_block) for wd in state_widths]
    out_specs += [pl.BlockSpec((None, streams_per_tile, tn, wd), sample_block) for wd in state_widths]
    out_shape += [jax.ShapeDtypeStruct((depth, bp, tp, wd), F32) for wd in state_widths]
    out_shape += [jax.ShapeDtypeStruct((depth, bs, tn, wd), F32) for wd in state_widths]
    args = [x, w, bf, gq, wq, gkv, wkv, bg, cos, sin]
    in_specs = [row(d), full(w), full(bf), full(gq), full(wq), full(gkv), full(wkv), full(bg), row(LANES), row(LANES)]
    aliases = {len(args) + j: len(outs) + j for j in range(len(states))}
    args += list(states)
    in_specs += [pl.BlockSpec(memory_space=pl.ANY)] * len(states)
    res = pl.pallas_call(
        functools.partial(
            _pre_kernel, fox_scale=FOX_HEAD_DIM ** -0.5 * LOG2E, mla_scale=MLA_QK_DIM ** -0.5 * LOG2E,
            n_prompt_tiles=n_prompt_tiles,
        ),
        grid=(n // tm,),
        in_specs=in_specs,
        out_specs=out_specs,
        out_shape=out_shape,
        input_output_aliases=aliases,
        compiler_params=_params("arbitrary"),
        name="pre_attention",
    )(*args)
    return {name: r for (name, _, _, _), r in zip(outs, res)}, tuple(res[len(outs):])


def _kvup_kernel(c_ref, w_ref, kn_ref, vmt_ref):
    kv = _dot(c_ref[0].astype(BF16), w_ref[0])
    kn_ref[0] = kv[:, 0:512].astype(BF16)
    vmt_ref[0] = kv[:, 512:1024].T.astype(BF16)


def _kv_up(ckv, wkv):
    depth, n, r = ckv.shape
    tm = ROW_TILE
    return pl.pallas_call(
        _kvup_kernel,
        grid=(depth, n // tm),
        in_specs=[pl.BlockSpec((1, tm, r), lambda l, i: (l, i, 0)), pl.BlockSpec((1, r, 1024), lambda l, i: (l, 0, 0))],
        out_specs=[pl.BlockSpec((1, tm, 512), lambda l, i: (l, i, 0)), pl.BlockSpec((1, 512, tm), lambda l, i: (l, 0, i))],
        out_shape=[jax.ShapeDtypeStruct((depth, n, 512), BF16), jax.ShapeDtypeStruct((depth, 512, n), BF16)],
        compiler_params=_params("parallel", "parallel"),
        name="kv_up_cached",
    )(ckv, wkv)


def _bias_layout():
    sel_q = np.zeros((3 * LANES, 4 * LANES), np.float32)
    sel_k = np.zeros((3 * LANES, 4 * LANES), np.float32)
    one_q = np.zeros((1, 4 * LANES), np.float32)
    one_k = np.zeros((1, 4 * LANES), np.float32)
    for h in range(FOX_HEADS):
        base = (h // 2) * LANES + (h % 2) * AUG_GROUP
        for piece in range(3):
            sel_q[piece * LANES + h, base + piece] = 1.0
            sel_k[piece * LANES + h, base + 3 + piece] = -1.0
            one_q[0, base + 3 + piece] = 1.0
            one_k[0, base + piece] = 1.0
    return (jnp.asarray(sel_q, BF16), jnp.asarray(one_q, F32), jnp.asarray(sel_k, BF16), jnp.asarray(one_k, F32))


def _cumsum_kernel(x_ref, sq_ref, oq_ref, sk_ref, ok_ref, qxt_ref, kx_ref, edge_ref, carry_ref, *, tt):
    @pl.when(pl.program_id(1) == 0)
    def _():
        carry_ref[...] = jnp.zeros_like(carry_ref)

    r = lax.broadcasted_iota(jnp.int32, (tt, tt), 0)
    c = lax.broadcasted_iota(jnp.int32, (tt, tt), 1)
    tri = jnp.where(c <= r, 1.0, 0.0).astype(BF16)
    hi, mid, lo = _split3(x_ref[...])
    cum = _dot(tri, lo) + _dot(tri, mid) + _dot(tri, hi) + carry_ref[...]
    carry_ref[...] = cum[tt - 1:tt, :]
    cum2 = cum * LOG2E
    edge_ref[...] = jnp.broadcast_to(jnp.concatenate([cum2[0:1], cum2[tt - 1:tt]], axis=1), edge_ref.shape)
    pieces = jnp.concatenate(_split3(cum2), axis=1)
    kx = (_dot(pieces, sk_ref[...]) + ok_ref[...]).astype(BF16)
    qx = _dot(pieces, sq_ref[...]) + oq_ref[...]
    for p in range(FOX_HEADS // 2):
        kx_ref[p] = kx[:, p * LANES:(p + 1) * LANES]
        qxt_ref[p] = qx[:, p * LANES:(p + 1) * LANES].T.astype(BF16)


def _forget_bias_operands(logf, n_seq, t, tt):
    w = logf.shape[-1]
    tiles = t // tt
    consts = _bias_layout()
    pairs = FOX_HEADS // 2
    return pl.pallas_call(
        functools.partial(_cumsum_kernel, tt=tt),
        grid=(n_seq, tiles),
        in_specs=[pl.BlockSpec((tt, w), lambda b, i: (b * tiles + i, 0))]
        + [pl.BlockSpec(c.shape, lambda b, i: (0, 0)) for c in consts],
        out_specs=[
            pl.BlockSpec((None, pairs, LANES, tt), lambda b, i: (b, 0, 0, i)),
            pl.BlockSpec((None, pairs, tt, LANES), lambda b, i: (b, 0, i, 0)),
            pl.BlockSpec((8, 2 * w), lambda b, i: (b * tiles + i, 0)),
        ],
        out_shape=[
            jax.ShapeDtypeStruct((n_seq, pairs, LANES, t), BF16),
            jax.ShapeDtypeStruct((n_seq, pairs, t, LANES), BF16),
            jax.ShapeDtypeStruct((8 * n_seq * tiles, 2 * w), F32),
        ],
        scratch_shapes=[pltpu.VMEM((1, w), F32)],
        compiler_params=_params("parallel", "arbitrary"),
        name="cumsum_time",
    )(logf, *consts)


def _chunk_end(pos, n_meta):
    body = n_meta + ((pos - n_meta) // CHUNK) * CHUNK + CHUNK - 1
    if n_meta == 0:
        return body
    return jnp.where(pos < n_meta, n_meta - 1, body)


def _attn_kernel(first_ref, qt_ref, qxt_ref, k_ref, kx_ref, vt_ref, *rest, fox, tq, tk, q_off, n_meta, n_ktiles):
    o_ref, *slots = rest[len(rest) - ATTN_SLOTS - 1:]
    p = pl.program_id(1)
    i = pl.program_id(2)
    q_lo = q_off + i * tq
    q_hi = q_lo + tq - 1
    if fox:
        vis_lo, vis_hi = q_lo, q_hi
    else:
        vis_lo, vis_hi = _chunk_end(q_lo, n_meta), _chunk_end(q_hi, n_meta)
    n_tot = jnp.minimum(vis_hi // tk + 1, n_ktiles)
    n_full = jnp.minimum((vis_lo + 1) // tk, n_tot - 1)
    first = jnp.clip(first_ref[pl.program_id(0), p, i], 0, n_full)
    n_tot = n_tot - first
    n_full = n_full - first

    q_pos = q_lo + lax.broadcasted_iota(jnp.int32, (1, tq), 1)
    row = lax.broadcasted_iota(jnp.int32, (LANES, 1), 0)
    qt2 = qt_ref[...]
    qxt2 = qxt_ref[...]
    cols = []
    for hh in range(2):
        x_group = row // AUG_GROUP == hh if fox else row // MLA_ROPE_DIM == 2 * (p % 2) + hh
        cols.append(jnp.concatenate([
            jnp.where(row // FOX_HEAD_DIM == hh, qt2, jnp.zeros_like(qt2)),
            jnp.where(x_group, qxt2, jnp.zeros_like(qxt2)),
        ], axis=0))
    qcbt = jnp.concatenate(cols, axis=1)
    last_visible = q_pos if fox else _chunk_end(q_pos, n_meta)
    last_visible = jnp.concatenate([last_visible, last_visible], axis=1)
    key_iota = lax.broadcasted_iota(jnp.int32, (tk, 2 * tq), 0)
    ones = jnp.ones((ONES_ROWS, tk), BF16)
    n_slots = len(slots)

    def key_rows(j):
        return pl.ds(pl.multiple_of(jnp.minimum(first + j, n_ktiles - 1) * tk, tk), tk)

    def produce(slot_refs, j, masked):
        kcs = []
        for r in range(len(slot_refs)):
            ks = key_rows(j + r)
            kcs.append(jnp.concatenate([k_ref[ks, :], kx_ref[ks, :]], axis=1))
        st_all = _dot(jnp.concatenate(kcs, axis=0), qcbt)
        cmax = []
        for r, slot_ref in enumerate(slot_refs):
            st = st_all[r * tk:(r + 1) * tk]
            if masked:
                st = jnp.where(key_iota <= last_visible - (first + j + r) * tk, st, MASK_VALUE)
            slot_ref[...] = st
            cmax.append(jnp.max(st, axis=0, keepdims=True))
        return cmax

    def accumulate(j, pt, alpha, accs):
        ks = key_rows(j)
        new = []
        for hh in range(2):
            va = jnp.concatenate([vt_ref[FOX_HEAD_DIM * hh:FOX_HEAD_DIM * (hh + 1), ks], ones], axis=0)
            sl = slice(hh * tq, (hh + 1) * tq)
            new.append(alpha[:, sl] * accs[hh] + _dot(va, pt[:, sl]))
        return tuple(new)

    def step(u, carry, mode):
        m, cmax, accs = carry
        cmax = list(cmax)
        for r0 in range(0, n_slots, QK_GROUP):
            group = range(r0, r0 + QK_GROUP)
            pts, alphas = [], []
            for r in group:
                m_new = jnp.maximum(m, cmax[r])
                pts.append(jnp.exp2(slots[r][...] - m_new).astype(BF16))
                alphas.append(jnp.exp2(m - m_new))
                m = m_new
            if mode != "drain":
                cmax[r0:r0 + QK_GROUP] = produce([slots[r] for r in group], n_slots * (u + 1) + r0, mode == "masked")
            for r, pt, alpha in zip(group, pts, alphas):
                accs = accumulate(n_slots * u + r, pt, alpha, accs)
        return m, tuple(cmax), accs

    def two_steps(v, carry):
        return step(2 * v + 1, step(2 * v, carry, "plain"), "plain")

    m0 = jnp.full((1, 2 * tq), MASK_VALUE, F32)
    a0 = jnp.zeros((FOX_HEAD_DIM + ONES_ROWS, tq), F32)
    carry = (m0, tuple(produce(list(slots), 0, True)), (a0, a0))
    n_steps = (n_tot + n_slots - 1) // n_slots
    n_plain = jnp.maximum(n_full - n_slots, 0) // n_slots
    carry = lax.fori_loop(0, n_plain // 2, two_steps, carry)
    carry = lax.fori_loop(n_plain - n_plain % 2, n_plain, functools.partial(step, mode="plain"), carry)
    carry = lax.fori_loop(n_plain, n_steps - 1, functools.partial(step, mode="masked"), carry)
    _, _, accs = step(n_steps - 1, carry, "drain")
    out_t = jnp.concatenate(
        [acc[0:FOX_HEAD_DIM] / acc[FOX_HEAD_DIM:FOX_HEAD_DIM + 1] for acc in accs], axis=0
    )
    o_ref[...] = out_t.T[:o_ref.shape[0]].astype(o_ref.dtype)


def _attention(qt, qxt, k, kx, vt, *, fox, n_batch, t_q, t_k, tq, tk, q_off, n_meta, out_rows, out_row0=0, out=None,
               first=None):
    nq = t_q // tq

    def time_major(a, rows_of, t, tile):
        tiles = t // tile
        if a.ndim == 2:
            return pl.BlockSpec((LANES, tile), lambda bb, p, i, _: (rows_of(p), bb * tiles + i % tiles))
        return pl.BlockSpec((None, LANES, tile), lambda bb, p, i, _: (bb, rows_of(p), i % tiles))

    def row_major(a, col_of):
        if a.ndim == 2:
            return pl.BlockSpec((t_k, LANES), lambda bb, p, i, _: (bb, col_of(p)))
        return pl.BlockSpec((None, t_k, LANES), lambda bb, p, i, _: (bb, 0, col_of(p)))

    qspec = time_major(qt, lambda p: p, t_q, tq)
    kspec = row_major(k, lambda p: p)
    if fox:
        qxspec = pl.BlockSpec((None, None, LANES, tq), lambda bb, p, i, _: (bb, p, 0, i))
        kxspec = pl.BlockSpec((None, None, t_k, LANES), lambda bb, p, i, _: (bb, p, 0, 0))
    else:
        qxspec = time_major(qxt, lambda p: p // 2, t_q, tq)
        kxspec = row_major(kx, lambda p: 0)
    vspec = time_major(vt, lambda p: p, t_k, t_k)
    if first is None:
        first = jnp.zeros((n_batch, 4, nq), jnp.int32)
    args = [first, qt, qxt, k, kx, vt]
    in_specs = [qspec, qxspec, kspec, kxspec, vspec]
    if isinstance(out, int):
        n_rows, aliases = out, {}
    else:
        n_rows, aliases = out.shape[0], {len(args): 0}
        args.append(out)
        in_specs.append(pl.BlockSpec(memory_space=pl.ANY))
    block0 = out_row0 // out_rows
    return pl.pallas_call(
        functools.partial(_attn_kernel, fox=fox, tq=tq, tk=tk, q_off=q_off, n_meta=n_meta, n_ktiles=t_k // tk),
        grid_spec=pltpu.PrefetchScalarGridSpec(
            num_scalar_prefetch=1,
            grid=(n_batch, 4, nq),
            in_specs=in_specs,
            out_specs=pl.BlockSpec((out_rows, LANES), lambda bb, p, i, _: (block0 + bb * nq + i, p)),
            scratch_shapes=[pltpu.VMEM((tk, 2 * tq), F32)] * ATTN_SLOTS,
        ),
        out_shape=jax.ShapeDtypeStruct((n_rows, 512), BF16),
        input_output_aliases=aliases,
        compiler_params=_params("parallel", "parallel", "arbitrary"),
        name="fox_attention" if fox else "mla_attention",
    )(*args)


def _first_live_key_tile(norm, edge, n_seq, t, tile):
    tiles = t // tile
    h = FOX_HEADS
    per_tile = lambda a: a[0:8 * n_seq * tiles:8].reshape(n_seq, tiles, 2 * LANES)
    qmax = per_tile(norm)[..., :h] * NORM_SLACK
    kmax = per_tile(norm)[..., LANES:LANES + h] * NORM_SLACK
    cum_q = per_tile(edge)[..., :h]
    cum_k = per_tile(edge)[..., LANES:LANES + h]
    bound = qmax[:, :, None] * lax.cummax(kmax, axis=1)[:, None] + cum_q[:, :, None] - cum_k[:, None]
    own = -qmax * kmax
    dead = bound < own[:, :, None] - UNDERFLOW_MARGIN
    dead = jnp.all(dead.reshape(n_seq, tiles, tiles, h // 2, 2), axis=-1)
    first = jnp.sum(jnp.cumprod(dead.astype(jnp.int32), axis=2), axis=2)
    return jnp.transpose(first, (0, 2, 1))


def _group_partner(x, lane, d, width):
    fwd = pltpu.roll(x, LANES - d, 1)
    back = pltpu.roll(x, width - d, 1)
    return jnp.where(lane % width + d < width, fwd, back)


def _route(logits, b_router, lane):
    scores = jax.nn.sigmoid(logits)
    sel = scores + b_router
    e = EXPERTS_PER_GROUP
    rank = jnp.zeros(sel.shape, jnp.int32)
    for d in range(1, e):
        other = _group_partner(sel, lane, d, e)
        other_first = lane % e + d >= e
        beats = (other > sel) | ((other == sel) & other_first)
        rank = rank + beats.astype(jnp.int32)
    top2 = rank < 2
    kept = jnp.where(top2, sel, 0.0)
    gscore = kept
    for d in range(1, e):
        gscore = gscore + _group_partner(kept, lane, d, e)
    grank = jnp.zeros_like(rank)
    for gstep in range(1, N_GROUPS):
        d = gstep * e
        other = _group_partner(gscore, lane, d, N_EXPERTS)
        other_first = lane % N_EXPERTS + d >= N_EXPERTS
        beats = (other > gscore) | ((other == gscore) & other_first)
        grank = grank + beats.astype(jnp.int32)
    chosen = top2 & (grank == 0) & (lane < N_EXPERTS)
    w = jnp.where(chosen, scores, 0.0)
    return w / jnp.sum(w, axis=1, keepdims=True)


def _post_kernel(x_ref, fo_ref, mo_ref, gate_ref, pa_ref, pb_ref, wo_ref, g1_ref, b1_ref, wr_ref, br_ref,
                 x1_ref, gates_ref, *, alpha):
    d = x_ref.shape[1]
    merged = gate_ref[:, 0:d].astype(F32) * _dot(fo_ref[...], pa_ref[...])
    merged = merged + gate_ref[:, d:2 * d].astype(F32) * _dot(mo_ref[...], pb_ref[...])
    mix = _dot(merged.astype(BF16), wo_ref[...])
    x1 = _layer_norm(alpha * x_ref[...] + mix, g1_ref[...], b1_ref[...])
    x1_ref[...] = x1
    hi, mid, lo = _split3(x1)
    a = _dot(hi, wr_ref[:, 0:2 * LANES])
    b = _dot(mid, wr_ref[:, 0:2 * LANES])
    logits = _dot(lo, wr_ref[:, 0:LANES]) + _dot(hi, wr_ref[:, 2 * LANES:3 * LANES])
    logits = logits + b[:, LANES:] + a[:, LANES:] + b[:, :LANES] + a[:, :LANES]
    lane = lax.broadcasted_iota(jnp.int32, logits.shape, 1)
    gates_ref[...] = _route(logits, br_ref[...], lane)


def _post_attention(x, fo, mo, gate, pa, pb, wo, g1, b1, wr3, br, alpha):
    n, d = x.shape
    tm = ROW_TILE
    row = lambda width: pl.BlockSpec((tm, width), lambda i: (i, 0))
    full = lambda a: pl.BlockSpec(a.shape, lambda i: (0,) * a.ndim)
    return pl.pallas_call(
        functools.partial(_post_kernel, alpha=alpha),
        grid=(n // tm,),
        in_specs=[row(d), row(512), row(512), row(2 * d), full(pa), full(pb), full(wo), full(g1), full(b1), full(wr3), full(br)],
        out_specs=[row(d), row(LANES)],
        out_shape=[jax.ShapeDtypeStruct((n, d), F32), jax.ShapeDtypeStruct((n, LANES), F32)],
        compiler_params=_params("parallel"),
        name="post_attention",
    )(x, fo, mo, gate, pa, pb, wo, g1, b1, wr3, br)


def _moe_kernel(x_ref, gates_ref, wg_ref, wu_ref, wd_ref, g2_ref, b2_ref, o_ref, xb_ref, acc_ref, *, alpha):
    e = pl.program_id(1)

    @pl.when(e == 0)
    def _():
        xb_ref[...] = x_ref[...].astype(BF16)
        acc_ref[...] = jnp.zeros_like(acc_ref)

    lane = lax.broadcasted_iota(jnp.int32, (1, LANES), 1)
    gate = jnp.sum(jnp.where(lane == e, gates_ref[...], 0.0), axis=1, keepdims=True)
    xb = xb_ref[...]
    hid = jax.nn.silu(_dot(xb, wg_ref[0])) * _dot(xb, wu_ref[0])
    acc_ref[...] += _dot((hid * gate).astype(BF16), wd_ref[0])

    @pl.when(e == pl.num_programs(1) - 1)
    def _():
        o_ref[...] = _layer_norm(alpha * x_ref[...] + acc_ref[...], g2_ref[...], b2_ref[...])


def _moe(x, gates, wg, wu, wd, g2, b2, alpha):
    n, d = x.shape
    n_exp, _, f = wg.shape
    tm = MOE_ROW_TILE if n % MOE_ROW_TILE == 0 else ROW_TILE
    return pl.pallas_call(
        functools.partial(_moe_kernel, alpha=alpha),
        grid=(n // tm, n_exp),
        in_specs=[
            pl.BlockSpec((tm, d), lambda i, e: (i, 0)),
            pl.BlockSpec((tm, LANES), lambda i, e: (i, 0)),
            pl.BlockSpec((1, d, f), lambda i, e: (e, 0, 0)),
            pl.BlockSpec((1, d, f), lambda i, e: (e, 0, 0)),
            pl.BlockSpec((1, f, d), lambda i, e: (e, 0, 0)),
            pl.BlockSpec((1, d), lambda i, e: (0, 0)),
            pl.BlockSpec((1, d), lambda i, e: (0, 0)),
        ],
        out_specs=pl.BlockSpec((tm, d), lambda i, e: (i, 0)),
        out_shape=jax.ShapeDtypeStruct((n, d), F32),
        scratch_shapes=[pltpu.VMEM((tm, d), BF16), pltpu.VMEM((tm, d), F32)],
        compiler_params=_params("parallel", "arbitrary"),
        name="moe_ln2",
    )(x, gates, wg, wu, wd, g2, b2)


def _rope_tables(pos):
    half = MLA_ROPE_DIM // 2
    inv_freq = ROPE_THETA ** (-jnp.arange(half, dtype=F32) / half)
    ang = pos.astype(F32)[:, None] * inv_freq[None, :]
    cos, sin = jnp.cos(ang), jnp.sin(ang)
    cos32 = jnp.concatenate([cos, cos], axis=1)
    sin32 = jnp.concatenate([-sin, sin], axis=1)
    return jnp.tile(cos32, (1, LANES // MLA_ROPE_DIM)), jnp.tile(sin32, (1, LANES // MLA_ROPE_DIM))


def _swap_halves(w):
    half = MLA_ROPE_DIM // 2
    return jnp.concatenate([w[..., half:], w[..., :half]], axis=-1)


def _prepare_w_in(w_in):
    depth, d, _ = w_in.shape
    o = [0]
    for sz in (FOX_WIDTH, FOX_WIDTH, FOX_WIDTH, FOX_HEADS, 256, 128, MLA_ROPE_DIM, 2 * d):
        o.append(o[-1] + sz)
    qkv = w_in[..., o[0]:o[3]]
    wf = w_in[..., o[3]:o[4]]
    wcq = w_in[..., o[4]:o[5]]
    wckv = w_in[..., o[5]:o[6]]
    wkr = w_in[..., o[6]:o[7]]
    wg = w_in[..., o[7]:o[8]]
    reps = LANES // MLA_ROPE_DIM
    wf_pad = jnp.concatenate([wf, jnp.zeros((depth, d, LANES - FOX_HEADS), w_in.dtype)], axis=-1)
    return jnp.concatenate(
        [qkv, wcq, wckv, wf_pad, jnp.tile(wkr, (1, 1, reps)), jnp.tile(_swap_halves(wkr), (1, 1, reps)), wg], axis=-1
    ).astype(BF16)


def _prepare_w_q_up(w):
    depth, r, _ = w.shape
    w = w.reshape(depth, r, MLA_HEADS, MLA_QK_DIM)
    nope = w[..., :MLA_NOPE_DIM].reshape(depth, r, MLA_HEADS * MLA_NOPE_DIM)
    rope = w[..., MLA_NOPE_DIM:]
    rope_sw = _swap_halves(rope).reshape(depth, r, MLA_HEADS * MLA_ROPE_DIM)
    rope = rope.reshape(depth, r, MLA_HEADS * MLA_ROPE_DIM)
    return jnp.concatenate([nope, rope, rope_sw], axis=-1).astype(BF16)


def _prepare_w_kv_up(w):
    depth, r, _ = w.shape
    w = w.reshape(depth, r, MLA_HEADS, MLA_NOPE_DIM + MLA_V_DIM)
    kn = w[..., :MLA_NOPE_DIM].reshape(depth, r, MLA_HEADS * MLA_NOPE_DIM)
    vm = w[..., MLA_NOPE_DIM:].reshape(depth, r, MLA_HEADS * MLA_V_DIM)
    return jnp.concatenate([kn, vm], axis=-1).astype(BF16)


def _pad_lanes(x, width=LANES):
    return jnp.pad(x, [(0, 0)] * (x.ndim - 1) + [(0, width - x.shape[-1])])


def kernel(x_prompt, x_sample, cache_fox_k, cache_fox_v, cache_fox_logf, cache_mla_ckv, cache_mla_krope, meta_tokens, ln_in_g, ln_in_b, w_in, b_fgt, g_q_norm, w_q_up, g_kv_norm, w_kv_up, w_proj_fox, w_proj_mla, b_gate, w_out, ln1_g, ln1_b, ln2_g, ln2_b, w_router, b_router, w_exp_gate, w_exp_up, w_exp_down):
    bp, seq, d = x_prompt.shape
    bs, tn, _ = x_sample.shape
    depth = w_in.shape[0]
    past = cache_fox_k.shape[2]
    alpha = (2 * depth) ** 0.25
    t_real = N_META + seq
    tp = -(-t_real // ATTN_Q_TILE) * ATTN_Q_TILE
    n_p = bp * tp
    n_s = bs * tn
    n = n_p + n_s
    n_pad = n
    tk_s = -(-(past + tn) // SAMPLE_KEY_TILE) * SAMPLE_KEY_TILE
    tq_s = SAMPLE_QUERY_TILE
    assert n % ROW_TILE == 0 and ROW_TILE % tn == 0 and tn <= tq_s, (n, tn)
    assert ATTN_Q_TILE == ATTN_TILE == ROW_TILE

    meta = jnp.broadcast_to(meta_tokens[None].astype(F32), (bp, N_META, d))
    xp = jnp.concatenate([meta, x_prompt, jnp.zeros((bp, tp - t_real, d), F32)], axis=1)
    x = jnp.concatenate([xp.reshape(n_p, d), x_sample.reshape(n_s, d), jnp.zeros((n_pad - n, d), F32)], axis=0)
    pos = jnp.concatenate([
        jnp.tile(jnp.arange(tp, dtype=jnp.int32), bp),
        jnp.tile(past + jnp.arange(tn, dtype=jnp.int32), bs),
        jnp.zeros((n_pad - n,), jnp.int32),
    ])
    cos, sin = _rope_tables(pos)

    w_in_r = _prepare_w_in(w_in)
    w_q_r = _prepare_w_q_up(w_q_up)
    w_kv_r = _prepare_w_kv_up(w_kv_up)
    bf_r = _pad_lanes(b_fgt)[:, None, :]
    pa, pb, wo = w_proj_fox.astype(BF16), w_proj_mla.astype(BF16), w_out.astype(BF16)
    wg, wu, wd = w_exp_gate.astype(BF16), w_exp_up.astype(BF16), w_exp_down.astype(BF16)
    wr = _pad_lanes(w_router)
    keep_bf16_bits = lambda a: lax.bitcast_convert_type(
        lax.bitcast_convert_type(a, jnp.uint32) & jnp.uint32(0xFFFF0000), F32
    )
    wr_hi = keep_bf16_bits(wr)
    wr_mid = keep_bf16_bits(wr - wr_hi)
    wr_lo = wr - wr_hi - wr_mid
    wr3 = jnp.concatenate([wr_hi, wr_mid, wr_lo], axis=1).astype(BF16)
    br = _pad_lanes(b_router[None, :])

    zpad = tk_s - past - tn
    kn_c, vmt_c = _kv_up(cache_mla_ckv.reshape(depth, bs * past, -1), w_kv_r)
    kn_c = kn_c.reshape(depth, bs, past, 512)
    vmt_c = jnp.transpose(vmt_c.reshape(depth, 512, bs, past), (0, 2, 1, 3))
    kf_c = cache_fox_k.reshape(depth, bs, past, FOX_WIDTH).astype(BF16)
    vft_c = jnp.transpose(cache_fox_v.reshape(depth, bs, past, FOX_WIDTH), (0, 1, 3, 2)).astype(BF16)
    kr_c = jnp.tile(cache_mla_krope, (1, 1, 1, LANES // MLA_ROPE_DIM)).astype(BF16)
    logf_c = _pad_lanes(cache_fox_logf)

    def with_past(cached, new):
        new = new.reshape(bs, tn, new.shape[-1])
        return jnp.concatenate([cached, new, jnp.zeros((bs, zpad, new.shape[-1]), new.dtype)], axis=1)

    def with_past_t(cached_t, new_t):
        new_t = jnp.transpose(new_t.reshape(512, bs, tn), (1, 0, 2))
        return jnp.concatenate([cached_t, new_t, jnp.zeros((bs, 512, zpad), new_t.dtype)], axis=2)

    def sample_queries_t(a_t):
        a_t = jnp.transpose(a_t[:, n_p:n].reshape(a_t.shape[0], bs, tn), (1, 0, 2))
        return jnp.pad(a_t, ((0, 0), (0, 0), (0, tq_s - tn)))

    prompt = dict(
        n_batch=bp, t_q=tp, t_k=tp, tq=ATTN_Q_TILE, tk=ATTN_TILE, q_off=0, n_meta=N_META, out_rows=ATTN_Q_TILE,
        out=n_pad,
    )
    sample = dict(
        n_batch=bs, t_q=tq_s, t_k=tk_s, tq=tq_s, tk=SAMPLE_KEY_TILE, q_off=past, n_meta=0, out_rows=tn, out_row0=n_p
    )

    x = _input_ln(x, ln_in_g[None, :], ln_in_b[None, :])
    state_widths = (FOX_WIDTH, FOX_WIDTH, FOX_HEADS, 128, MLA_ROPE_DIM)
    states = tuple(jnp.zeros((depth, bp, tp, wd), F32) for wd in state_widths)
    states += tuple(jnp.zeros((depth, bs, tn, wd), F32) for wd in state_widths)
    for l in range(depth):
        pre, states = _pre_attention(
            x, w_in_r[l], bf_r[l], g_q_norm[l][None, :], w_q_r[l], g_kv_norm[l][None, :], w_kv_r[l],
            b_gate[l][None, :], cos, sin,
            layer=l, depth=depth, prompt_dims=(bp, tp), sample_dims=(bs, tn), states=states,
        )
        qx_p, kx_p, edge_p = _forget_bias_operands(pre["logf"], bp, tp, ATTN_TILE)
        first_p = _first_live_key_tile(pre["norm"], edge_p, bp, tp, ATTN_TILE)
        fo = _attention(pre["qft"], qx_p, pre["kf"], kx_p, pre["vft"], fox=True, first=first_p, **prompt)
        mo = _attention(pre["qnt"], pre["qrt"], pre["kn"], pre["kr"], pre["vmt"], fox=False, **prompt)

        logf_s = with_past(logf_c[l], pre["logf"][n_p:n]).reshape(bs * tk_s, LANES)
        qx_s, kx_s, _ = _forget_bias_operands(logf_s, bs, tk_s, SAMPLE_KEY_TILE)
        qx_s = jnp.pad(qx_s[:, :, :, past:past + tn], ((0, 0), (0, 0), (0, 0), (0, tq_s - tn)))
        fo = _attention(
            sample_queries_t(pre["qft"]), qx_s, with_past(kf_c[l], pre["kf"][n_p:n]), kx_s,
            with_past_t(vft_c[l], pre["vft"][:, n_p:n]), fox=True, out=fo, **sample,
        )
        mo = _attention(
            sample_queries_t(pre["qnt"]), sample_queries_t(pre["qrt"]), with_past(kn_c[l], pre["kn"][n_p:n]),
            with_past(kr_c[l], pre["kr"][n_p:n]), with_past_t(vmt_c[l], pre["vmt"][:, n_p:n]), fox=False, out=mo,
            **sample,
        )
        x1, gates = _post_attention(
            x, fo, mo, pre["gate"], pa[l], pb[l], wo[l], ln1_g[l][None, :], ln1_b[l][None, :], wr3, br, alpha
        )
        x = _moe(x1, gates, wg[l], wu[l], wd[l], ln2_g[l][None, :], ln2_b[l][None, :], alpha)

    heads = (FOX_HEADS, FOX_HEAD_DIM)
    k_p, v_p, logf_p, ckv_p, kr_p, k_s, v_s, logf_s, ckv_s, kr_s = states
    y_prompt = x[:n_p].reshape(bp, tp, d)[:, N_META:t_real]
    y_sample = x[n_p:n].reshape(bs, tn, d)
    return (
        y_prompt,
        y_sample,
        k_p[:, :, :t_real].reshape(depth, bp, t_real, *heads),
        v_p[:, :, :t_real].reshape(depth, bp, t_real, *heads),
        logf_p[:, :, :t_real],
        ckv_p[:, :, :t_real],
        kr_p[:, :, :t_real],
        k_s.reshape(depth, bs, tn, *heads),
        v_s.reshape(depth, bs, tn, *heads),
        logf_s,
        ckv_s,
        kr_s,
    )
```

```python
import functools

import jax
import jax.numpy as jnp
import numpy as np
from jax import lax
from jax.experimental import pallas as pl
from jax.experimental.pallas import tpu as pltpu

F32 = jnp.float32
BF16 = jnp.bfloat16

CHUNK = 64
N_META = 16
FOX_HEADS = 8
FOX_HEAD_DIM = 64
FOX_WIDTH = FOX_HEADS * FOX_HEAD_DIM
MLA_HEADS = 8
MLA_NOPE_DIM = 64
MLA_ROPE_DIM = 32
MLA_QK_DIM = MLA_NOPE_DIM + MLA_ROPE_DIM
MLA_V_DIM = 64
N_EXPERTS = 16
N_GROUPS = 4
EXPERTS_PER_GROUP = N_EXPERTS // N_GROUPS
ROPE_THETA = 10000.0
LN_EPS = 1e-5
RMS_EPS = 1e-6
LOG2E = 1.4426950408889634

LANES = 128
ROW_TILE = 256
MOE_ROW_TILE = 1024
ATTN_TILE = 256
ATTN_Q_TILE = 256
ATTN_SLOTS = 4
QK_GROUP = 2
SAMPLE_QUERY_TILE = 128
SAMPLE_KEY_TILE = 384
AUG_GROUP = 8
ONES_ROWS = 16
MASK_VALUE = -1e30
UNDERFLOW_MARGIN = 170.0
NORM_SLACK = 1.01
VMEM_LIMIT = 56 * 1024 * 1024

C_QKV = 0
C_CQ = 3 * FOX_WIDTH
C_CKV = C_CQ + 256
C_LOGF = C_CKV + 128
C_KR = C_LOGF + LANES
C_KRS = C_KR + LANES
C_GATE = C_KRS + LANES


def _params(*sem):
    return pltpu.CompilerParams(dimension_semantics=sem, vmem_limit_bytes=VMEM_LIMIT)


def _layer_norm(x, g, b):
    mu = jnp.mean(x, axis=-1, keepdims=True)
    xc = x - mu
    var = jnp.mean(xc * xc, axis=-1, keepdims=True)
    return xc * lax.rsqrt(var + LN_EPS) * g + b


def _rms_norm(x, g):
    return x * lax.rsqrt(jnp.mean(x * x, axis=-1, keepdims=True) + RMS_EPS) * g


def _dot(a, b):
    return jnp.dot(a, b, preferred_element_type=F32)


def _split3(x):
    hi = x.astype(BF16)
    r1 = x - hi.astype(F32)
    mid = r1.astype(BF16)
    lo = (r1 - mid.astype(F32)).astype(BF16)
    return hi, mid, lo


def _ln_kernel(x_ref, g_ref, b_ref, o_ref):
    o_ref[...] = _layer_norm(x_ref[...], g_ref[...], b_ref[...])


def _input_ln(x, g, b):
    n, d = x.shape
    return pl.pallas_call(
        _ln_kernel,
        grid=(n // ROW_TILE,),
        in_specs=[
            pl.BlockSpec((ROW_TILE, d), lambda i: (i, 0)),
            pl.BlockSpec((1, d), lambda i: (0, 0)),
            pl.BlockSpec((1, d), lambda i: (0, 0)),
        ],
        out_specs=pl.BlockSpec((ROW_TILE, d), lambda i: (i, 0)),
        out_shape=jax.ShapeDtypeStruct((n, d), F32),
        compiler_params=_params("parallel"),
        name="input_ln",
    )(x, g, b)


N_ROW_STATES = 5
N_OPERANDS = 11


def _pre_kernel(x_ref, w_ref, bf_ref, gq_ref, wq_ref, gkv_ref, wkv_ref, bg_ref, cos_ref, sin_ref, *rest,
                fox_scale, mla_scale, n_prompt_tiles):
    outs = rest[len(rest) - N_OPERANDS - 2 * N_ROW_STATES:]
    (qft_ref, kf_ref, vft_ref, logf_ref, qnt_ref, qrt_ref, kr_ref, kn_ref, vmt_ref, gate_ref,
     norm_ref) = outs[:N_OPERANDS]
    prompt_refs = outs[N_OPERANDS:N_OPERANDS + N_ROW_STATES]
    sample_refs = outs[N_OPERANDS + N_ROW_STATES:]
    xb = x_ref[...].astype(BF16)
    cos = cos_ref[...]
    sin = sin_ref[...]

    qkv = _dot(xb, w_ref[:, C_QKV:C_CQ])
    q = qkv[:, 0:FOX_WIDTH] * fox_scale
    qft_ref[...] = q.T.astype(BF16)
    k = qkv[:, FOX_WIDTH:2 * FOX_WIDTH]
    v = qkv[:, 2 * FOX_WIDTH:3 * FOX_WIDTH]
    kf_ref[...] = k.astype(BF16)
    vft_ref[...] = v.T.astype(BF16)

    feat = lax.broadcasted_iota(jnp.int32, (FOX_WIDTH, LANES), 0)
    head = lax.broadcasted_iota(jnp.int32, (FOX_WIDTH, LANES), 1)
    head_of = jnp.where(feat // FOX_HEAD_DIM == head, 1.0, 0.0).astype(BF16)
    top = lambda a: jnp.sqrt(jnp.max(_dot((a * a).astype(BF16), head_of), axis=0, keepdims=True))
    norm_ref[...] = jnp.broadcast_to(jnp.concatenate([top(q), top(k)], axis=1), norm_ref.shape)

    z = _dot(xb, w_ref[:, C_LOGF:C_KR]) + bf_ref[...]
    logf = jnp.minimum(z, 0.0) - jnp.log1p(jnp.exp(-jnp.abs(z)))
    logf_ref[...] = logf

    cq = _rms_norm(_dot(xb, w_ref[:, C_CQ:C_CKV]), gq_ref[...]).astype(BF16)
    qm = _dot(cq, wq_ref[...])
    qnt_ref[...] = (qm[:, 0:512] * mla_scale).T.astype(BF16)
    cos2 = jnp.concatenate([cos, cos], axis=1)
    sin2 = jnp.concatenate([sin, sin], axis=1)
    qrt_ref[...] = ((qm[:, 512:768] * cos2 + qm[:, 768:1024] * sin2) * mla_scale).T.astype(BF16)

    ckv = _rms_norm(_dot(xb, w_ref[:, C_CKV:C_LOGF]), gkv_ref[...])
    kv = _dot(ckv.astype(BF16), wkv_ref[...])
    kn_ref[...] = kv[:, 0:512].astype(BF16)
    vmt_ref[...] = kv[:, 512:1024].T.astype(BF16)
    kr = _dot(xb, w_ref[:, C_KR:C_KRS]) * cos + _dot(xb, w_ref[:, C_KRS:C_GATE]) * sin
    kr_ref[...] = kr.astype(BF16)

    g = _dot(xb, w_ref[:, C_GATE:]) + bg_ref[...]
    gate_ref[...] = jax.nn.sigmoid(g).astype(BF16)

    states = (k, v, logf[:, :FOX_HEADS], ckv, kr[:, :MLA_ROPE_DIM])
    i = pl.program_id(0)

    @pl.when(i < n_prompt_tiles)
    def _():
        for ref, val in zip(prompt_refs, states):
            ref[...] = val

    @pl.when(i >= n_prompt_tiles)
    def _():
        for ref, val in zip(sample_refs, states):
            ref[...] = val.reshape(ref.shape)


def _pre_attention(x, w, bf, gq, wq, gkv, wkv, bg, cos, sin, *, layer, depth, prompt_dims, sample_dims, states):
    n, d = x.shape
    tm = ROW_TILE
    bp, tp = prompt_dims
    bs, tn = sample_dims
    tiles_per_stream = tp // tm
    n_prompt_tiles = bp * tiles_per_stream
    streams_per_tile = tm // tn
    row = lambda width: pl.BlockSpec((tm, width), lambda i: (i, 0))
    col = lambda height: pl.BlockSpec((height, tm), lambda i: (0, i))
    full = lambda a: pl.BlockSpec(a.shape, lambda i: (0,) * a.ndim)
    outs = [
        ("qft", 512, BF16, True), ("kf", 512, BF16, False), ("vft", 512, BF16, True), ("logf", LANES, F32, False),
        ("qnt", 512, BF16, True), ("qrt", 256, BF16, True), ("kr", LANES, BF16, False), ("kn", 512, BF16, False),
        ("vmt", 512, BF16, True), ("gate", 2 * d, BF16, False),
    ]
    state_widths = (FOX_WIDTH, FOX_WIDTH, FOX_HEADS, 128, MLA_ROPE_DIM)

    def prompt_block(i):
        ip = jnp.minimum(i, n_prompt_tiles - 1)
        return layer, ip // tiles_per_stream, ip % tiles_per_stream, 0

    def sample_block(i):
        return layer, jnp.maximum(i - n_prompt_tiles, 0), 0, 0

    out_specs = [col(wd) if tr else row(wd) for _, wd, _, tr in outs]
    out_shape = [jax.ShapeDtypeStruct((wd, n) if tr else (n, wd), dt) for _, wd, dt, tr in outs]
    outs.append(("norm", 2 * LANES, F32, False))
    out_specs.append(pl.BlockSpec((8, 2 * LANES), lambda i: (i, 0)))
    out_shape.append(jax.ShapeDtypeStruct((8 * (n // tm), 2 * LANES), F32))
    assert len(outs) == N_OPERANDS
    out_specs += [pl.BlockSpec((None, None, tm, wd), prompt_block) for wd in state_widths]
    out_specs += [pl.BlockSpec((None, streams_per_tile, tn, wd), sample_block) for wd in state_widths]
    out_shape += [jax.ShapeDtypeStruct((depth, bp, tp, wd), F32) for wd in state_widths]
    out_shape += [jax.ShapeDtypeStruct((depth, bs, tn, wd), F32) for wd in state_widths]
    args = [x, w, bf, gq, wq, gkv, wkv, bg, cos, sin]
    in_specs = [row(d), full(w), full(bf), full(gq), full(wq), full(gkv), full(wkv), full(bg), row(LANES), row(LANES)]
    aliases = {len(args) + j: len(outs) + j for j in range(len(states))}
    args += list(states)
    in_specs += [pl.BlockSpec(memory_space=pl.ANY)] * len(states)
    res = pl.pallas_call(
        functools.partial(
            _pre_kernel, fox_scale=FOX_HEAD_DIM ** -0.5 * LOG2E, mla_scale=MLA_QK_DIM ** -0.5 * LOG2E,
            n_prompt_tiles=n_prompt_tiles,
        ),
        grid=(n // tm,),
        in_specs=in_specs,
        out_specs=out_specs,
        out_shape=out_shape,
        input_output_aliases=aliases,
        compiler_params=_params("arbitrary"),
        name="pre_attention",
    )(*args)
    return {name: r for (name, _, _, _), r in zip(outs, res)}, tuple(res[len(outs):])


def _kvup_kernel(c_ref, w_ref, kn_ref, vmt_ref):
    kv = _dot(c_ref[0].astype(BF16), w_ref[0])
    kn_ref[0] = kv[:, 0:512].astype(BF16)
    vmt_ref[0] = kv[:, 512:1024].T.astype(BF16)


def _kv_up(ckv, wkv):
    depth, n, r = ckv.shape
    tm = ROW_TILE
    return pl.pallas_call(
        _kvup_kernel,
        grid=(depth, n // tm),
        in_specs=[pl.BlockSpec((1, tm, r), lambda l, i: (l, i, 0)), pl.BlockSpec((1, r, 1024), lambda l, i: (l, 0, 0))],
        out_specs=[pl.BlockSpec((1, tm, 512), lambda l, i: (l, i, 0)), pl.BlockSpec((1, 512, tm), lambda l, i: (l, 0, i))],
        out_shape=[jax.ShapeDtypeStruct((depth, n, 512), BF16), jax.ShapeDtypeStruct((depth, 512, n), BF16)],
        compiler_params=_params("parallel", "parallel"),
        name="kv_up_cached",
    )(ckv, wkv)


def _bias_layout():
    sel_q = np.zeros((3 * LANES, 4 * LANES), np.float32)
    sel_k = np.zeros((3 * LANES, 4 * LANES), np.float32)
    one_q = np.zeros((1, 4 * LANES), np.float32)
    one_k = np.zeros((1, 4 * LANES), np.float32)
    for h in range(FOX_HEADS):
        base = (h // 2) * LANES + (h % 2) * AUG_GROUP
        for piece in range(3):
            sel_q[piece * LANES + h, base + piece] = 1.0
            sel_k[piece * LANES + h, base + 3 + piece] = -1.0
            one_q[0, base + 3 + piece] = 1.0
            one_k[0, base + piece] = 1.0
    return (jnp.asarray(sel_q, BF16), jnp.asarray(one_q, F32), jnp.asarray(sel_k, BF16), jnp.asarray(one_k, F32))


def _cumsum_kernel(x_ref, sq_ref, oq_ref, sk_ref, ok_ref, qxt_ref, kx_ref, edge_ref, carry_ref, *, tt):
    @pl.when(pl.program_id(1) == 0)
    def _():
        carry_ref[...] = jnp.zeros_like(carry_ref)

    r = lax.broadcasted_iota(jnp.int32, (tt, tt), 0)
    c = lax.broadcasted_iota(jnp.int32, (tt, tt), 1)
    tri = jnp.where(c <= r, 1.0, 0.0).astype(BF16)
    hi, mid, lo = _split3(x_ref[...])
    cum = _dot(tri, lo) + _dot(tri, mid) + _dot(tri, hi) + carry_ref[...]
    carry_ref[...] = cum[tt - 1:tt, :]
    cum2 = cum * LOG2E
    edge_ref[...] = jnp.broadcast_to(jnp.concatenate([cum2[0:1], cum2[tt - 1:tt]], axis=1), edge_ref.shape)
    pieces = jnp.concatenate(_split3(cum2), axis=1)
    kx = (_dot(pieces, sk_ref[...]) + ok_ref[...]).astype(BF16)
    qx = _dot(pieces, sq_ref[...]) + oq_ref[...]
    for p in range(FOX_HEADS // 2):
        kx_ref[p] = kx[:, p * LANES:(p + 1) * LANES]
        qxt_ref[p] = qx[:, p * LANES:(p + 1) * LANES].T.astype(BF16)


def _forget_bias_operands(logf, n_seq, t, tt):
    w = logf.shape[-1]
    tiles = t // tt
    consts = _bias_layout()
    pairs = FOX_HEADS // 2
    return pl.pallas_call(
        functools.partial(_cumsum_kernel, tt=tt),
        grid=(n_seq, tiles),
        in_specs=[pl.BlockSpec((tt, w), lambda b, i: (b * tiles + i, 0))]
        + [pl.BlockSpec(c.shape, lambda b, i: (0, 0)) for c in consts],
        out_specs=[
            pl.BlockSpec((None, pairs, LANES, tt), lambda b, i: (b, 0, 0, i)),
            pl.BlockSpec((None, pairs, tt, LANES), lambda b, i: (b, 0, i, 0)),
            pl.BlockSpec((8, 2 * w), lambda b, i: (b * tiles + i, 0)),
        ],
        out_shape=[
            jax.ShapeDtypeStruct((n_seq, pairs, LANES, t), BF16),
            jax.ShapeDtypeStruct((n_seq, pairs, t, LANES), BF16),
            jax.ShapeDtypeStruct((8 * n_seq * tiles, 2 * w), F32),
        ],
        scratch_shapes=[pltpu.VMEM((1, w), F32)],
        compiler_params=_params("parallel", "arbitrary"),
        name="cumsum_time",
    )(logf, *consts)


def _chunk_end(pos, n_meta):
    body = n_meta + ((pos - n_meta) // CHUNK) * CHUNK + CHUNK - 1
    if n_meta == 0:
        return body
    return jnp.where(pos < n_meta, n_meta - 1, body)


def _attn_kernel(first_ref, qt_ref, qxt_ref, k_ref, kx_ref, vt_ref, *rest, fox, tq, tk, q_off, n_meta, n_ktiles):
    o_ref, *slots = rest[len(rest) - ATTN_SLOTS - 1:]
    p = pl.program_id(1)
    i = pl.program_id(2)
    q_lo = q_off + i * tq
    q_hi = q_lo + tq - 1
    if fox:
        vis_lo, vis_hi = q_lo, q_hi
    else:
        vis_lo, vis_hi = _chunk_end(q_lo, n_meta), _chunk_end(q_hi, n_meta)
    n_tot = jnp.minimum(vis_hi // tk + 1, n_ktiles)
    n_full = jnp.minimum((vis_lo + 1) // tk, n_tot - 1)
    first = jnp.clip(first_ref[pl.program_id(0), p, i], 0, n_full)
    n_tot = n_tot - first
    n_full = n_full - first

    q_pos = q_lo + lax.broadcasted_iota(jnp.int32, (1, tq), 1)
    row = lax.broadcasted_iota(jnp.int32, (LANES, 1), 0)
    qt2 = qt_ref[...]
    qxt2 = qxt_ref[...]
    cols = []
    for hh in range(2):
        x_group = row // AUG_GROUP == hh if fox else row // MLA_ROPE_DIM == 2 * (p % 2) + hh
        cols.append(jnp.concatenate([
            jnp.where(row // FOX_HEAD_DIM == hh, qt2, jnp.zeros_like(qt2)),
            jnp.where(x_group, qxt2, jnp.zeros_like(qxt2)),
        ], axis=0))
    qcbt = jnp.concatenate(cols, axis=1)
    last_visible = q_pos if fox else _chunk_end(q_pos, n_meta)
    last_visible = jnp.concatenate([last_visible, last_visible], axis=1)
    key_iota = lax.broadcasted_iota(jnp.int32, (tk, 2 * tq), 0)
    ones = jnp.ones((ONES_ROWS, tk), BF16)
    n_slots = len(slots)

    def key_rows(j):
        return pl.ds(pl.multiple_of(jnp.minimum(first + j, n_ktiles - 1) * tk, tk), tk)

    def produce(slot_refs, j, masked):
        kcs = []
        for r in range(len(slot_refs)):
            ks = key_rows(j + r)
            kcs.append(jnp.concatenate([k_ref[ks, :], kx_ref[ks, :]], axis=1))
        st_all = _dot(jnp.concatenate(kcs, axis=0), qcbt)
        cmax = []
        for r, slot_ref in enumerate(slot_refs):
            st = st_all[r * tk:(r + 1) * tk]
            if masked:
                st = jnp.where(key_iota <= last_visible - (first + j + r) * tk, st, MASK_VALUE)
            slot_ref[...] = st
            cmax.append(jnp.max(st, axis=0, keepdims=True))
        return cmax

    def accumulate(j, pt, alpha, accs):
        ks = key_rows(j)
        new = []
        for hh in range(2):
            va = jnp.concatenate([vt_ref[FOX_HEAD_DIM * hh:FOX_HEAD_DIM * (hh + 1), ks], ones], axis=0)
            sl = slice(hh * tq, (hh + 1) * tq)
            new.append(alpha[:, sl] * accs[hh] + _dot(va, pt[:, sl]))
        return tuple(new)

    def step(u, carry, mode):
        m, cmax, accs = carry
        cmax = list(cmax)
        for r0 in range(0, n_slots, QK_GROUP):
            group = range(r0, r0 + QK_GROUP)
            pts, alphas = [], []
            for r in group:
                m_new = jnp.maximum(m, cmax[r])
                pts.append(jnp.exp2(slots[r][...] - m_new).astype(BF16))
                alphas.append(jnp.exp2(m - m_new))
                m = m_new
            if mode != "drain":
                cmax[r0:r0 + QK_GROUP] = produce([slots[r] for r in group], n_slots * (u + 1) + r0, mode == "masked")
            for r, pt, alpha in zip(group, pts, alphas):
                accs = accumulate(n_slots * u + r, pt, alpha, accs)
        return m, tuple(cmax), accs

    def two_steps(v, carry):
        return step(2 * v + 1, step(2 * v, carry, "plain"), "plain")

    m0 = jnp.full((1, 2 * tq), MASK_VALUE, F32)
    a0 = jnp.zeros((FOX_HEAD_DIM + ONES_ROWS, tq), F32)
    carry = (m0, tuple(produce(list(slots), 0, True)), (a0, a0))
    n_steps = (n_tot + n_slots - 1) // n_slots
    n_plain = jnp.maximum(n_full - n_slots, 0) // n_slots
    carry = lax.fori_loop(0, n_plain // 2, two_steps, carry)
    carry = lax.fori_loop(n_plain - n_plain % 2, n_plain, functools.partial(step, mode="plain"), carry)
    carry = lax.fori_loop(n_plain, n_steps - 1, functools.partial(step, mode="masked"), carry)
    _, _, accs = step(n_steps - 1, carry, "drain")
    out_t = jnp.concatenate(
        [acc[0:FOX_HEAD_DIM] / acc[FOX_HEAD_DIM:FOX_HEAD_DIM + 1] for acc in accs], axis=0
    )
    o_ref[...] = out_t.T[:o_ref.shape[0]].astype(o_ref.dtype)


def _attention(qt, qxt, k, kx, vt, *, fox, n_batch, t_q, t_k, tq, tk, q_off, n_meta, out_rows, out_row0=0, out=None,
               first=None):
    nq = t_q // tq

    def time_major(a, rows_of, t, tile):
        tiles = t // tile
        if a.ndim == 2:
            return pl.BlockSpec((LANES, tile), lambda bb, p, i, _: (rows_of(p), bb * tiles + i % tiles))
        return pl.BlockSpec((None, LANES, tile), lambda bb, p, i, _: (bb, rows_of(p), i % tiles))

    def row_major(a, col_of):
        if a.ndim == 2:
            return pl.BlockSpec((t_k, LANES), lambda bb, p, i, _: (bb, col_of(p)))
        return pl.BlockSpec((None, t_k, LANES), lambda bb, p, i, _: (bb, 0, col_of(p)))

    qspec = time_major(qt, lambda p: p, t_q, tq)
    kspec = row_major(k, lambda p: p)
    if fox:
        qxspec = pl.BlockSpec((None, None, LANES, tq), lambda bb, p, i, _: (bb, p, 0, i))
        kxspec = pl.BlockSpec((None, None, t_k, LANES), lambda bb, p, i, _: (bb, p, 0, 0))
    else:
        qxspec = time_major(qxt, lambda p: p // 2, t_q, tq)
        kxspec = row_major(kx, lambda p: 0)
    vspec = time_major(vt, lambda p: p, t_k, t_k)
    if first is None:
        first = jnp.zeros((n_batch, 4, nq), jnp.int32)
    args = [first, qt, qxt, k, kx, vt]
    in_specs = [qspec, qxspec, kspec, kxspec, vspec]
    if isinstance(out, int):
        n_rows, aliases = out, {}
    else:
        n_rows, aliases = out.shape[0], {len(args): 0}
        args.append(out)
        in_specs.append(pl.BlockSpec(memory_space=pl.ANY))
    block0 = out_row0 // out_rows
    return pl.pallas_call(
        functools.partial(_attn_kernel, fox=fox, tq=tq, tk=tk, q_off=q_off, n_meta=n_meta, n_ktiles=t_k // tk),
        grid_spec=pltpu.PrefetchScalarGridSpec(
            num_scalar_prefetch=1,
            grid=(n_batch, 4, nq),
            in_specs=in_specs,
            out_specs=pl.BlockSpec((out_rows, LANES), lambda bb, p, i, _: (block0 + bb * nq + i, p)),
            scratch_shapes=[pltpu.VMEM((tk, 2 * tq), F32)] * ATTN_SLOTS,
        ),
        out_shape=jax.ShapeDtypeStruct((n_rows, 512), BF16),
        input_output_aliases=aliases,
        compiler_params=_params("parallel", "parallel", "arbitrary"),
        name="fox_attention" if fox else "mla_attention",
    )(*args)


def _first_live_key_tile(norm, edge, n_seq, t, tile):
    tiles = t // tile
    h = FOX_HEADS
    per_tile = lambda a: a[0:8 * n_seq * tiles:8].reshape(n_seq, tiles, 2 * LANES)
    qmax = per_tile(norm)[..., :h] * NORM_SLACK
    kmax = per_tile(norm)[..., LANES:LANES + h] * NORM_SLACK
    cum_q = per_tile(edge)[..., :h]
    cum_k = per_tile(edge)[..., LANES:LANES + h]
    bound = qmax[:, :, None] * lax.cummax(kmax, axis=1)[:, None] + cum_q[:, :, None] - cum_k[:, None]
    own = -qmax * kmax
    dead = bound < own[:, :, None] - UNDERFLOW_MARGIN
    dead = jnp.all(dead.reshape(n_seq, tiles, tiles, h // 2, 2), axis=-1)
    first = jnp.sum(jnp.cumprod(dead.astype(jnp.int32), axis=2), axis=2)
    return jnp.transpose(first, (0, 2, 1))


def _group_partner(x, lane, d, width):
    fwd = pltpu.roll(x, LANES - d, 1)
    back = pltpu.roll(x, width - d, 1)
    return jnp.where(lane % width + d < width, fwd, back)


def _route(logits, b_router, lane):
    scores = jax.nn.sigmoid(logits)
    sel = scores + b_router
    e = EXPERTS_PER_GROUP
    rank = jnp.zeros(sel.shape, jnp.int32)
    for d in range(1, e):
        other = _group_partner(sel, lane, d, e)
        other_first = lane % e + d >= e
        beats = (other > sel) | ((other == sel) & other_first)
        rank = rank + beats.astype(jnp.int32)
    top2 = rank < 2
    kept = jnp.where(top2, sel, 0.0)
    gscore = kept
    for d in range(1, e):
        gscore = gscore + _group_partner(kept, lane, d, e)
    grank = jnp.zeros_like(rank)
    for gstep in range(1, N_GROUPS):
        d = gstep * e
        other = _group_partner(gscore, lane, d, N_EXPERTS)
        other_first = lane % N_EXPERTS + d >= N_EXPERTS
        beats = (other > gscore) | ((other == gscore) & other_first)
        grank = grank + beats.astype(jnp.int32)
    chosen = top2 & (grank == 0) & (lane < N_EXPERTS)
    w = jnp.where(chosen, scores, 0.0)
    return w / jnp.sum(w, axis=1, keepdims=True)


def _post_kernel(x_ref, fo_ref, mo_ref, gate_ref, pa_ref, pb_ref, wo_ref, g1_ref, b1_ref, wr_ref, br_ref,
                 x1_ref, gates_ref, *, alpha):
    d = x_ref.shape[1]
    merged = gate_ref[:, 0:d].astype(F32) * _dot(fo_ref[...], pa_ref[...])
    merged = merged + gate_ref[:, d:2 * d].astype(F32) * _dot(mo_ref[...], pb_ref[...])
    mix = _dot(merged.astype(BF16), wo_ref[...])
    x1 = _layer_norm(alpha * x_ref[...] + mix, g1_ref[...], b1_ref[...])
    x1_ref[...] = x1
    hi, mid, lo = _split3(x1)
    a = _dot(hi, wr_ref[:, 0:2 * LANES])
    b = _dot(mid, wr_ref[:, 0:2 * LANES])
    logits = _dot(lo, wr_ref[:, 0:LANES]) + _dot(hi, wr_ref[:, 2 * LANES:3 * LANES])
    logits = logits + b[:, LANES:] + a[:, LANES:] + b[:, :LANES] + a[:, :LANES]
    lane = lax.broadcasted_iota(jnp.int32, logits.shape, 1)
    gates_ref[...] = _route(logits, br_ref[...], lane)


def _post_attention(x, fo, mo, gate, pa, pb, wo, g1, b1, wr3, br, alpha):
    n, d = x.shape
    tm = ROW_TILE
    row = lambda width: pl.BlockSpec((tm, width), lambda i: (i, 0))
    full = lambda a: pl.BlockSpec(a.shape, lambda i: (0,) * a.ndim)
    return pl.pallas_call(
        functools.partial(_post_kernel, alpha=alpha),
        grid=(n // tm,),
        in_specs=[row(d), row(512), row(512), row(2 * d), full(pa), full(pb), full(wo), full(g1), full(b1), full(wr3), full(br)],
        out_specs=[row(d), row(LANES)],
        out_shape=[jax.ShapeDtypeStruct((n, d), F32), jax.ShapeDtypeStruct((n, LANES), F32)],
        compiler_params=_params("parallel"),
        name="post_attention",
    )(x, fo, mo, gate, pa, pb, wo, g1, b1, wr3, br)


def _moe_kernel(x_ref, gates_ref, wg_ref, wu_ref, wd_ref, g2_ref, b2_ref, o_ref, xb_ref, acc_ref, *, alpha):
    e = pl.program_id(1)

    @pl.when(e == 0)
    def _():
        xb_ref[...] = x_ref[...].astype(BF16)
        acc_ref[...] = jnp.zeros_like(acc_ref)

    lane = lax.broadcasted_iota(jnp.int32, (1, LANES), 1)
    gate = jnp.sum(jnp.where(lane == e, gates_ref[...], 0.0), axis=1, keepdims=True)
    xb = xb_ref[...]
    hid = jax.nn.silu(_dot(xb, wg_ref[0])) * _dot(xb, wu_ref[0])
    acc_ref[...] += _dot((hid * gate).astype(BF16), wd_ref[0])

    @pl.when(e == pl.num_programs(1) - 1)
    def _():
        o_ref[...] = _layer_norm(alpha * x_ref[...] + acc_ref[...], g2_ref[...], b2_ref[...])


def _moe(x, gates, wg, wu, wd, g2, b2, alpha):
    n, d = x.shape
    n_exp, _, f = wg.shape
    tm = MOE_ROW_TILE if n % MOE_ROW_TILE == 0 else ROW_TILE
    return pl.pallas_call(
        functools.partial(_moe_kernel, alpha=alpha),
        grid=(n // tm, n_exp),
        in_specs=[
            pl.BlockSpec((tm, d), lambda i, e: (i, 0)),
            pl.BlockSpec((tm, LANES), lambda i, e: (i, 0)),
            pl.BlockSpec((1, d, f), lambda i, e: (e, 0, 0)),
            pl.BlockSpec((1, d, f), lambda i, e: (e, 0, 0)),
            pl.BlockSpec((1, f, d), lambda i, e: (e, 0, 0)),
            pl.BlockSpec((1, d), lambda i, e: (0, 0)),
            pl.BlockSpec((1, d), lambda i, e: (0, 0)),
        ],
        out_specs=pl.BlockSpec((tm, d), lambda i, e: (i, 0)),
        out_shape=jax.ShapeDtypeStruct((n, d), F32),
        scratch_shapes=[pltpu.VMEM((tm, d), BF16), pltpu.VMEM((tm, d), F32)],
        compiler_params=_params("parallel", "arbitrary"),
        name="moe_ln2",
    )(x, gates, wg, wu, wd, g2, b2)


def _rope_tables(pos):
    half = MLA_ROPE_DIM // 2
    inv_freq = ROPE_THETA ** (-jnp.arange(half, dtype=F32) / half)
    ang = pos.astype(F32)[:, None] * inv_freq[None, :]
    cos, sin = jnp.cos(ang), jnp.sin(ang)
    cos32 = jnp.concatenate([cos, cos], axis=1)
    sin32 = jnp.concatenate([-sin, sin], axis=1)
    return jnp.tile(cos32, (1, LANES // MLA_ROPE_DIM)), jnp.tile(sin32, (1, LANES // MLA_ROPE_DIM))


def _swap_halves(w):
    half = MLA_ROPE_DIM // 2
    return jnp.concatenate([w[..., half:], w[..., :half]], axis=-1)


def _prepare_w_in(w_in):
    depth, d, _ = w_in.shape
    o = [0]
    for sz in (FOX_WIDTH, FOX_WIDTH, FOX_WIDTH, FOX_HEADS, 256, 128, MLA_ROPE_DIM, 2 * d):
        o.append(o[-1] + sz)
    qkv = w_in[..., o[0]:o[3]]
    wf = w_in[..., o[3]:o[4]]
    wcq = w_in[..., o[4]:o[5]]
    wckv = w_in[..., o[5]:o[6]]
    wkr = w_in[..., o[6]:o[7]]
    wg = w_in[..., o[7]:o[8]]
    reps = LANES // MLA_ROPE_DIM
    wf_pad = jnp.concatenate([wf, jnp.zeros((depth, d, LANES - FOX_HEADS), w_in.dtype)], axis=-1)
    return jnp.concatenate(
        [qkv, wcq, wckv, wf_pad, jnp.tile(wkr, (1, 1, reps)), jnp.tile(_swap_halves(wkr), (1, 1, reps)), wg], axis=-1
    ).astype(BF16)


def _prepare_w_q_up(w):
    depth, r, _ = w.shape
    w = w.reshape(depth, r, MLA_HEADS, MLA_QK_DIM)
    nope = w[..., :MLA_NOPE_DIM].reshape(depth, r, MLA_HEADS * MLA_NOPE_DIM)
    rope = w[..., MLA_NOPE_DIM:]
    rope_sw = _swap_halves(rope).reshape(depth, r, MLA_HEADS * MLA_ROPE_DIM)
    rope = rope.reshape(depth, r, MLA_HEADS * MLA_ROPE_DIM)
    return jnp.concatenate([nope, rope, rope_sw], axis=-1).astype(BF16)


def _prepare_w_kv_up(w):
    depth, r, _ = w.shape
    w = w.reshape(depth, r, MLA_HEADS, MLA_NOPE_DIM + MLA_V_DIM)
    kn = w[..., :MLA_NOPE_DIM].reshape(depth, r, MLA_HEADS * MLA_NOPE_DIM)
    vm = w[..., MLA_NOPE_DIM:].reshape(depth, r, MLA_HEADS * MLA_V_DIM)
    return jnp.concatenate([kn, vm], axis=-1).astype(BF16)


def _pad_lanes(x, width=LANES):
    return jnp.pad(x, [(0, 0)] * (x.ndim - 1) + [(0, width - x.shape[-1])])


def kernel(x_prompt, x_sample, cache_fox_k, cache_fox_v, cache_fox_logf, cache_mla_ckv, cache_mla_krope, meta_tokens, ln_in_g, ln_in_b, w_in, b_fgt, g_q_norm, w_q_up, g_kv_norm, w_kv_up, w_proj_fox, w_proj_mla, b_gate, w_out, ln1_g, ln1_b, ln2_g, ln2_b, w_router, b_router, w_exp_gate, w_exp_up, w_exp_down):
    bp, seq, d = x_prompt.shape
    bs, tn, _ = x_sample.shape
    depth = w_in.shape[0]
    past = cache_fox_k.shape[2]
    alpha = (2 * depth) ** 0.25
    t_real = N_META + seq
    tp = -(-t_real // ATTN_Q_TILE) * ATTN_Q_TILE
    n_p = bp * tp
    n_s = bs * tn
    n = n_p + n_s
    n_pad = n
    tk_s = -(-(past + tn) // SAMPLE_KEY_TILE) * SAMPLE_KEY_TILE
    tq_s = SAMPLE_QUERY_TILE
    assert n % ROW_TILE == 0 and ROW_TILE % tn == 0 and tn <= tq_s, (n, tn)
    assert ATTN_Q_TILE == ATTN_TILE == ROW_TILE

    meta = jnp.broadcast_to(meta_tokens[None].astype(F32), (bp, N_META, d))
    xp = jnp.concatenate([meta, x_prompt, jnp.zeros((bp, tp - t_real, d), F32)], axis=1)
    x = jnp.concatenate([xp.reshape(n_p, d), x_sample.reshape(n_s, d), jnp.zeros((n_pad - n, d), F32)], axis=0)
    pos = jnp.concatenate([
        jnp.tile(jnp.arange(tp, dtype=jnp.int32), bp),
        jnp.tile(past + jnp.arange(tn, dtype=jnp.int32), bs),
        jnp.zeros((n_pad - n,), jnp.int32),
    ])
    cos, sin = _rope_tables(pos)

    w_in_r = _prepare_w_in(w_in)
    w_q_r = _prepare_w_q_up(w_q_up)
    w_kv_r = _prepare_w_kv_up(w_kv_up)
    bf_r = _pad_lanes(b_fgt)[:, None, :]
    pa, pb, wo = w_proj_fox.astype(BF16), w_proj_mla.astype(BF16), w_out.astype(BF16)
    wg, wu, wd = w_exp_gate.astype(BF16), w_exp_up.astype(BF16), w_exp_down.astype(BF16)
    wr = _pad_lanes(w_router)
    keep_bf16_bits = lambda a: lax.bitcast_convert_type(
        lax.bitcast_convert_type(a, jnp.uint32) & jnp.uint32(0xFFFF0000), F32
    )
    wr_hi = keep_bf16_bits(wr)
    wr_mid = keep_bf16_bits(wr - wr_hi)
    wr_lo = wr - wr_hi - wr_mid
    wr3 = jnp.concatenate([wr_hi, wr_mid, wr_lo], axis=1).astype(BF16)
    br = _pad_lanes(b_router[None, :])

    zpad = tk_s - past - tn
    kn_c, vmt_c = _kv_up(cache_mla_ckv.reshape(depth, bs * past, -1), w_kv_r)
    kn_c = kn_c.reshape(depth, bs, past, 512)
    vmt_c = jnp.transpose(vmt_c.reshape(depth, 512, bs, past), (0, 2, 1, 3))
    kf_c = cache_fox_k.reshape(depth, bs, past, FOX_WIDTH).astype(BF16)
    vft_c = jnp.transpose(cache_fox_v.reshape(depth, bs, past, FOX_WIDTH), (0, 1, 3, 2)).astype(BF16)
    kr_c = jnp.tile(cache_mla_krope, (1, 1, 1, LANES // MLA_ROPE_DIM)).astype(BF16)
    logf_c = _pad_lanes(cache_fox_logf)

    def with_past(cached, new):
        new = new.reshape(bs, tn, new.shape[-1])
        return jnp.concatenate([cached, new, jnp.zeros((bs, zpad, new.shape[-1]), new.dtype)], axis=1)

    def with_past_t(cached_t, new_t):
        new_t = jnp.transpose(new_t.reshape(512, bs, tn), (1, 0, 2))
        return jnp.concatenate([cached_t, new_t, jnp.zeros((bs, 512, zpad), new_t.dtype)], axis=2)

    def sample_queries_t(a_t):
        a_t = jnp.transpose(a_t[:, n_p:n].reshape(a_t.shape[0], bs, tn), (1, 0, 2))
        return jnp.pad(a_t, ((0, 0), (0, 0), (0, tq_s - tn)))

    prompt = dict(
        n_batch=bp, t_q=tp, t_k=tp, tq=ATTN_Q_TILE, tk=ATTN_TILE, q_off=0, n_meta=N_META, out_rows=ATTN_Q_TILE,
        out=n_pad,
    )
    sample = dict(
        n_batch=bs, t_q=tq_s, t_k=tk_s, tq=tq_s, tk=SAMPLE_KEY_TILE, q_off=past, n_meta=0, out_rows=tn, out_row0=n_p
    )

    x = _input_ln(x, ln_in_g[None, :], ln_in_b[None, :])
    state_widths = (FOX_WIDTH, FOX_WIDTH, FOX_HEADS, 128, MLA_ROPE_DIM)
    states = tuple(jnp.zeros((depth, bp, tp, wd), F32) for wd in state_widths)
    states += tuple(jnp.zeros((depth, bs, tn, wd), F32) for wd in state_widths)
    for l in range(depth):
        pre, states = _pre_attention(
            x, w_in_r[l], bf_r[l], g_q_norm[l][None, :], w_q_r[l], g_kv_norm[l][None, :], w_kv_r[l],
            b_gate[l][None, :], cos, sin,
            layer=l, depth=depth, prompt_dims=(bp, tp), sample_dims=(bs, tn), states=states,
        )
        qx_p, kx_p, edge_p = _forget_bias_operands(pre["logf"], bp, tp, ATTN_TILE)
        first_p = _first_live_key_tile(pre["norm"], edge_p, bp, tp, ATTN_TILE)
        fo = _attention(pre["qft"], qx_p, pre["kf"], kx_p, pre["vft"], fox=True, first=first_p, **prompt)
        mo = _attention(pre["qnt"], pre["qrt"], pre["kn"], pre["kr"], pre["vmt"], fox=False, **prompt)

        logf_s = with_past(logf_c[l], pre["logf"][n_p:n]).reshape(bs * tk_s, LANES)
        qx_s, kx_s, _ = _forget_bias_operands(logf_s, bs, tk_s, SAMPLE_KEY_TILE)
        qx_s = jnp.pad(qx_s[:, :, :, past:past + tn], ((0, 0), (0, 0), (0, 0), (0, tq_s - tn)))
        fo = _attention(
            sample_queries_t(pre["qft"]), qx_s, with_past(kf_c[l], pre["kf"][n_p:n]), kx_s,
            with_past_t(vft_c[l], pre["vft"][:, n_p:n]), fox=True, out=fo, **sample,
        )
        mo = _attention(
            sample_queries_t(pre["qnt"]), sample_queries_t(pre["qrt"]), with_past(kn_c[l], pre["kn"][n_p:n]),
            with_past(kr_c[l], pre["kr"][n_p:n]), with_past_t(vmt_c[l], pre["vmt"][:, n_p:n]), fox=False, out=mo,
            **sample,
        )
        x1, gates = _post_attention(
            x, fo, mo, pre["gate"], pa[l], pb[l], wo[l], ln1_g[l][None, :], ln1_b[l][None, :], wr3, br, alpha
        )
        x = _moe(x1, gates, wg[l], wu[l], wd[l], ln2_g[l][None, :], ln2_b[l][None, :], alpha)

    heads = (FOX_HEADS, FOX_HEAD_DIM)
    k_p, v_p, logf_p, ckv_p, kr_p, k_s, v_s, logf_s, ckv_s, kr_s = states
    y_prompt = x[:n_p].reshape(bp, tp, d)[:, N_META:t_real]
    y_sample = x[n_p:n].reshape(bs, tn, d)
    return (
        y_prompt,
        y_sample,
        k_p[:, :, :t_real].reshape(depth, bp, t_real, *heads),
        v_p[:, :, :t_real].reshape(depth, bp, t_real, *heads),
        logf_p[:, :, :t_real],
        ckv_p[:, :, :t_real],
        kr_p[:, :, :t_real],
        k_s.reshape(depth, bs, tn, *heads),
        v_s.reshape(depth, bs, tn, *heads),
        logf_s,
        ckv_s,
        kr_s,
    )
```

```python
import functools

import jax
import jax.numpy as jnp
import numpy as np
from jax import lax
from jax.experimental import pallas as pl
from jax.experimental.pallas import tpu as pltpu

F32 = jnp.float32
BF16 = jnp.bfloat16

CHUNK = 64
N_META = 16
FOX_HEADS = 8
FOX_HEAD_DIM = 64
FOX_WIDTH = FOX_HEADS * FOX_HEAD_DIM
MLA_HEADS = 8
MLA_NOPE_DIM = 64
MLA_ROPE_DIM = 32
MLA_QK_DIM = MLA_NOPE_DIM + MLA_ROPE_DIM
MLA_V_DIM = 64
N_EXPERTS = 16
N_GROUPS = 4
EXPERTS_PER_GROUP = N_EXPERTS // N_GROUPS
ROPE_THETA = 10000.0
LN_EPS = 1e-5
RMS_EPS = 1e-6
LOG2E = 1.4426950408889634

LANES = 128
ROW_TILE = 256
MOE_ROW_TILE = 1536
ATTN_TILE = 256
ATTN_Q_TILE = 256
ATTN_SLOTS = 4
QK_GROUP = 2
SAMPLE_QUERY_TILE = 128
SAMPLE_KEY_TILE = 384
AUG_GROUP = 8
ONES_ROWS = 16
MASK_VALUE = -1e30
UNDERFLOW_MARGIN = 170.0
NORM_SLACK = 1.01
VMEM_LIMIT = 56 * 1024 * 1024

C_QKV = 0
C_CQ = 3 * FOX_WIDTH
C_CKV = C_CQ + 256
C_LOGF = C_CKV + 128
C_KR = C_LOGF + LANES
C_KRS = C_KR + LANES
C_GATE = C_KRS + LANES


def _params(*sem):
    return pltpu.CompilerParams(dimension_semantics=sem, vmem_limit_bytes=VMEM_LIMIT)


def _layer_norm(x, g, b):
    mu = jnp.mean(x, axis=-1, keepdims=True)
    xc = x - mu
    var = jnp.mean(xc * xc, axis=-1, keepdims=True)
    return xc * lax.rsqrt(var + LN_EPS) * g + b


def _rms_norm(x, g):
    return x * lax.rsqrt(jnp.mean(x * x, axis=-1, keepdims=True) + RMS_EPS) * g


def _dot(a, b):
    return jnp.dot(a, b, preferred_element_type=F32)


def _split3(x):
    hi = x.astype(BF16)
    r1 = x - hi.astype(F32)
    mid = r1.astype(BF16)
    lo = (r1 - mid.astype(F32)).astype(BF16)
    return hi, mid, lo


def _ln_kernel(x_ref, g_ref, b_ref, o_ref):
    o_ref[...] = _layer_norm(x_ref[...], g_ref[...], b_ref[...])


def _input_ln(x, g, b):
    n, d = x.shape
    return pl.pallas_call(
        _ln_kernel,
        grid=(n // ROW_TILE,),
        in_specs=[
            pl.BlockSpec((ROW_TILE, d), lambda i: (i, 0)),
            pl.BlockSpec((1, d), lambda i: (0, 0)),
            pl.BlockSpec((1, d), lambda i: (0, 0)),
        ],
        out_specs=pl.BlockSpec((ROW_TILE, d), lambda i: (i, 0)),
        out_shape=jax.ShapeDtypeStruct((n, d), F32),
        compiler_params=_params("parallel"),
        name="input_ln",
    )(x, g, b)


N_ROW_STATES = 5
N_OPERANDS = 11


def _pre_kernel(x_ref, w_ref, bf_ref, gq_ref, wq_ref, gkv_ref, wkv_ref, bg_ref, cos_ref, sin_ref, *rest,
                fox_scale, mla_scale, n_prompt_tiles):
    outs = rest[len(rest) - N_OPERANDS - 2 * N_ROW_STATES:]
    (qft_ref, kf_ref, vft_ref, logf_ref, qnt_ref, qrt_ref, kr_ref, kn_ref, vmt_ref, gate_ref,
     norm_ref) = outs[:N_OPERANDS]
    prompt_refs = outs[N_OPERANDS:N_OPERANDS + N_ROW_STATES]
    sample_refs = outs[N_OPERANDS + N_ROW_STATES:]
    xb = x_ref[...].astype(BF16)
    cos = cos_ref[...]
    sin = sin_ref[...]

    qkv = _dot(xb, w_ref[:, C_QKV:C_CQ])
    q = qkv[:, 0:FOX_WIDTH] * fox_scale
    qft_ref[...] = q.T.astype(BF16)
    k = qkv[:, FOX_WIDTH:2 * FOX_WIDTH]
    v = qkv[:, 2 * FOX_WIDTH:3 * FOX_WIDTH]
    kf_ref[...] = k.astype(BF16)
    vft_ref[...] = v.T.astype(BF16)

    feat = lax.broadcasted_iota(jnp.int32, (FOX_WIDTH, LANES), 0)
    head = lax.broadcasted_iota(jnp.int32, (FOX_WIDTH, LANES), 1)
    head_of = jnp.where(feat // FOX_HEAD_DIM == head, 1.0, 0.0).astype(BF16)
    top = lambda a: jnp.sqrt(jnp.max(_dot((a * a).astype(BF16), head_of), axis=0, keepdims=True))
    norm_ref[...] = jnp.broadcast_to(jnp.concatenate([top(q), top(k)], axis=1), norm_ref.shape)

    z = _dot(xb, w_ref[:, C_LOGF:C_KR]) + bf_ref[...]
    logf = jnp.minimum(z, 0.0) - jnp.log1p(jnp.exp(-jnp.abs(z)))
    logf_ref[...] = logf

    cq = _rms_norm(_dot(xb, w_ref[:, C_CQ:C_CKV]), gq_ref[...]).astype(BF16)
    qm = _dot(cq, wq_ref[...])
    qnt_ref[...] = (qm[:, 0:512] * mla_scale).T.astype(BF16)
    cos2 = jnp.concatenate([cos, cos], axis=1)
    sin2 = jnp.concatenate([sin, sin], axis=1)
    qrt_ref[...] = ((qm[:, 512:768] * cos2 + qm[:, 768:1024] * sin2) * mla_scale).T.astype(BF16)

    ckv = _rms_norm(_dot(xb, w_ref[:, C_CKV:C_LOGF]), gkv_ref[...])
    kv = _dot(ckv.astype(BF16), wkv_ref[...])
    kn_ref[...] = kv[:, 0:512].astype(BF16)
    vmt_ref[...] = kv[:, 512:1024].T.astype(BF16)
    kr = _dot(xb, w_ref[:, C_KR:C_KRS]) * cos + _dot(xb, w_ref[:, C_KRS:C_GATE]) * sin
    kr_ref[...] = kr.astype(BF16)

    g = _dot(xb, w_ref[:, C_GATE:]) + bg_ref[...]
    gate_ref[...] = jax.nn.sigmoid(g).astype(BF16)

    states = (k, v, logf[:, :FOX_HEADS], ckv, kr[:, :MLA_ROPE_DIM])
    i = pl.program_id(0)

    @pl.when(i < n_prompt_tiles)
    def _():
        for ref, val in zip(prompt_refs, states):
            ref[...] = val

    @pl.when(i >= n_prompt_tiles)
    def _():
        for ref, val in zip(sample_refs, states):
            ref[...] = val.reshape(ref.shape)


def _pre_attention(x, w, bf, gq, wq, gkv, wkv, bg, cos, sin, *, layer, depth, prompt_dims, sample_dims, states):
    n, d = x.shape
    tm = ROW_TILE
    bp, tp = prompt_dims
    bs, tn = sample_dims
    tiles_per_stream = tp // tm
    n_prompt_tiles = bp * tiles_per_stream
    streams_per_tile = tm // tn
    row = lambda width: pl.BlockSpec((tm, width), lambda i: (i, 0))
    col = lambda height: pl.BlockSpec((height, tm), lambda i: (0, i))
    full = lambda a: pl.BlockSpec(a.shape, lambda i: (0,) * a.ndim)
    outs = [
        ("qft", 512, BF16, True), ("kf", 512, BF16, False), ("vft", 512, BF16, True), ("logf", LANES, F32, False),
        ("qnt", 512, BF16, True), ("qrt", 256, BF16, True), ("kr", LANES, BF16, False), ("kn", 512, BF16, False),
        ("vmt", 512, BF16, True), ("gate", 2 * d, BF16, False),
    ]
    state_widths = (FOX_WIDTH, FOX_WIDTH, FOX_HEADS, 128, MLA_ROPE_DIM)

    def prompt_block(i):
        ip = jnp.minimum(i, n_prompt_tiles - 1)
        return layer, ip // tiles_per_stream, ip % tiles_per_stream, 0

    def sample_block(i):
        return layer, jnp.maximum(i - n_prompt_tiles, 0), 0, 0

    out_specs = [col(wd) if tr else row(wd) for _, wd, _, tr in outs]
    out_shape = [jax.ShapeDtypeStruct((wd, n) if tr else (n, wd), dt) for _, wd, dt, tr in outs]
    outs.append(("norm", 2 * LANES, F32, False))
    out_specs.append(pl.BlockSpec((8, 2 * LANES), lambda i: (i, 0)))
    out_shape.append(jax.ShapeDtypeStruct((8 * (n // tm), 2 * LANES), F32))
    assert len(outs) == N_OPERANDS
    out_specs += [pl.BlockSpec((None, None, tm, wd), prompt_block) for wd in state_widths]
    out_specs += [pl.BlockSpec((None, streams_per_tile, tn, wd), sample_block) for wd in state_widths]
    out_shape += [jax.ShapeDtypeStruct((depth, bp, tp, wd), F32) for wd in state_widths]
    out_shape += [jax.ShapeDtypeStruct((depth, bs, tn, wd), F32) for wd in state_widths]
    args = [x, w, bf, gq, wq, gkv, wkv, bg, cos, sin]
    in_specs = [row(d), full(w), full(bf), full(gq), full(wq), full(gkv), full(wkv), full(bg), row(LANES), row(LANES)]
    aliases = {len(args) + j: len(outs) + j for j in range(len(states))}
    args += list(states)
    in_specs += [pl.BlockSpec(memory_space=pl.ANY)] * len(states)
    res = pl.pallas_call(
        functools.partial(
            _pre_kernel, fox_scale=FOX_HEAD_DIM ** -0.5 * LOG2E, mla_scale=MLA_QK_DIM ** -0.5 * LOG2E,
            n_prompt_tiles=n_prompt_tiles,
        ),
        grid=(n // tm,),
        in_specs=in_specs,
        out_specs=out_specs,
        out_shape=out_shape,
        input_output_aliases=aliases,
        compiler_params=_params("arbitrary"),
        name="pre_attention",
    )(*args)
    return {name: r for (name, _, _, _), r in zip(outs, res)}, tuple(res[len(outs):])


def _kvup_kernel(c_ref, w_ref, kn_ref, vmt_ref):
    kv = _dot(c_ref[0].astype(BF16), w_ref[0])
    kn_ref[0] = kv[:, 0:512].astype(BF16)
    vmt_ref[0] = kv[:, 512:1024].T.astype(BF16)


def _kv_up(ckv, wkv):
    depth, n, r = ckv.shape
    tm = ROW_TILE
    return pl.pallas_call(
        _kvup_kernel,
        grid=(depth, n // tm),
        in_specs=[pl.BlockSpec((1, tm, r), lambda l, i: (l, i, 0)), pl.BlockSpec((1, r, 1024), lambda l, i: (l, 0, 0))],
        out_specs=[pl.BlockSpec((1, tm, 512), lambda l, i: (l, i, 0)), pl.BlockSpec((1, 512, tm), lambda l, i: (l, 0, i))],
        out_shape=[jax.ShapeDtypeStruct((depth, n, 512), BF16), jax.ShapeDtypeStruct((depth, 512, n), BF16)],
        compiler_params=_params("parallel", "parallel"),
        name="kv_up_cached",
    )(ckv, wkv)


def _bias_layout():
    sel_q = np.zeros((3 * LANES, 4 * LANES), np.float32)
    sel_k = np.zeros((3 * LANES, 4 * LANES), np.float32)
    one_q = np.zeros((1, 4 * LANES), np.float32)
    one_k = np.zeros((1, 4 * LANES), np.float32)
    for h in range(FOX_HEADS):
        base = (h // 2) * LANES + (h % 2) * AUG_GROUP
        for piece in range(3):
            sel_q[piece * LANES + h, base + piece] = 1.0
            sel_k[piece * LANES + h, base + 3 + piece] = -1.0
            one_q[0, base + 3 + piece] = 1.0
            one_k[0, base + piece] = 1.0
    return (jnp.asarray(sel_q, BF16), jnp.asarray(one_q, F32), jnp.asarray(sel_k, BF16), jnp.asarray(one_k, F32))


def _cumsum_kernel(x_ref, sq_ref, oq_ref, sk_ref, ok_ref, qxt_ref, kx_ref, edge_ref, carry_ref, *, tt):
    @pl.when(pl.program_id(1) == 0)
    def _():
        carry_ref[...] = jnp.zeros_like(carry_ref)

    r = lax.broadcasted_iota(jnp.int32, (tt, tt), 0)
    c = lax.broadcasted_iota(jnp.int32, (tt, tt), 1)
    tri = jnp.where(c <= r, 1.0, 0.0).astype(BF16)
    hi, mid, lo = _split3(x_ref[...])
    cum = _dot(tri, lo) + _dot(tri, mid) + _dot(tri, hi) + carry_ref[...]
    carry_ref[...] = cum[tt - 1:tt, :]
    cum2 = cum * LOG2E
    edge_ref[...] = jnp.broadcast_to(jnp.concatenate([cum2[0:1], cum2[tt - 1:tt]], axis=1), edge_ref.shape)
    pieces = jnp.concatenate(_split3(cum2), axis=1)
    kx = (_dot(pieces, sk_ref[...]) + ok_ref[...]).astype(BF16)
    qx = _dot(pieces, sq_ref[...]) + oq_ref[...]
    for p in range(FOX_HEADS // 2):
        kx_ref[p] = kx[:, p * LANES:(p + 1) * LANES]
        qxt_ref[p] = qx[:, p * LANES:(p + 1) * LANES].T.astype(BF16)


def _forget_bias_operands(logf, n_seq, t, tt):
    w = logf.shape[-1]
    tiles = t // tt
    consts = _bias_layout()
    pairs = FOX_HEADS // 2
    return pl.pallas_call(
        functools.partial(_cumsum_kernel, tt=tt),
        grid=(n_seq, tiles),
        in_specs=[pl.BlockSpec((tt, w), lambda b, i: (b * tiles + i, 0))]
        + [pl.BlockSpec(c.shape, lambda b, i: (0, 0)) for c in consts],
        out_specs=[
            pl.BlockSpec((None, pairs, LANES, tt), lambda b, i: (b, 0, 0, i)),
            pl.BlockSpec((None, pairs, tt, LANES), lambda b, i: (b, 0, i, 0)),
            pl.BlockSpec((8, 2 * w), lambda b, i: (b * tiles + i, 0)),
        ],
        out_shape=[
            jax.ShapeDtypeStruct((n_seq, pairs, LANES, t), BF16),
            jax.ShapeDtypeStruct((n_seq, pairs, t, LANES), BF16),
            jax.ShapeDtypeStruct((8 * n_seq * tiles, 2 * w), F32),
        ],
        scratch_shapes=[pltpu.VMEM((1, w), F32)],
        compiler_params=_params("parallel", "arbitrary"),
        name="cumsum_time",
    )(logf, *consts)


def _chunk_end(pos, n_meta):
    body = n_meta + ((pos - n_meta) // CHUNK) * CHUNK + CHUNK - 1
    if n_meta == 0:
        return body
    return jnp.where(pos < n_meta, n_meta - 1, body)


def _attn_kernel(first_ref, qt_ref, qxt_ref, k_ref, kx_ref, vt_ref, *rest, fox, tq, tk, q_off, n_meta, n_ktiles):
    o_ref, *slots = rest[len(rest) - ATTN_SLOTS - 1:]
    p = pl.program_id(1)
    i = pl.program_id(2)
    q_lo = q_off + i * tq
    q_hi = q_lo + tq - 1
    if fox:
        vis_lo, vis_hi = q_lo, q_hi
    else:
        vis_lo, vis_hi = _chunk_end(q_lo, n_meta), _chunk_end(q_hi, n_meta)
    n_tot = jnp.minimum(vis_hi // tk + 1, n_ktiles)
    n_full = jnp.minimum((vis_lo + 1) // tk, n_tot - 1)
    first = jnp.clip(first_ref[pl.program_id(0), p, i], 0, n_full)
    n_tot = n_tot - first
    n_full = n_full - first

    q_pos = q_lo + lax.broadcasted_iota(jnp.int32, (1, tq), 1)
    row = lax.broadcasted_iota(jnp.int32, (LANES, 1), 0)
    qt2 = qt_ref[...]
    qxt2 = qxt_ref[...]
    cols = []
    for hh in range(2):
        x_group = row // AUG_GROUP == hh if fox else row // MLA_ROPE_DIM == 2 * (p % 2) + hh
        cols.append(jnp.concatenate([
            jnp.where(row // FOX_HEAD_DIM == hh, qt2, jnp.zeros_like(qt2)),
            jnp.where(x_group, qxt2, jnp.zeros_like(qxt2)),
        ], axis=0))
    qcbt = jnp.concatenate(cols, axis=1)
    last_visible = q_pos if fox else _chunk_end(q_pos, n_meta)
    last_visible = jnp.concatenate([last_visible, last_visible], axis=1)
    key_iota = lax.broadcasted_iota(jnp.int32, (tk, 2 * tq), 0)
    ones = jnp.ones((ONES_ROWS, tk), BF16)
    n_slots = len(slots)

    def key_rows(j):
        return pl.ds(pl.multiple_of(jnp.minimum(first + j, n_ktiles - 1) * tk, tk), tk)

    def produce(slot_refs, j, masked):
        kcs = []
        for r in range(len(slot_refs)):
            ks = key_rows(j + r)
            kcs.append(jnp.concatenate([k_ref[ks, :], kx_ref[ks, :]], axis=1))
        st_all = _dot(jnp.concatenate(kcs, axis=0), qcbt)
        cmax = []
        for r, slot_ref in enumerate(slot_refs):
            st = st_all[r * tk:(r + 1) * tk]
            if masked:
                st = jnp.where(key_iota <= last_visible - (first + j + r) * tk, st, MASK_VALUE)
            slot_ref[...] = st
            cmax.append(jnp.max(st, axis=0, keepdims=True))
        return cmax

    def accumulate(j, pt, alpha, accs):
        ks = key_rows(j)
        new = []
        for hh in range(2):
            va = jnp.concatenate([vt_ref[FOX_HEAD_DIM * hh:FOX_HEAD_DIM * (hh + 1), ks], ones], axis=0)
            sl = slice(hh * tq, (hh + 1) * tq)
            new.append(alpha[:, sl] * accs[hh] + _dot(va, pt[:, sl]))
        return tuple(new)

    def step(u, carry, mode):
        m, cmax, accs = carry
        cmax = list(cmax)
        for r0 in range(0, n_slots, QK_GROUP):
            group = range(r0, r0 + QK_GROUP)
            pts, alphas = [], []
            for r in group:
                m_new = jnp.maximum(m, cmax[r])
                pts.append(jnp.exp2(slots[r][...] - m_new).astype(BF16))
                alphas.append(jnp.exp2(m - m_new))
                m = m_new
            if mode != "drain":
                cmax[r0:r0 + QK_GROUP] = produce([slots[r] for r in group], n_slots * (u + 1) + r0, mode == "masked")
            for r, pt, alpha in zip(group, pts, alphas):
                accs = accumulate(n_slots * u + r, pt, alpha, accs)
        return m, tuple(cmax), accs

    def two_steps(v, carry):
        return step(2 * v + 1, step(2 * v, carry, "plain"), "plain")

    m0 = jnp.full((1, 2 * tq), MASK_VALUE, F32)
    a0 = jnp.zeros((FOX_HEAD_DIM + ONES_ROWS, tq), F32)
    carry = (m0, tuple(produce(list(slots), 0, True)), (a0, a0))
    n_steps = (n_tot + n_slots - 1) // n_slots
    n_plain = jnp.maximum(n_full - n_slots, 0) // n_slots
    carry = lax.fori_loop(0, n_plain // 2, two_steps, carry)
    carry = lax.fori_loop(n_plain - n_plain % 2, n_plain, functools.partial(step, mode="plain"), carry)
    carry = lax.fori_loop(n_plain, n_steps - 1, functools.partial(step, mode="masked"), carry)
    _, _, accs = step(n_steps - 1, carry, "drain")
    out_t = jnp.concatenate(
        [acc[0:FOX_HEAD_DIM] / acc[FOX_HEAD_DIM:FOX_HEAD_DIM + 1] for acc in accs], axis=0
    )
    o_ref[...] = out_t.T[:o_ref.shape[0]].astype(o_ref.dtype)


def _attention(qt, qxt, k, kx, vt, *, fox, n_batch, t_q, t_k, tq, tk, q_off, n_meta, out_rows, out_row0=0, out=None,
               first=None):
    nq = t_q // tq

    def time_major(a, rows_of, t, tile):
        tiles = t // tile
        if a.ndim == 2:
            return pl.BlockSpec((LANES, tile), lambda bb, p, i, _: (rows_of(p), bb * tiles + i % tiles))
        return pl.BlockSpec((None, LANES, tile), lambda bb, p, i, _: (bb, rows_of(p), i % tiles))

    def row_major(a, col_of):
        if a.ndim == 2:
            return pl.BlockSpec((t_k, LANES), lambda bb, p, i, _: (bb, col_of(p)))
        return pl.BlockSpec((None, t_k, LANES), lambda bb, p, i, _: (bb, 0, col_of(p)))

    qspec = time_major(qt, lambda p: p, t_q, tq)
    kspec = row_major(k, lambda p: p)
    if fox:
        qxspec = pl.BlockSpec((None, None, LANES, tq), lambda bb, p, i, _: (bb, p, 0, i))
        kxspec = pl.BlockSpec((None, None, t_k, LANES), lambda bb, p, i, _: (bb, p, 0, 0))
    else:
        qxspec = time_major(qxt, lambda p: p // 2, t_q, tq)
        kxspec = row_major(kx, lambda p: 0)
    vspec = time_major(vt, lambda p: p, t_k, t_k)
    if first is None:
        first = jnp.zeros((n_batch, 4, nq), jnp.int32)
    args = [first, qt, qxt, k, kx, vt]
    in_specs = [qspec, qxspec, kspec, kxspec, vspec]
    if isinstance(out, int):
        n_rows, aliases = out, {}
    else:
        n_rows, aliases = out.shape[0], {len(args): 0}
        args.append(out)
        in_specs.append(pl.BlockSpec(memory_space=pl.ANY))
    block0 = out_row0 // out_rows
    return pl.pallas_call(
        functools.partial(_attn_kernel, fox=fox, tq=tq, tk=tk, q_off=q_off, n_meta=n_meta, n_ktiles=t_k // tk),
        grid_spec=pltpu.PrefetchScalarGridSpec(
            num_scalar_prefetch=1,
            grid=(n_batch, 4, nq),
            in_specs=in_specs,
            out_specs=pl.BlockSpec((out_rows, LANES), lambda bb, p, i, _: (block0 + bb * nq + i, p)),
            scratch_shapes=[pltpu.VMEM((tk, 2 * tq), F32)] * ATTN_SLOTS,
        ),
        out_shape=jax.ShapeDtypeStruct((n_rows, 512), BF16),
        input_output_aliases=aliases,
        compiler_params=_params("parallel", "parallel", "arbitrary"),
        name="fox_attention" if fox else "mla_attention",
    )(*args)


def _first_live_key_tile(norm, edge, n_seq, t, tile):
    tiles = t // tile
    h = FOX_HEADS
    per_tile = lambda a: a[0:8 * n_seq * tiles:8].reshape(n_seq, tiles, 2 * LANES)
    qmax = per_tile(norm)[..., :h] * NORM_SLACK
    kmax = per_tile(norm)[..., LANES:LANES + h] * NORM_SLACK
    cum_q = per_tile(edge)[..., :h]
    cum_k = per_tile(edge)[..., LANES:LANES + h]
    bound = qmax[:, :, None] * lax.cummax(kmax, axis=1)[:, None] + cum_q[:, :, None] - cum_k[:, None]
    own = -qmax * kmax
    dead = bound < own[:, :, None] - UNDERFLOW_MARGIN
    dead = jnp.all(dead.reshape(n_seq, tiles, tiles, h // 2, 2), axis=-1)
    first = jnp.sum(jnp.cumprod(dead.astype(jnp.int32), axis=2), axis=2)
    return jnp.transpose(first, (0, 2, 1))


def _group_partner(x, lane, d, width):
    fwd = pltpu.roll(x, LANES - d, 1)
    back = pltpu.roll(x, width - d, 1)
    return jnp.where(lane % width + d < width, fwd, back)


def _route(logits, b_router, lane):
    scores = jax.nn.sigmoid(logits)
    sel = scores + b_router
    e = EXPERTS_PER_GROUP
    rank = jnp.zeros(sel.shape, jnp.int32)
    for d in range(1, e):
        other = _group_partner(sel, lane, d, e)
        other_first = lane % e + d >= e
        beats = (other > sel) | ((other == sel) & other_first)
        rank = rank + beats.astype(jnp.int32)
    top2 = rank < 2
    kept = jnp.where(top2, sel, 0.0)
    gscore = kept
    for d in range(1, e):
        gscore = gscore + _group_partner(kept, lane, d, e)
    grank = jnp.zeros_like(rank)
    for gstep in range(1, N_GROUPS):
        d = gstep * e
        other = _group_partner(gscore, lane, d, N_EXPERTS)
        other_first = lane % N_EXPERTS + d >= N_EXPERTS
        beats = (other > gscore) | ((other == gscore) & other_first)
        grank = grank + beats.astype(jnp.int32)
    chosen = top2 & (grank == 0) & (lane < N_EXPERTS)
    w = jnp.where(chosen, scores, 0.0)
    return w / jnp.sum(w, axis=1, keepdims=True)


def _post_kernel(x_ref, fo_ref, mo_ref, gate_ref, pa_ref, pb_ref, wo_ref, g1_ref, b1_ref, wr_ref, br_ref,
                 x1_ref, gates_ref, *, alpha):
    d = x_ref.shape[1]
    merged = gate_ref[:, 0:d].astype(F32) * _dot(fo_ref[...], pa_ref[...])
    merged = merged + gate_ref[:, d:2 * d].astype(F32) * _dot(mo_ref[...], pb_ref[...])
    mix = _dot(merged.astype(BF16), wo_ref[...])
    x1 = _layer_norm(alpha * x_ref[...] + mix, g1_ref[...], b1_ref[...])
    x1_ref[...] = x1
    hi, mid, lo = _split3(x1)
    a = _dot(hi, wr_ref[:, 0:2 * LANES])
    b = _dot(mid, wr_ref[:, 0:2 * LANES])
    logits = _dot(lo, wr_ref[:, 0:LANES]) + _dot(hi, wr_ref[:, 2 * LANES:3 * LANES])
    logits = logits + b[:, LANES:] + a[:, LANES:] + b[:, :LANES] + a[:, :LANES]
    lane = lax.broadcasted_iota(jnp.int32, logits.shape, 1)
    gates_ref[...] = _route(logits, br_ref[...], lane)


def _post_attention(x, fo, mo, gate, pa, pb, wo, g1, b1, wr3, br, alpha):
    n, d = x.shape
    tm = ROW_TILE
    row = lambda width: pl.BlockSpec((tm, width), lambda i: (i, 0))
    full = lambda a: pl.BlockSpec(a.shape, lambda i: (0,) * a.ndim)
    return pl.pallas_call(
        functools.partial(_post_kernel, alpha=alpha),
        grid=(n // tm,),
        in_specs=[row(d), row(512), row(512), row(2 * d), full(pa), full(pb), full(wo), full(g1), full(b1), full(wr3), full(br)],
        out_specs=[row(d), row(LANES)],
        out_shape=[jax.ShapeDtypeStruct((n, d), F32), jax.ShapeDtypeStruct((n, LANES), F32)],
        compiler_params=_params("parallel"),
        name="post_attention",
    )(x, fo, mo, gate, pa, pb, wo, g1, b1, wr3, br)


def _moe_kernel(x_ref, gates_ref, wg_ref, wu_ref, wd_ref, g2_ref, b2_ref, o_ref, xb_ref, acc_ref, *, alpha):
    e = pl.program_id(1)

    @pl.when(e == 0)
    def _():
        xb_ref[...] = x_ref[...].astype(BF16)
        acc_ref[...] = jnp.zeros_like(acc_ref)

    lane = lax.broadcasted_iota(jnp.int32, (1, LANES), 1)
    gate = jnp.sum(jnp.where(lane == e, gates_ref[...], 0.0), axis=1, keepdims=True)
    xb = xb_ref[...]
    hid = jax.nn.silu(_dot(xb, wg_ref[0])) * _dot(xb, wu_ref[0])
    acc_ref[...] += _dot((hid * gate).astype(BF16), wd_ref[0])

    @pl.when(e == pl.num_programs(1) - 1)
    def _():
        o_ref[...] = _layer_norm(alpha * x_ref[...] + acc_ref[...], g2_ref[...], b2_ref[...])


def _moe(x, gates, wg, wu, wd, g2, b2, alpha):
    n, d = x.shape
    n_exp, _, f = wg.shape
    tm = MOE_ROW_TILE if n % MOE_ROW_TILE == 0 else ROW_TILE
    return pl.pallas_call(
        functools.partial(_moe_kernel, alpha=alpha),
        grid=(n // tm, n_exp),
        in_specs=[
            pl.BlockSpec((tm, d), lambda i, e: (i, 0)),
            pl.BlockSpec((tm, LANES), lambda i, e: (i, 0)),
            pl.BlockSpec((1, d, f), lambda i, e: (e, 0, 0)),
            pl.BlockSpec((1, d, f), lambda i, e: (e, 0, 0)),
            pl.BlockSpec((1, f, d), lambda i, e: (e, 0, 0)),
            pl.BlockSpec((1, d), lambda i, e: (0, 0)),
            pl.BlockSpec((1, d), lambda i, e: (0, 0)),
        ],
        out_specs=pl.BlockSpec((tm, d), lambda i, e: (i, 0)),
        out_shape=jax.ShapeDtypeStruct((n, d), F32),
        scratch_shapes=[pltpu.VMEM((tm, d), BF16), pltpu.VMEM((tm, d), F32)],
        compiler_params=_params("parallel", "arbitrary"),
        name="moe_ln2",
    )(x, gates, wg, wu, wd, g2, b2)


def _rope_tables(pos):
    half = MLA_ROPE_DIM // 2
    inv_freq = ROPE_THETA ** (-jnp.arange(half, dtype=F32) / half)
    ang = pos.astype(F32)[:, None] * inv_freq[None, :]
    cos, sin = jnp.cos(ang), jnp.sin(ang)
    cos32 = jnp.concatenate([cos, cos], axis=1)
    sin32 = jnp.concatenate([-sin, sin], axis=1)
    return jnp.tile(cos32, (1, LANES // MLA_ROPE_DIM)), jnp.tile(sin32, (1, LANES // MLA_ROPE_DIM))


def _swap_halves(w):
    half = MLA_ROPE_DIM // 2
    return jnp.concatenate([w[..., half:], w[..., :half]], axis=-1)


def _prepare_w_in(w_in):
    depth, d, _ = w_in.shape
    o = [0]
    for sz in (FOX_WIDTH, FOX_WIDTH, FOX_WIDTH, FOX_HEADS, 256, 128, MLA_ROPE_DIM, 2 * d):
        o.append(o[-1] + sz)
    qkv = w_in[..., o[0]:o[3]]
    wf = w_in[..., o[3]:o[4]]
    wcq = w_in[..., o[4]:o[5]]
    wckv = w_in[..., o[5]:o[6]]
    wkr = w_in[..., o[6]:o[7]]
    wg = w_in[..., o[7]:o[8]]
    reps = LANES // MLA_ROPE_DIM
    wf_pad = jnp.concatenate([wf, jnp.zeros((depth, d, LANES - FOX_HEADS), w_in.dtype)], axis=-1)
    return jnp.concatenate(
        [qkv, wcq, wckv, wf_pad, jnp.tile(wkr, (1, 1, reps)), jnp.tile(_swap_halves(wkr), (1, 1, reps)), wg], axis=-1
    ).astype(BF16)


def _prepare_w_q_up(w):
    depth, r, _ = w.shape
    w = w.reshape(depth, r, MLA_HEADS, MLA_QK_DIM)
    nope = w[..., :MLA_NOPE_DIM].reshape(depth, r, MLA_HEADS * MLA_NOPE_DIM)
    rope = w[..., MLA_NOPE_DIM:]
    rope_sw = _swap_halves(rope).reshape(depth, r, MLA_HEADS * MLA_ROPE_DIM)
    rope = rope.reshape(depth, r, MLA_HEADS * MLA_ROPE_DIM)
    return jnp.concatenate([nope, rope, rope_sw], axis=-1).astype(BF16)


def _prepare_w_kv_up(w):
    depth, r, _ = w.shape
    w = w.reshape(depth, r, MLA_HEADS, MLA_NOPE_DIM + MLA_V_DIM)
    kn = w[..., :MLA_NOPE_DIM].reshape(depth, r, MLA_HEADS * MLA_NOPE_DIM)
    vm = w[..., MLA_NOPE_DIM:].reshape(depth, r, MLA_HEADS * MLA_V_DIM)
    return jnp.concatenate([kn, vm], axis=-1).astype(BF16)


def _pad_lanes(x, width=LANES):
    return jnp.pad(x, [(0, 0)] * (x.ndim - 1) + [(0, width - x.shape[-1])])


def kernel(x_prompt, x_sample, cache_fox_k, cache_fox_v, cache_fox_logf, cache_mla_ckv, cache_mla_krope, meta_tokens, ln_in_g, ln_in_b, w_in, b_fgt, g_q_norm, w_q_up, g_kv_norm, w_kv_up, w_proj_fox, w_proj_mla, b_gate, w_out, ln1_g, ln1_b, ln2_g, ln2_b, w_router, b_router, w_exp_gate, w_exp_up, w_exp_down):
    bp, seq, d = x_prompt.shape
    bs, tn, _ = x_sample.shape
    depth = w_in.shape[0]
    past = cache_fox_k.shape[2]
    alpha = (2 * depth) ** 0.25
    t_real = N_META + seq
    tp = -(-t_real // ATTN_Q_TILE) * ATTN_Q_TILE
    n_p = bp * tp
    n_s = bs * tn
    n = n_p + n_s
    n_pad = n
    tk_s = -(-(past + tn) // SAMPLE_KEY_TILE) * SAMPLE_KEY_TILE
    tq_s = SAMPLE_QUERY_TILE
    assert n % ROW_TILE == 0 and ROW_TILE % tn == 0 and tn <= tq_s, (n, tn)
    assert ATTN_Q_TILE == ATTN_TILE == ROW_TILE

    meta = jnp.broadcast_to(meta_tokens[None].astype(F32), (bp, N_META, d))
    xp = jnp.concatenate([meta, x_prompt, jnp.zeros((bp, tp - t_real, d), F32)], axis=1)
    x = jnp.concatenate([xp.reshape(n_p, d), x_sample.reshape(n_s, d), jnp.zeros((n_pad - n, d), F32)], axis=0)
    pos = jnp.concatenate([
        jnp.tile(jnp.arange(tp, dtype=jnp.int32), bp),
        jnp.tile(past + jnp.arange(tn, dtype=jnp.int32), bs),
        jnp.zeros((n_pad - n,), jnp.int32),
    ])
    cos, sin = _rope_tables(pos)

    w_in_r = _prepare_w_in(w_in)
    w_q_r = _prepare_w_q_up(w_q_up)
    w_kv_r = _prepare_w_kv_up(w_kv_up)
    bf_r = _pad_lanes(b_fgt)[:, None, :]
    pa, pb, wo = w_proj_fox.astype(BF16), w_proj_mla.astype(BF16), w_out.astype(BF16)
    wg, wu, wd = w_exp_gate.astype(BF16), w_exp_up.astype(BF16), w_exp_down.astype(BF16)
    wr = _pad_lanes(w_router)
    keep_bf16_bits = lambda a: lax.bitcast_convert_type(
        lax.bitcast_convert_type(a, jnp.uint32) & jnp.uint32(0xFFFF0000), F32
    )
    wr_hi = keep_bf16_bits(wr)
    wr_mid = keep_bf16_bits(wr - wr_hi)
    wr_lo = wr - wr_hi - wr_mid
    wr3 = jnp.concatenate([wr_hi, wr_mid, wr_lo], axis=1).astype(BF16)
    br = _pad_lanes(b_router[None, :])

    zpad = tk_s - past - tn
    kn_c, vmt_c = _kv_up(cache_mla_ckv.reshape(depth, bs * past, -1), w_kv_r)
    kn_c = kn_c.reshape(depth, bs, past, 512)
    vmt_c = jnp.transpose(vmt_c.reshape(depth, 512, bs, past), (0, 2, 1, 3))
    kf_c = cache_fox_k.reshape(depth, bs, past, FOX_WIDTH).astype(BF16)
    vft_c = jnp.transpose(cache_fox_v.reshape(depth, bs, past, FOX_WIDTH), (0, 1, 3, 2)).astype(BF16)
    kr_c = jnp.tile(cache_mla_krope, (1, 1, 1, LANES // MLA_ROPE_DIM)).astype(BF16)
    logf_c = _pad_lanes(cache_fox_logf)

    def with_past(cached, new):
        new = new.reshape(bs, tn, new.shape[-1])
        return jnp.concatenate([cached, new, jnp.zeros((bs, zpad, new.shape[-1]), new.dtype)], axis=1)

    def with_past_t(cached_t, new_t):
        new_t = jnp.transpose(new_t.reshape(512, bs, tn), (1, 0, 2))
        return jnp.concatenate([cached_t, new_t, jnp.zeros((bs, 512, zpad), new_t.dtype)], axis=2)

    def sample_queries_t(a_t):
        a_t = jnp.transpose(a_t[:, n_p:n].reshape(a_t.shape[0], bs, tn), (1, 0, 2))
        return jnp.pad(a_t, ((0, 0), (0, 0), (0, tq_s - tn)))

    prompt = dict(
        n_batch=bp, t_q=tp, t_k=tp, tq=ATTN_Q_TILE, tk=ATTN_TILE, q_off=0, n_meta=N_META, out_rows=ATTN_Q_TILE,
        out=n_pad,
    )
    sample = dict(
        n_batch=bs, t_q=tq_s, t_k=tk_s, tq=tq_s, tk=SAMPLE_KEY_TILE, q_off=past, n_meta=0, out_rows=tn, out_row0=n_p
    )

    x = _input_ln(x, ln_in_g[None, :], ln_in_b[None, :])
    state_widths = (FOX_WIDTH, FOX_WIDTH, FOX_HEADS, 128, MLA_ROPE_DIM)
    states = tuple(jnp.zeros((depth, bp, tp, wd), F32) for wd in state_widths)
    states += tuple(jnp.zeros((depth, bs, tn, wd), F32) for wd in state_widths)
    for l in range(depth):
        pre, states = _pre_attention(
            x, w_in_r[l], bf_r[l], g_q_norm[l][None, :], w_q_r[l], g_kv_norm[l][None, :], w_kv_r[l],
            b_gate[l][None, :], cos, sin,
            layer=l, depth=depth, prompt_dims=(bp, tp), sample_dims=(bs, tn), states=states,
        )
        qx_p, kx_p, edge_p = _forget_bias_operands(pre["logf"], bp, tp, ATTN_TILE)
        first_p = _first_live_key_tile(pre["norm"], edge_p, bp, tp, ATTN_TILE)
        fo = _attention(pre["qft"], qx_p, pre["kf"], kx_p, pre["vft"], fox=True, first=first_p, **prompt)
        mo = _attention(pre["qnt"], pre["qrt"], pre["kn"], pre["kr"], pre["vmt"], fox=False, **prompt)

        logf_s = with_past(logf_c[l], pre["logf"][n_p:n]).reshape(bs * tk_s, LANES)
        qx_s, kx_s, _ = _forget_bias_operands(logf_s, bs, tk_s, SAMPLE_KEY_TILE)
        qx_s = jnp.pad(qx_s[:, :, :, past:past + tn], ((0, 0), (0, 0), (0, 0), (0, tq_s - tn)))
        fo = _attention(
            sample_queries_t(pre["qft"]), qx_s, with_past(kf_c[l], pre["kf"][n_p:n]), kx_s,
            with_past_t(vft_c[l], pre["vft"][:, n_p:n]), fox=True, out=fo, **sample,
        )
        mo = _attention(
            sample_queries_t(pre["qnt"]), sample_queries_t(pre["qrt"]), with_past(kn_c[l], pre["kn"][n_p:n]),
            with_past(kr_c[l], pre["kr"][n_p:n]), with_past_t(vmt_c[l], pre["vmt"][:, n_p:n]), fox=False, out=mo,
            **sample,
        )
        x1, gates = _post_attention(
            x, fo, mo, pre["gate"], pa[l], pb[l], wo[l], ln1_g[l][None, :], ln1_b[l][None, :], wr3, br, alpha
        )
        x = _moe(x1, gates, wg[l], wu[l], wd[l], ln2_g[l][None, :], ln2_b[l][None, :], alpha)

    heads = (FOX_HEADS, FOX_HEAD_DIM)
    k_p, v_p, logf_p, ckv_p, kr_p, k_s, v_s, logf_s, ckv_s, kr_s = states
    y_prompt = x[:n_p].reshape(bp, tp, d)[:, N_META:t_real]
    y_sample = x[n_p:n].reshape(bs, tn, d)
    return (
        y_prompt,
        y_sample,
        k_p[:, :, :t_real].reshape(depth, bp, t_real, *heads),
        v_p[:, :, :t_real].reshape(depth, bp, t_real, *heads),
        logf_p[:, :, :t_real],
        ckv_p[:, :, :t_real],
        kr_p[:, :, :t_real],
        k_s.reshape(depth, bs, tn, *heads),
        v_s.reshape(depth, bs, tn, *heads),
        logf_s,
        ckv_s,
        kr_s,
    )
```
